```python
import functools
import jax
import jax.numpy as jnp
from jax import lax
import numpy as np


D_MODEL = 2048
BATCH = 8
SEQ = 4096
DEPTH = 1

POOL_WIDTH = D_MODEL // 4
POOL_WINDOWS = (2, 4, 8, 16)
N_POOL_GROUPS = len(POOL_WINDOWS)
POOL_GROUP = POOL_WIDTH // N_POOL_GROUPS
RWKV_WIDTH = D_MODEL - POOL_WIDTH
HEAD_SIZE = 64
N_RWKV_HEADS = RWKV_WIDTH // HEAD_SIZE
DECAY_LORA = 64
AAA_LORA = 64
GATE_LORA = 224
RWKV_IN = 3 * RWKV_WIDTH + DECAY_LORA + AAA_LORA + GATE_LORA
IN_WIDTH = POOL_WIDTH + RWKV_IN
D_FF = 5632
MACARON_WEIGHT = 0.5
N_SUBLAYERS = 3
N_MOD = 3
NORM_EPS = 1e-6
LN_X_EPS = 1e-5 * HEAD_SIZE

kernel_name = 'hybrid_pool_rwkv7_macaron_block'


def rms_norm(x, gain):
    xf = x.astype(jnp.float32)
    y = xf * lax.rsqrt(jnp.mean(xf * xf, axis=-1, keepdims=True) + NORM_EPS)
    return (y * gain.astype(jnp.float32)).astype(x.dtype)


def modulate(h, shift, scale):
    return h * (1 + scale[:, None, :]) + shift[:, None, :]


def token_shift(p, mu):
    prev = jnp.pad(p, ((0, 0), (1, 0), (0, 0)))[:, :-1]
    return p + mu * (prev - p)


def swiglu(h, w_gate, w_up, w_down):
    return (jax.nn.silu(h @ w_gate) * (h @ w_up)) @ w_down


def multiscale_pool(u, pool_w, pool_scale):
    B, S, _ = u.shape
    cs = jnp.cumsum(u.astype(jnp.float32), axis=1)
    t = jnp.arange(1, S + 1, dtype=jnp.float32)
    outs = []
    for gi, win in enumerate(POOL_WINDOWS):
        lo, hi = gi * POOL_GROUP, (gi + 1) * POOL_GROUP
        c_g = cs[..., lo:hi]
        lagged = jnp.pad(c_g, ((0, 0), (win, 0), (0, 0)))[:, :S]
        count = jnp.minimum(t, float(win))[None, :, None]
        mean = (c_g - lagged) / count
        outs.append(mean.astype(u.dtype) - u[..., lo:hi])
    pooled = jnp.stack(outs, axis=2)
    mixed = jnp.einsum('bsgc,gcd->bsgd', pooled, pool_w)
    return mixed.reshape(B, S, POOL_WIDTH) * pool_scale


def rwkv7_time_mix(p, mu, w0, w2, a0, a2, g2, k_k, k_a, r_k, lnx_w, lnx_b):
    B, S, _ = p.shape
    H, N, R = N_RWKV_HEADS, HEAD_SIZE, RWKV_WIDTH
    f32 = jnp.float32
    p = token_shift(p, mu)
    r = p[..., :R]
    k = p[..., R:2 * R]
    v = p[..., 2 * R:3 * R]
    o = 3 * R
    xw = p[..., o:o + DECAY_LORA]
    o = o + DECAY_LORA
    xa = p[..., o:o + AAA_LORA]
    o = o + AAA_LORA
    xg = p[..., o:]
    w_log = -jax.nn.softplus(-(w0 + jnp.tanh(xw) @ w2)) - 0.5
    decay = jnp.exp(-jnp.exp(w_log.astype(f32)))
    a = jax.nn.sigmoid(a0 + xa @ a2)
    g = jax.nn.sigmoid(xg) @ g2
    kk = (k * k_k).astype(f32).reshape(B, S, H, N)
    kk = kk / jnp.maximum(jnp.linalg.norm(kk, axis=-1, keepdims=True), 1e-12)
    k = k * (1 + (a - 1) * k_a)
    r_h = r.astype(f32).reshape(B, S, H, N)
    k_h = k.astype(f32).reshape(B, S, H, N)
    v_h = v.astype(f32).reshape(B, S, H, N)
    a_h = a.astype(f32).reshape(B, S, H, N)
    w_h = decay.reshape(B, S, H, N)

    def step(state, inp):
        r_t, w_t, k_t, v_t, kk_t, a_t = inp
        sa = jnp.einsum('bhvk,bhk->bhv', state, -kk_t)
        state = (state * w_t[:, :, None, :]
                 + sa[..., None] * (kk_t * a_t)[:, :, None, :]
                 + v_t[..., None] * k_t[:, :, None, :])
        return state, jnp.einsum('bhvk,bhk->bhv', state, r_t)

    xs = tuple(jnp.moveaxis(t, 1, 0) for t in (r_h, w_h, k_h, v_h, kk, a_h))
    state0 = jnp.zeros((B, H, N, N), f32)
    _, y = lax.scan(step, state0, xs)
    y = jnp.moveaxis(y, 0, 1)
    mean = jnp.mean(y, axis=-1, keepdims=True)
    var = jnp.mean(jnp.square(y - mean), axis=-1, keepdims=True)
    y = ((y - mean) * lax.rsqrt(var + LN_X_EPS) * lnx_w.astype(f32).reshape(H, N)
         + lnx_b.astype(f32).reshape(H, N))
    bonus = jnp.sum(r_h * k_h * r_k.astype(f32), axis=-1, keepdims=True) * v_h
    y = (y + bonus).reshape(B, S, R).astype(p.dtype)
    return y * g


def hybrid_mixer(h, w_in, mu_shift, pool_w, pool_scale, w0, w2, a0, a2, g2,
                 k_k, k_a, r_k, lnx_w, lnx_b, w_out):
    p = h @ w_in
    y_pool = multiscale_pool(p[..., :POOL_WIDTH], pool_w, pool_scale)
    y_rwkv = rwkv7_time_mix(p[..., POOL_WIDTH:], mu_shift, w0, w2, a0, a2, g2,
                            k_k, k_a, r_k, lnx_w, lnx_b)
    return jnp.concatenate([y_pool, y_rwkv], axis=-1) @ w_out


def sandwich_sublayer(x, fn, gain_pre, gain_post, shift, scale, gate, weight):
    h = modulate(rms_norm(x, gain_pre), shift, scale)
    y = rms_norm(fn(h), gain_post)
    return x + weight * (1 + gate[:, None, :]) * y


def setup_inputs(seed: int = 0) -> dict:
    key = jax.random.key(seed)
    ks = jax.random.split(key, 32)
    f32 = jnp.float32
    L, D = DEPTH, D_MODEL

    def nrm(k, shape, scale):
        return jax.random.normal(k, shape, f32) * scale

    return {
        'x': nrm(ks[0], (BATCH, SEQ, D), 1.0),
        'c': nrm(ks[1], (BATCH, D), 1.0),
        'w_ada': nrm(ks[2], (L, D, N_SUBLAYERS * N_MOD * D), 0.1 * D ** -0.5),
        'b_ada': nrm(ks[3], (L, N_SUBLAYERS * N_MOD * D), 0.02),
        'norm_pre': 1.0 + nrm(ks[4], (L, N_SUBLAYERS, D), 0.05),
        'norm_post': 1.0 + nrm(ks[5], (L, N_SUBLAYERS, D), 0.05),
        'ffn1_w_gate': nrm(ks[6], (L, D, D_FF), D ** -0.5),
        'ffn1_w_up': nrm(ks[7], (L, D, D_FF), D ** -0.5),
        'ffn1_w_down': nrm(ks[8], (L, D_FF, D), D_FF ** -0.5),
        'w_in': nrm(ks[9], (L, D, IN_WIDTH), D ** -0.5),
        'mu_shift': jax.random.uniform(ks[10], (L, RWKV_IN), f32, 0.0, 1.0),
        'pool_w': nrm(ks[11], (L, N_POOL_GROUPS, POOL_GROUP, POOL_GROUP), POOL_GROUP ** -0.5),
        'pool_scale': 1.0 + nrm(ks[12], (L, POOL_WIDTH), 0.1),
        'w0': jax.random.uniform(ks[13], (L, RWKV_WIDTH), f32, -6.0, 1.0),
        'w2': nrm(ks[14], (L, DECAY_LORA, RWKV_WIDTH), 0.1 * DECAY_LORA ** -0.5),
        'a0': nrm(ks[15], (L, RWKV_WIDTH), 0.1),
        'a2': nrm(ks[16], (L, AAA_LORA, RWKV_WIDTH), 0.5 * AAA_LORA ** -0.5),
        'g2': nrm(ks[17], (L, GATE_LORA, RWKV_WIDTH), GATE_LORA ** -0.5),
        'k_k': 0.85 + nrm(ks[18], (L, RWKV_WIDTH), 0.05),
        'k_a': 1.0 + nrm(ks[19], (L, RWKV_WIDTH), 0.05),
        'r_k': nrm(ks[20], (L, N_RWKV_HEADS, HEAD_SIZE), 0.1),
        'lnx_w': 1.0 + nrm(ks[21], (L, RWKV_WIDTH), 0.05),
        'lnx_b': nrm(ks[22], (L, RWKV_WIDTH), 0.02),
        'w_out': nrm(ks[23], (L, D, D), D ** -0.5),
        'ffn2_w_gate': nrm(ks[24], (L, D, D_FF), D ** -0.5),
        'ffn2_w_up': nrm(ks[25], (L, D, D_FF), D ** -0.5),
        'ffn2_w_down': nrm(ks[26], (L, D_FF, D), D_FF ** -0.5),
    }


def reference(x, c, w_ada, b_ada, norm_pre, norm_post, ffn1_w_gate, ffn1_w_up, ffn1_w_down,
              w_in, mu_shift, pool_w, pool_scale, w0, w2, a0, a2, g2, k_k, k_a, r_k,
              lnx_w, lnx_b, w_out, ffn2_w_gate, ffn2_w_up, ffn2_w_down):
    B = x.shape[0]
    for l in range(DEPTH):
        mod = (jax.nn.silu(c) @ w_ada[l] + b_ada[l]).reshape(B, N_SUBLAYERS, N_MOD, D_MODEL)
        ffn1 = functools.partial(swiglu, w_gate=ffn1_w_gate[l], w_up=ffn1_w_up[l], w_down=ffn1_w_down[l])
        mixer = functools.partial(
            hybrid_mixer, w_in=w_in[l], mu_shift=mu_shift[l], pool_w=pool_w[l],
            pool_scale=pool_scale[l], w0=w0[l], w2=w2[l], a0=a0[l], a2=a2[l], g2=g2[l],
            k_k=k_k[l], k_a=k_a[l], r_k=r_k[l], lnx_w=lnx_w[l], lnx_b=lnx_b[l], w_out=w_out[l])
        ffn2 = functools.partial(swiglu, w_gate=ffn2_w_gate[l], w_up=ffn2_w_up[l], w_down=ffn2_w_down[l])
        x = sandwich_sublayer(x, ffn1, norm_pre[l, 0], norm_post[l, 0],
                              mod[:, 0, 0], mod[:, 0, 1], mod[:, 0, 2], MACARON_WEIGHT)
        x = sandwich_sublayer(x, mixer, norm_pre[l, 1], norm_post[l, 1],
                              mod[:, 1, 0], mod[:, 1, 1], mod[:, 1, 2], 1.0)
        x = sandwich_sublayer(x, ffn2, norm_pre[l, 2], norm_post[l, 2],
                              mod[:, 2, 0], mod[:, 2, 1], mod[:, 2, 2], MACARON_WEIGHT)
    return x
```

```python
import functools

import jax
import jax.numpy as jnp
from jax import lax
from jax.experimental import pallas as pl
from jax.experimental.pallas import tpu as pltpu

F32 = jnp.float32
BF16 = jnp.bfloat16

NORM_EPS = 1e-6
HEAD_SIZE = 64
LN_X_EPS = 1e-5 * HEAD_SIZE
POOL_WINDOWS = (2, 4, 8, 16)
POOL_GROUP = 128
MACARON_WEIGHT = 0.5
N_MOD = 3

LANES = 128
CHUNK = 64
VMEM_LIMIT = 56 * 1024 * 1024


def _cparams(sem):
    return pltpu.CompilerParams(dimension_semantics=sem, vmem_limit_bytes=VMEM_LIMIT)


def _sigmoid(z):
    return 1.0 / (1.0 + jnp.exp(-z))


def _rms(x, gain):
    ms = jnp.mean(x * x, axis=-1, keepdims=True)
    return x * lax.rsqrt(ms + NORM_EPS) * gain


def _norm_mod(x, gain, mod_ref, sub):
    shift = mod_ref[0, N_MOD * sub:N_MOD * sub + 1, :]
    scale = mod_ref[0, N_MOD * sub + 1:N_MOD * sub + 2, :]
    return _rms(x, gain) * (1.0 + scale) + shift


def _bdot(a, b):
    return jnp.dot(a.astype(BF16), b.astype(BF16), preferred_element_type=F32)


def _ada_kernel(c_ref, w_ref, b_ref, o_ref):
    c = c_ref[...]
    s = c * _sigmoid(c)
    o_ref[...] = jnp.dot(s, w_ref[...], preferred_element_type=F32,
                         precision=lax.Precision.HIGHEST) + b_ref[...]


def _ada(c, w, b, tn=1024):
    bsz, d = c.shape
    n = w.shape[1]
    return pl.pallas_call(
        _ada_kernel,
        grid=(n // tn,),
        in_specs=[pl.BlockSpec((bsz, d), lambda j: (0, 0)),
                  pl.BlockSpec((d, tn), lambda j: (0, j)),
                  pl.BlockSpec((1, tn), lambda j: (0, j))],
        out_specs=pl.BlockSpec((bsz, tn), lambda j: (0, j)),
        out_shape=jax.ShapeDtypeStruct((bsz, n), F32),
        compiler_params=_cparams(("arbitrary",)),
        name="ada",
    )(c, w, b.reshape(1, n))


def _ffn_kernel(x_ref, mod_ref, gpre_ref, gpost_ref, wg_ref, wu_ref, wd_ref, o_ref, h_ref, *, sub):
    j = pl.program_id(1)

    @pl.when(j == 0)
    def _():
        h_ref[...] = _norm_mod(x_ref[...], gpre_ref[...], mod_ref, sub).astype(BF16)
        o_ref[...] = jnp.zeros_like(o_ref)

    h = h_ref[...]
    g = jnp.dot(h, wg_ref[...], preferred_element_type=F32)
    u = jnp.dot(h, wu_ref[...], preferred_element_type=F32)
    act = (g * _sigmoid(g) * u).astype(BF16)
    o_ref[...] += jnp.dot(act, wd_ref[...], preferred_element_type=F32)

    @pl.when(j == pl.num_programs(1) - 1)
    def _():
        gate = mod_ref[0, N_MOD * sub + 2:N_MOD * sub + 3, :]
        y = _rms(o_ref[...], gpost_ref[...])
        o_ref[...] = x_ref[...] + MACARON_WEIGHT * (1.0 + gate) * y


def _ffn(x2, mod3, gpre, gpost, wg, wu, wd, *, sub, seq, tm=512, tf=512):
    t, d = x2.shape
    f = wg.shape[1]
    tps = seq // tm
    return pl.pallas_call(
        functools.partial(_ffn_kernel, sub=sub),
        grid=(t // tm, f // tf),
        in_specs=[pl.BlockSpec((tm, d), lambda i, j: (i, 0)),
                  pl.BlockSpec((1,) + mod3.shape[1:], lambda i, j: (i // tps, 0, 0)),
                  pl.BlockSpec((1, d), lambda i, j: (0, 0)),
                  pl.BlockSpec((1, d), lambda i, j: (0, 0)),
                  pl.BlockSpec((d, tf), lambda i, j: (0, j)),
                  pl.BlockSpec((d, tf), lambda i, j: (0, j)),
                  pl.BlockSpec((tf, d), lambda i, j: (j, 0))],
        out_specs=pl.BlockSpec((tm, d), lambda i, j: (i, 0)),
        out_shape=jax.ShapeDtypeStruct((t, d), F32),
        scratch_shapes=[pltpu.VMEM((tm, d), BF16)],
        compiler_params=_cparams(("arbitrary", "arbitrary")),
        name=f"ffn{sub}",
    )(x2, mod3, gpre, gpost, wg, wu, wd)


def _inproj_kernel(x_ref, mod_ref, gpre_ref, w_ref, mu_ref, o_ref, h_ref, carry_ref, *, tps):
    i = pl.program_id(0)
    j = pl.program_id(1)

    @pl.when(j == 0)
    def _():
        h_ref[...] = _norm_mod(x_ref[...], gpre_ref[...], mod_ref, 1).astype(BF16)

    res = jnp.dot(h_ref[...], w_ref[...], preferred_element_type=F32)
    tm, tn = res.shape
    carried = carry_ref[j][0:1, :]
    first = jnp.where(i % tps == 0, jnp.zeros_like(carried), carried)
    row = lax.broadcasted_iota(jnp.int32, (tm, 1), 0)
    prev = jnp.where(row == 0, first, pltpu.roll(res, 1, 0))
    carry_ref[j] = jnp.broadcast_to(res[tm - 1:tm, :], (8, tn))
    o_ref[...] = res + mu_ref[...] * (prev - res)


def _inproj(x2, mod3, gpre, w, mu, *, seq, tm=512, tn=512):
    t, d = x2.shape
    n = w.shape[1]
    tps = seq // tm
    return pl.pallas_call(
        functools.partial(_inproj_kernel, tps=tps),
        grid=(t // tm, n // tn),
        in_specs=[pl.BlockSpec((tm, d), lambda i, j: (i, 0)),
                  pl.BlockSpec((1,) + mod3.shape[1:], lambda i, j: (i // tps, 0, 0)),
                  pl.BlockSpec((1, d), lambda i, j: (0, 0)),
                  pl.BlockSpec((d, tn), lambda i, j: (0, j)),
                  pl.BlockSpec((1, tn), lambda i, j: (0, j))],
        out_specs=pl.BlockSpec((tm, tn), lambda i, j: (i, j)),
        out_shape=jax.ShapeDtypeStruct((t, n), F32),
        scratch_shapes=[pltpu.VMEM((tm, d), BF16), pltpu.VMEM((n // tn, 8, tn), F32)],
        compiler_params=_cparams(("arbitrary", "arbitrary")),
        name="inproj",
    )(x2, mod3, gpre, w, mu)


def _pool_kernel(u_ref, w_ref, s_ref, o_ref, ext_ref, *, tps):
    i = pl.program_id(0)
    tm = u_ref.shape[0]
    pad = max(POOL_WINDOWS)

    @pl.when(i % tps == 0)
    def _():
        ext_ref[0:pad, :] = jnp.zeros((pad, ext_ref.shape[1]), F32)

    ext_ref[pad:pad + tm, :] = u_ref[...]
    t_in_seq = (i % tps) * tm + lax.broadcasted_iota(jnp.int32, (tm, 1), 0)
    tpos = (t_in_seq + 1).astype(F32)
    for gi, win in enumerate(POOL_WINDOWS):
        lo = gi * POOL_GROUP
        acc = ext_ref[pad:pad + tm, lo:lo + POOL_GROUP]
        u_g = acc
        for dlt in range(1, win):
            acc = acc + ext_ref[pad - dlt:pad - dlt + tm, lo:lo + POOL_GROUP]
        pooled = acc / jnp.minimum(tpos, float(win)) - u_g
        mixed = _bdot(pooled, w_ref[gi])
        o_ref[:, lo:lo + POOL_GROUP] = mixed * s_ref[:, lo:lo + POOL_GROUP]
    ext_ref[0:pad, :] = ext_ref[tm:tm + pad, :]


def _pool(p_all, pool_w, pool_scale, *, seq, tm=512):
    t = p_all.shape[0]
    pw = pool_scale.shape[1]
    tps = seq // tm
    return pl.pallas_call(
        functools.partial(_pool_kernel, tps=tps),
        grid=(t // tm,),
        in_specs=[pl.BlockSpec((tm, pw), lambda i: (i, 0)),
                  pl.BlockSpec(pool_w.shape, lambda i: (0, 0, 0)),
                  pl.BlockSpec((1, pw), lambda i: (0, 0))],
        out_specs=pl.BlockSpec((tm, pw), lambda i: (i, 0)),
        out_shape=jax.ShapeDtypeStruct((t, pw), F32),
        scratch_shapes=[pltpu.VMEM((tm + max(POOL_WINDOWS), pw), F32)],
        compiler_params=_cparams(("arbitrary",)),
        name="pool",
    )(p_all, pool_w, pool_scale)


def _lora_kernel(p_ref, w2_ref, a2_ref, g2_ref, w0_ref, a0_ref, ld_ref, a_ref, g_ref):
    p = p_ref[...]
    z = w0_ref[...] + _bdot(jnp.tanh(p), w2_ref[...])
    w_log = -(jnp.maximum(-z, 0.0) + jnp.log(1.0 + jnp.exp(-jnp.abs(z)))) - 0.5
    ld_ref[...] = -jnp.exp(w_log)
    a_ref[...] = _sigmoid(a0_ref[...] + _bdot(p, a2_ref[...]))
    g_ref[...] = _bdot(_sigmoid(p), g2_ref[...])


def _lora(p_all, col_block, w2p, a2p, g2p, w0, a0, *, tm=512):
    t = p_all.shape[0]
    kw, r = w2p.shape
    full = lambda shape: pl.BlockSpec(shape, lambda i: (0, 0))
    out = jax.ShapeDtypeStruct((t, r), F32)
    return pl.pallas_call(
        _lora_kernel,
        grid=(t // tm,),
        in_specs=[pl.BlockSpec((tm, kw), lambda i: (i, col_block)),
                  full((kw, r)), full((kw, r)), full((kw, r)), full((1, r)), full((1, r))],
        out_specs=[pl.BlockSpec((tm, r), lambda i: (i, 0))] * 3,
        out_shape=[out, out, out],
        compiler_params=_cparams(("arbitrary",)),
        name="lora",
    )(p_all, w2p, a2p, g2p, w0, a0)


def _split_dot(x, m, parts):
    acc = None
    rem = x
    for _ in range(parts):
        piece = rem.astype(BF16)
        rem = rem - piece.astype(F32)
        term = jnp.dot(piece, m, preferred_element_type=F32)
        acc = term if acc is None else acc + term
    return acc


def _scan_kernel(r_ref, k_ref, v_ref, ld_ref, a_ref, g_ref, kkw_ref, ka_ref, rk_ref, lw_ref, lb_ref,
                 o_ref, h_ref):
    C = CHUNK
    P = LANES
    N = HEAD_SIZE

    @pl.when(pl.program_id(2) == 0)
    def _():
        h_ref[...] = jnp.zeros_like(h_ref)

    lane = lax.broadcasted_iota(jnp.int32, (1, P), 1)
    in_a = lane < N
    ri = lax.broadcasted_iota(jnp.int32, (2 * C, 2 * C), 0)
    ci = lax.broadcasted_iota(jnp.int32, (2 * C, 2 * C), 1)

    def same(b):
        sh = b.bit_length() - 1
        return (ri >> sh) == (ci >> sh)

    strict = (ri > ci) & same(C)
    incl = (ri >= ci) & same(C)
    eye = ri == ci
    blk16 = strict & same(16)
    blk32 = strict & same(32) & jnp.logical_not(same(16))
    blk64 = strict & jnp.logical_not(same(32))
    eye_f = eye.astype(F32)
    head_ones = same(N).astype(BF16)
    tri = (lax.broadcasted_iota(jnp.int32, (C, C), 0)
           >= lax.broadcasted_iota(jnp.int32, (C, C), 1)).astype(BF16)

    def stack(x):
        return jnp.concatenate([jnp.where(in_a, x, 0.0), jnp.where(in_a, 0.0, x)], axis=0)

    def headsum(x):
        return _split_dot(x, head_ones, 2)

    kkw = kkw_ref[...]
    ka = ka_ref[...]
    rk = rk_ref[...]
    lw = lw_ref[...]
    lb = lb_ref[...]

    for c in range(r_ref.shape[0] // C):
        sl = pl.ds(c * C, C)
        r = r_ref[sl, :]
        k = k_ref[sl, :]
        v = v_ref[sl, :]
        ld = ld_ref[sl, :]
        a = a_ref[sl, :]

        kkraw = k * kkw
        kmod = k * (1.0 + (a - 1.0) * ka)
        sums = headsum(jnp.concatenate([kkraw * kkraw, r * kmod * rk], axis=0))
        kk = kkraw / jnp.maximum(jnp.sqrt(sums[:C]), 1e-12)
        bonus = sums[C:]

        cum = jnp.dot(tri, ld.astype(BF16), preferred_element_type=F32)
        rem = ld - ld.astype(BF16).astype(F32)
        cum = cum + jnp.dot(tri, rem.astype(BF16), preferred_element_type=F32)
        rem = rem - rem.astype(BF16).astype(F32)
        cum = cum + jnp.dot(tri, rem.astype(BF16), preferred_element_type=F32)
        cum_last = cum[C - 1:C, :]

        inv_g = jnp.exp(-cum)
        to_end = jnp.exp(cum_last - cum)
        kb = kk * a
        rt = r * jnp.exp(cum)
        at = -kk * jnp.exp(cum - ld)
        bt = kb * inv_g
        kt = kmod * inv_g
        g_end = jnp.exp(cum_last)

        s_v = stack(v).astype(BF16)
        lhs = jnp.concatenate([stack(at), stack(rt)], axis=0).astype(BF16)
        rhs = jnp.concatenate([stack(bt), stack(kt)], axis=0).astype(BF16)
        scores = lax.dot_general(lhs, rhs, (((1,), (1,)), ((), ())), preferred_element_type=F32)
        s_ab = scores[:2 * C, :2 * C]
        s_ak = jnp.where(strict, scores[:2 * C, 2 * C:], 0.0)
        s_rb = jnp.where(incl, scores[2 * C:, :2 * C], 0.0)
        s_rk = jnp.where(incl, scores[2 * C:, 2 * C:], 0.0)

        x1 = jnp.where(blk16, s_ab, 0.0)
        x2 = jnp.dot(x1, x1, preferred_element_type=F32)
        x4 = jnp.dot(x2, x2, preferred_element_type=F32)
        x8 = jnp.dot(x4, x4, preferred_element_type=F32)
        tinv = eye_f + x1
        tinv = tinv + jnp.dot(tinv, x2, preferred_element_type=F32)
        tinv = tinv + jnp.dot(tinv, x4, preferred_element_type=F32)
        tinv = tinv + jnp.dot(tinv, x8, preferred_element_type=F32)
        for blk in (blk32, blk64):
            off = jnp.where(blk, s_ab, 0.0)
            tinv = tinv + jnp.dot(jnp.dot(tinv, off, preferred_element_type=F32), tinv,
                                  preferred_element_type=F32)

        h = h_ref[...]
        ar = _bdot(jnp.concatenate([at, rt], axis=0), h)
        rhs_sa = stack(ar[:C]) + jnp.dot(s_ak.astype(BF16), s_v, preferred_element_type=F32)
        s_sa = jnp.dot(tinv, rhs_sa, preferred_element_type=F32)
        sa_v = jnp.concatenate([s_sa.astype(BF16), s_v], axis=0)
        y_s = stack(ar[C:]) + jnp.dot(jnp.concatenate([s_rb, s_rk], axis=1).astype(BF16), sa_v,
                                      preferred_element_type=F32)
        y = y_s[:C] + y_s[C:]

        upd_l = jnp.concatenate([stack(kb * to_end), stack(kmod * to_end)], axis=0).astype(BF16)
        upd = lax.dot_general(upd_l, sa_v, (((0,), (0,)), ((), ())), preferred_element_type=F32)
        g_col = jnp.sum(jnp.where(eye, g_end, 0.0), axis=1, keepdims=True)
        h_ref[...] = h * g_col + upd

        mean = headsum(y) * (1.0 / N)
        dev = y - mean
        var = headsum(dev * dev) * (1.0 / N)
        yn = dev * lax.rsqrt(var + LN_X_EPS) * lw + lb
        o_ref[sl, :] = (yn + bonus * v) * g_ref[sl, :]


def _scan(p_all, ld, a, g, kkw, ka, rk, lw, lb, *, batch, seq, r_block, k_block, v_block, tb=256):
    t, rw = ld.shape
    n_pairs = rw // LANES
    tpb = seq // tb
    tok = lambda col0: pl.BlockSpec((tb, LANES), lambda b, p, s: (b * tpb + s, col0 + p))
    par = pl.BlockSpec((1, LANES), lambda b, p, s: (0, p))
    return pl.pallas_call(
        _scan_kernel,
        grid=(batch, n_pairs, tpb),
        in_specs=[tok(r_block), tok(k_block), tok(v_block), tok(0), tok(0), tok(0),
                  par, par, par, par, par],
        out_specs=tok(0),
        out_shape=jax.ShapeDtypeStruct((t, rw), F32),
        scratch_shapes=[pltpu.VMEM((LANES, LANES), F32)],
        compiler_params=_cparams(("arbitrary", "arbitrary", "arbitrary")),
        name="scan",
    )(p_all, p_all, p_all, ld, a, g, kkw, ka, rk, lw, lb)


def _outproj_kernel(x_ref, yp_ref, yr_ref, mod_ref, gpost_ref, wp_ref, wr_ref, o_ref):
    y = (jnp.dot(yp_ref[...].astype(BF16), wp_ref[...], preferred_element_type=F32)
         + jnp.dot(yr_ref[...].astype(BF16), wr_ref[...], preferred_element_type=F32))
    gate = mod_ref[0, N_MOD + 2:N_MOD + 3, :]
    o_ref[...] = x_ref[...] + (1.0 + gate) * _rms(y, gpost_ref[...])


def _outproj(x2, y_pool, y_rwkv, mod3, gpost, wp, wr, *, seq, tm=512):
    t, d = x2.shape
    tps = seq // tm
    return pl.pallas_call(
        _outproj_kernel,
        grid=(t // tm,),
        in_specs=[pl.BlockSpec((tm, d), lambda i: (i, 0)),
                  pl.BlockSpec((tm, y_pool.shape[1]), lambda i: (i, 0)),
                  pl.BlockSpec((tm, y_rwkv.shape[1]), lambda i: (i, 0)),
                  pl.BlockSpec((1,) + mod3.shape[1:], lambda i: (i // tps, 0, 0)),
                  pl.BlockSpec((1, d), lambda i: (0, 0)),
                  pl.BlockSpec(wp.shape, lambda i: (0, 0)),
                  pl.BlockSpec(wr.shape, lambda i: (0, 0))],
        out_specs=pl.BlockSpec((tm, d), lambda i: (i, 0)),
        out_shape=jax.ShapeDtypeStruct((t, d), F32),
        compiler_params=_cparams(("arbitrary",)),
        name="outproj",
    )(x2, y_pool, y_rwkv, mod3, gpost, wp, wr)


def _pad_cols(w, n):
    return jnp.pad(w, ((0, 0), (0, n - w.shape[1])))


def _layer(x2, c, batch, seq, w_ada, b_ada, norm_pre, norm_post, f1g, f1u, f1d, w_in, mu_shift, pool_w,
           pool_scale, w0, w2, a0, a2, g2, k_k, k_a, r_k, lnx_w, lnx_b, w_out, f2g, f2u, f2d):
    d = x2.shape[1]
    pool_width = pool_scale.shape[0]
    rw = w0.shape[0]
    n_sub = norm_pre.shape[0]
    col_tile = 512

    mod3 = _ada(c, w_ada, b_ada).reshape(batch, n_sub * N_MOD, d)
    row = lambda vec: vec.reshape(1, -1)

    x2 = _ffn(x2, mod3, row(norm_pre[0]), row(norm_post[0]), f1g.astype(BF16), f1u.astype(BF16),
              f1d.astype(BF16), sub=0, seq=seq)

    main = pool_width + 3 * rw
    n_lora = w_in.shape[1] - main
    assert main % col_tile == 0 and n_lora <= col_tile and pool_width == col_tile and rw % LANES == 0
    w_in_p = _pad_cols(w_in, main + col_tile).astype(BF16)
    mu_p = jnp.concatenate([jnp.zeros((pool_width,), F32), mu_shift, jnp.zeros((col_tile - n_lora,), F32)])
    p_all = _inproj(x2, mod3, row(norm_pre[1]), w_in_p, row(mu_p), seq=seq, tn=col_tile)

    y_pool = _pool(p_all, pool_w.astype(BF16), row(pool_scale), seq=seq)

    n_w, n_a = w2.shape[0], a2.shape[0]
    w2p = jnp.pad(w2, ((0, col_tile - n_w), (0, 0))).astype(BF16)
    a2p = jnp.pad(a2, ((n_w, col_tile - n_w - n_a), (0, 0))).astype(BF16)
    g2p = jnp.pad(g2, ((n_w + n_a, col_tile - n_lora), (0, 0))).astype(BF16)
    ld, a, g = _lora(p_all, main // col_tile, w2p, a2p, g2p, row(w0), row(a0))

    pb = pool_width // LANES
    y_rwkv = _scan(p_all, ld, a, g, row(k_k), row(k_a), row(r_k), row(lnx_w), row(lnx_b),
                   batch=batch, seq=seq, r_block=pb, k_block=pb + rw // LANES,
                   v_block=pb + 2 * rw // LANES)

    x2 = _outproj(x2, y_pool, y_rwkv, mod3, row(norm_post[1]), w_out[:pool_width].astype(BF16),
                  w_out[pool_width:].astype(BF16), seq=seq)

    x2 = _ffn(x2, mod3, row(norm_pre[2]), row(norm_post[2]), f2g.astype(BF16), f2u.astype(BF16),
              f2d.astype(BF16), sub=2, seq=seq)
    return x2


def kernel(x, c, w_ada, b_ada, norm_pre, norm_post, ffn1_w_gate, ffn1_w_up, ffn1_w_down, w_in, mu_shift,
           pool_w, pool_scale, w0, w2, a0, a2, g2, k_k, k_a, r_k, lnx_w, lnx_b, w_out, ffn2_w_gate,
           ffn2_w_up, ffn2_w_down):
    batch, seq, d = x.shape
    x2 = x.reshape(batch * seq, d)
    for l in range(w_ada.shape[0]):
        x2 = _layer(x2, c, batch, seq, w_ada[l], b_ada[l], norm_pre[l], norm_post[l], ffn1_w_gate[l],
                    ffn1_w_up[l], ffn1_w_down[l], w_in[l], mu_shift[l], pool_w[l], pool_scale[l], w0[l],
                    w2[l], a0[l], a2[l], g2[l], k_k[l], k_a[l], r_k[l], lnx_w[l], lnx_b[l], w_out[l],
                    ffn2_w_gate[l], ffn2_w_up[l], ffn2_w_down[l])
    return x2.reshape(batch, seq, d)
```

```python
import functools

import jax
import jax.numpy as jnp
from jax import lax
from jax.experimental import pallas as pl
from jax.experimental.pallas import tpu as pltpu

F32 = jnp.float32
BF16 = jnp.bfloat16

NORM_EPS = 1e-6
HEAD_SIZE = 64
LN_X_EPS = 1e-5 * HEAD_SIZE
POOL_WINDOWS = (2, 4, 8, 16)
POOL_GROUP = 128
MACARON_WEIGHT = 0.5
N_MOD = 3

LANES = 128
CHUNK = 64
VMEM_LIMIT = 56 * 1024 * 1024


def _cparams(sem):
    return pltpu.CompilerParams(dimension_semantics=sem, vmem_limit_bytes=VMEM_LIMIT)


def _sigmoid(z):
    return 1.0 / (1.0 + jnp.exp(-z))


def _rms(x, gain):
    ms = jnp.mean(x * x, axis=-1, keepdims=True)
    return x * lax.rsqrt(ms + NORM_EPS) * gain


def _norm_mod(x, gain, mod_ref, sub):
    shift = mod_ref[0, N_MOD * sub:N_MOD * sub + 1, :]
    scale = mod_ref[0, N_MOD * sub + 1:N_MOD * sub + 2, :]
    return _rms(x, gain) * (1.0 + scale) + shift


def _bdot(a, b):
    return jnp.dot(a.astype(BF16), b.astype(BF16), preferred_element_type=F32)


def _ada_kernel(c_ref, w_ref, b_ref, o_ref):
    c = c_ref[...]
    s = c * _sigmoid(c)
    o_ref[...] = jnp.dot(s, w_ref[...], preferred_element_type=F32,
                         precision=lax.Precision.HIGHEST) + b_ref[...]


def _ada(c, w, b, tn=1024):
    bsz, d = c.shape
    n = w.shape[1]
    return pl.pallas_call(
        _ada_kernel,
        grid=(n // tn,),
        in_specs=[pl.BlockSpec((bsz, d), lambda j: (0, 0)),
                  pl.BlockSpec((d, tn), lambda j: (0, j)),
                  pl.BlockSpec((1, tn), lambda j: (0, j))],
        out_specs=pl.BlockSpec((bsz, tn), lambda j: (0, j)),
        out_shape=jax.ShapeDtypeStruct((bsz, n), F32),
        compiler_params=_cparams(("arbitrary",)),
        name="ada",
    )(c, w, b.reshape(1, n))


def _ffn_kernel(x_ref, mod_ref, gpre_ref, gpost_ref, wg_ref, wu_ref, wd_ref, o_ref, h_ref, *, sub):
    j = pl.program_id(1)

    @pl.when(j == 0)
    def _():
        h_ref[...] = _norm_mod(x_ref[...], gpre_ref[...], mod_ref, sub).astype(BF16)
        o_ref[...] = jnp.zeros_like(o_ref)

    h = h_ref[...]
    g = jnp.dot(h, wg_ref[...], preferred_element_type=F32)
    u = jnp.dot(h, wu_ref[...], preferred_element_type=F32)
    act = (g * _sigmoid(g) * u).astype(BF16)
    o_ref[...] += jnp.dot(act, wd_ref[...], preferred_element_type=F32)

    @pl.when(j == pl.num_programs(1) - 1)
    def _():
        gate = mod_ref[0, N_MOD * sub + 2:N_MOD * sub + 3, :]
        y = _rms(o_ref[...], gpost_ref[...])
        o_ref[...] = x_ref[...] + MACARON_WEIGHT * (1.0 + gate) * y


def _ffn(x2, mod3, gpre, gpost, wg, wu, wd, *, sub, seq, tm=512, tf=512):
    t, d = x2.shape
    f = wg.shape[1]
    tps = seq // tm
    return pl.pallas_call(
        functools.partial(_ffn_kernel, sub=sub),
        grid=(t // tm, f // tf),
        in_specs=[pl.BlockSpec((tm, d), lambda i, j: (i, 0)),
                  pl.BlockSpec((1,) + mod3.shape[1:], lambda i, j: (i // tps, 0, 0)),
                  pl.BlockSpec((1, d), lambda i, j: (0, 0)),
                  pl.BlockSpec((1, d), lambda i, j: (0, 0)),
                  pl.BlockSpec((d, tf), lambda i, j: (0, j)),
                  pl.BlockSpec((d, tf), lambda i, j: (0, j)),
                  pl.BlockSpec((tf, d), lambda i, j: (j, 0))],
        out_specs=pl.BlockSpec((tm, d), lambda i, j: (i, 0)),
        out_shape=jax.ShapeDtypeStruct((t, d), F32),
        scratch_shapes=[pltpu.VMEM((tm, d), BF16)],
        compiler_params=_cparams(("arbitrary", "arbitrary")),
        name=f"ffn{sub}",
    )(x2, mod3, gpre, gpost, wg, wu, wd)


def _inproj_kernel(x_ref, mod_ref, gpre_ref, w_ref, mu_ref, o_ref, h_ref, carry_ref, *, tps):
    i = pl.program_id(0)
    j = pl.program_id(1)

    @pl.when(j == 0)
    def _():
        h_ref[...] = _norm_mod(x_ref[...], gpre_ref[...], mod_ref, 1).astype(BF16)

    res = jnp.dot(h_ref[...], w_ref[...], preferred_element_type=F32)
    tm, tn = res.shape
    carried = carry_ref[j][0:1, :]
    first = jnp.where(i % tps == 0, jnp.zeros_like(carried), carried)
    row = lax.broadcasted_iota(jnp.int32, (tm, 1), 0)
    prev = jnp.where(row == 0, first, pltpu.roll(res, 1, 0))
    carry_ref[j] = jnp.broadcast_to(res[tm - 1:tm, :], (8, tn))
    o_ref[...] = res + mu_ref[...] * (prev - res)


def _inproj(x2, mod3, gpre, w, mu, *, seq, tm=512, tn=512):
    t, d = x2.shape
    n = w.shape[1]
    tps = seq // tm
    return pl.pallas_call(
        functools.partial(_inproj_kernel, tps=tps),
        grid=(t // tm, n // tn),
        in_specs=[pl.BlockSpec((tm, d), lambda i, j: (i, 0)),
                  pl.BlockSpec((1,) + mod3.shape[1:], lambda i, j: (i // tps, 0, 0)),
                  pl.BlockSpec((1, d), lambda i, j: (0, 0)),
                  pl.BlockSpec((d, tn), lambda i, j: (0, j)),
                  pl.BlockSpec((1, tn), lambda i, j: (0, j))],
        out_specs=pl.BlockSpec((tm, tn), lambda i, j: (i, j)),
        out_shape=jax.ShapeDtypeStruct((t, n), F32),
        scratch_shapes=[pltpu.VMEM((tm, d), BF16), pltpu.VMEM((n // tn, 8, tn), F32)],
        compiler_params=_cparams(("arbitrary", "arbitrary")),
        name="inproj",
    )(x2, mod3, gpre, w, mu)


def _pool_kernel(u_ref, w_ref, s_ref, o_ref, ext_ref, *, tps):
    i = pl.program_id(0)
    tm = u_ref.shape[0]
    pad = max(POOL_WINDOWS)

    @pl.when(i % tps == 0)
    def _():
        ext_ref[0:pad, :] = jnp.zeros((pad, ext_ref.shape[1]), F32)

    ext_ref[pad:pad + tm, :] = u_ref[...]
    t_in_seq = (i % tps) * tm + lax.broadcasted_iota(jnp.int32, (tm, 1), 0)
    tpos = (t_in_seq + 1).astype(F32)
    for gi, win in enumerate(POOL_WINDOWS):
        lo = gi * POOL_GROUP
        acc = ext_ref[pad:pad + tm, lo:lo + POOL_GROUP]
        u_g = acc
        for dlt in range(1, win):
            acc = acc + ext_ref[pad - dlt:pad - dlt + tm, lo:lo + POOL_GROUP]
        pooled = acc / jnp.minimum(tpos, float(win)) - u_g
        mixed = _bdot(pooled, w_ref[gi])
        o_ref[:, lo:lo + POOL_GROUP] = mixed * s_ref[:, lo:lo + POOL_GROUP]
    ext_ref[0:pad, :] = ext_ref[tm:tm + pad, :]


def _pool(p_all, col_block, pool_w, pool_scale, *, seq, tm=512):
    t = p_all.shape[0]
    pw = pool_scale.shape[1]
    tps = seq // tm
    return pl.pallas_call(
        functools.partial(_pool_kernel, tps=tps),
        grid=(t // tm,),
        in_specs=[pl.BlockSpec((tm, pw), lambda i: (i, col_block)),
                  pl.BlockSpec(pool_w.shape, lambda i: (0, 0, 0)),
                  pl.BlockSpec((1, pw), lambda i: (0, 0))],
        out_specs=pl.BlockSpec((tm, pw), lambda i: (i, 0)),
        out_shape=jax.ShapeDtypeStruct((t, pw), F32),
        scratch_shapes=[pltpu.VMEM((tm + max(POOL_WINDOWS), pw), F32)],
        compiler_params=_cparams(("arbitrary",)),
        name="pool",
    )(p_all, pool_w, pool_scale)


def _lora_kernel(p_ref, w2_ref, a2_ref, g2_ref, w0_ref, a0_ref, ld_ref, a_ref, g_ref):
    p = p_ref[...]
    z = w0_ref[...] + _bdot(jnp.tanh(p), w2_ref[...])
    w_log = -(jnp.maximum(-z, 0.0) + jnp.log(1.0 + jnp.exp(-jnp.abs(z)))) - 0.5
    ld_ref[...] = -jnp.exp(w_log)
    a_ref[...] = _sigmoid(a0_ref[...] + _bdot(p, a2_ref[...]))
    g_ref[...] = _bdot(_sigmoid(p), g2_ref[...])


def _lora(p_all, col_block, w2p, a2p, g2p, w0, a0, *, tm=512):
    t = p_all.shape[0]
    kw, r = w2p.shape
    full = lambda shape: pl.BlockSpec(shape, lambda i: (0, 0))
    out = jax.ShapeDtypeStruct((t, r), F32)
    return pl.pallas_call(
        _lora_kernel,
        grid=(t // tm,),
        in_specs=[pl.BlockSpec((tm, kw), lambda i: (i, col_block)),
                  full((kw, r)), full((kw, r)), full((kw, r)), full((1, r)), full((1, r))],
        out_specs=[pl.BlockSpec((tm, r), lambda i: (i, 0))] * 3,
        out_shape=[out, out, out],
        compiler_params=_cparams(("arbitrary",)),
        name="lora",
    )(p_all, w2p, a2p, g2p, w0, a0)


def _split_dot(x, m, parts):
    acc = None
    rem = x
    for _ in range(parts):
        piece = rem.astype(BF16)
        rem = rem - piece.astype(F32)
        term = jnp.dot(piece, m, preferred_element_type=F32)
        acc = term if acc is None else acc + term
    return acc


def _scan_kernel(r_ref, k_ref, v_ref, ld_ref, a_ref, g_ref, kkw_ref, ka_ref, rk_ref, lw_ref, lb_ref,
                 o_ref, h_ref):
    C = CHUNK
    P = LANES
    N = HEAD_SIZE

    @pl.when(pl.program_id(2) == 0)
    def _():
        h_ref[...] = jnp.zeros_like(h_ref)

    lane = lax.broadcasted_iota(jnp.int32, (1, P), 1)
    in_a = lane < N
    ri = lax.broadcasted_iota(jnp.int32, (2 * C, 2 * C), 0)
    ci = lax.broadcasted_iota(jnp.int32, (2 * C, 2 * C), 1)

    def same(b):
        sh = b.bit_length() - 1
        return (ri >> sh) == (ci >> sh)

    strict = (ri > ci) & same(C)
    incl = (ri >= ci) & same(C)
    eye = ri == ci
    blk16 = strict & same(16)
    blk32 = strict & same(32) & jnp.logical_not(same(16))
    blk64 = strict & jnp.logical_not(same(32))
    eye_f = eye.astype(F32)
    head_ones = same(N).astype(BF16)
    tri = (lax.broadcasted_iota(jnp.int32, (C, C), 0)
           >= lax.broadcasted_iota(jnp.int32, (C, C), 1)).astype(BF16)

    def stack(x):
        return jnp.concatenate([jnp.where(in_a, x, 0.0), jnp.where(in_a, 0.0, x)], axis=0)

    def headsum(x):
        return _split_dot(x, head_ones, 2)

    def pair_chunk(c, q):
        sl = pl.ds(c * C, C)
        ln = pl.ds(q * P, P)
        kkw = kkw_ref[:, ln]
        ka = ka_ref[:, ln]
        rk = rk_ref[:, ln]
        lw = lw_ref[:, ln]
        lb = lb_ref[:, ln]
        r = r_ref[sl, ln]
        k = k_ref[sl, ln]
        v = v_ref[sl, ln]
        ld = ld_ref[sl, ln]
        a = a_ref[sl, ln]

        kkraw = k * kkw
        kmod = k * (1.0 + (a - 1.0) * ka)
        sums = headsum(jnp.concatenate([kkraw * kkraw, r * kmod * rk], axis=0))
        cum = jnp.dot(tri, ld.astype(BF16), preferred_element_type=F32)
        rem = ld - ld.astype(BF16).astype(F32)
        cum = cum + jnp.dot(tri, rem.astype(BF16), preferred_element_type=F32)
        rem = rem - rem.astype(BF16).astype(F32)
        cum = cum + jnp.dot(tri, rem.astype(BF16), preferred_element_type=F32)
        yield
        kk = kkraw / jnp.maximum(jnp.sqrt(sums[:C]), 1e-12)
        bonus = sums[C:]
        cum_last = cum[C - 1:C, :]

        inv_g = jnp.exp(-cum)
        to_end = jnp.exp(cum_last - cum)
        kb = kk * a
        rt = r * jnp.exp(cum)
        at = -kk * jnp.exp(cum - ld)
        bt = kb * inv_g
        kt = kmod * inv_g
        g_end = jnp.exp(cum_last)

        s_v = stack(v).astype(BF16)
        lhs = jnp.concatenate([stack(at), stack(rt)], axis=0).astype(BF16)
        rhs = jnp.concatenate([stack(bt), stack(kt)], axis=0).astype(BF16)
        scores = lax.dot_general(lhs, rhs, (((1,), (1,)), ((), ())), preferred_element_type=F32)
        h = h_ref[q]
        ar = _bdot(jnp.concatenate([at, rt], axis=0), h)
        yield
        s_ab = scores[:2 * C, :2 * C]
        s_ak = jnp.where(strict, scores[:2 * C, 2 * C:], 0.0)
        s_rb = jnp.where(incl, scores[2 * C:, :2 * C], 0.0)
        s_rk = jnp.where(incl, scores[2 * C:, 2 * C:], 0.0)

        x1 = jnp.where(blk16, s_ab, 0.0)
        x2 = jnp.dot(x1, x1, preferred_element_type=F32)
        rhs_sa = stack(ar[:C]) + jnp.dot(s_ak.astype(BF16), s_v, preferred_element_type=F32)
        yield
        x4 = jnp.dot(x2, x2, preferred_element_type=F32)
        tinv = eye_f + x1
        tinv = tinv + jnp.dot(tinv, x2, preferred_element_type=F32)
        yield
        x8 = jnp.dot(x4, x4, preferred_element_type=F32)
        tinv = tinv + jnp.dot(tinv, x4, preferred_element_type=F32)
        yield
        tinv = tinv + jnp.dot(tinv, x8, preferred_element_type=F32)
        yield
        for blk in (blk32, blk64):
            off = jnp.where(blk, s_ab, 0.0)
            part = jnp.dot(tinv, off, preferred_element_type=F32)
            yield
            tinv = tinv + jnp.dot(part, tinv, preferred_element_type=F32)
            yield

        s_sa = jnp.dot(tinv, rhs_sa, preferred_element_type=F32)
        yield
        sa_v = jnp.concatenate([s_sa.astype(BF16), s_v], axis=0)
        y_s = stack(ar[C:]) + jnp.dot(jnp.concatenate([s_rb, s_rk], axis=1).astype(BF16), sa_v,
                                      preferred_element_type=F32)
        upd_l = jnp.concatenate([stack(kb * to_end), stack(kmod * to_end)], axis=0).astype(BF16)
        upd = lax.dot_general(upd_l, sa_v, (((0,), (0,)), ((), ())), preferred_element_type=F32)
        yield
        y = y_s[:C] + y_s[C:]
        g_col = jnp.sum(jnp.where(eye, g_end, 0.0), axis=1, keepdims=True)
        h_ref[q] = h * g_col + upd

        mean = headsum(y) * (1.0 / N)
        yield
        dev = y - mean
        var = headsum(dev * dev) * (1.0 / N)
        yield
        yn = dev * lax.rsqrt(var + LN_X_EPS) * lw + lb
        o_ref[sl, ln] = (yn + bonus * v) * g_ref[sl, ln]

    for c in range(r_ref.shape[0] // C):
        live = [pair_chunk(c, q) for q in range(r_ref.shape[1] // P)]
        while live:
            advanced = []
            for gen in live:
                try:
                    next(gen)
                    advanced.append(gen)
                except StopIteration:
                    pass
            live = advanced


def _scan(p_all, ld, a, g, kkw, ka, rk, lw, lb, *, batch, seq, r_block, k_block, v_block, tb=CHUNK, pairs=12):
    t, rw = ld.shape
    width = pairs * LANES
    tpb = seq // tb
    tok = lambda blk0: pl.BlockSpec((tb, width), lambda b, p, s: (b * tpb + s, blk0 // pairs + p))
    par = pl.BlockSpec((1, width), lambda b, p, s: (0, p))
    return pl.pallas_call(
        _scan_kernel,
        grid=(batch, rw // width, tpb),
        in_specs=[tok(r_block), tok(k_block), tok(v_block), tok(0), tok(0), tok(0),
                  par, par, par, par, par],
        out_specs=tok(0),
        out_shape=jax.ShapeDtypeStruct((t, rw), F32),
        scratch_shapes=[pltpu.VMEM((pairs, LANES, LANES), F32)],
        compiler_params=_cparams(("arbitrary", "arbitrary", "arbitrary")),
        name="scan",
    )(p_all, p_all, p_all, ld, a, g, kkw, ka, rk, lw, lb)


def _outproj_kernel(x_ref, yp_ref, yr_ref, mod_ref, gpost_ref, wp_ref, wr_ref, o_ref):
    y = (jnp.dot(yp_ref[...].astype(BF16), wp_ref[...], preferred_element_type=F32)
         + jnp.dot(yr_ref[...].astype(BF16), wr_ref[...], preferred_element_type=F32))
    gate = mod_ref[0, N_MOD + 2:N_MOD + 3, :]
    o_ref[...] = x_ref[...] + (1.0 + gate) * _rms(y, gpost_ref[...])


def _outproj(x2, y_pool, y_rwkv, mod3, gpost, wp, wr, *, seq, tm=512):
    t, d = x2.shape
    tps = seq // tm
    return pl.pallas_call(
        _outproj_kernel,
        grid=(t // tm,),
        in_specs=[pl.BlockSpec((tm, d), lambda i: (i, 0)),
                  pl.BlockSpec((tm, y_pool.shape[1]), lambda i: (i, 0)),
                  pl.BlockSpec((tm, y_rwkv.shape[1]), lambda i: (i, 0)),
                  pl.BlockSpec((1,) + mod3.shape[1:], lambda i: (i // tps, 0, 0)),
                  pl.BlockSpec((1, d), lambda i: (0, 0)),
                  pl.BlockSpec(wp.shape, lambda i: (0, 0)),
                  pl.BlockSpec(wr.shape, lambda i: (0, 0))],
        out_specs=pl.BlockSpec((tm, d), lambda i: (i, 0)),
        out_shape=jax.ShapeDtypeStruct((t, d), F32),
        compiler_params=_cparams(("arbitrary",)),
        name="outproj",
    )(x2, y_pool, y_rwkv, mod3, gpost, wp, wr)


def _pad_cols(w, n):
    return jnp.pad(w, ((0, 0), (0, n - w.shape[1])))


def _layer(x2, c, batch, seq, w_ada, b_ada, norm_pre, norm_post, f1g, f1u, f1d, w_in, mu_shift, pool_w,
           pool_scale, w0, w2, a0, a2, g2, k_k, k_a, r_k, lnx_w, lnx_b, w_out, f2g, f2u, f2d):
    d = x2.shape[1]
    pool_width = pool_scale.shape[0]
    rw = w0.shape[0]
    n_sub = norm_pre.shape[0]
    col_tile = 512

    mod3 = _ada(c, w_ada, b_ada).reshape(batch, n_sub * N_MOD, d)
    row = lambda vec: vec.reshape(1, -1)

    x2 = _ffn(x2, mod3, row(norm_pre[0]), row(norm_post[0]), f1g.astype(BF16), f1u.astype(BF16),
              f1d.astype(BF16), sub=0, seq=seq)

    n_lora = w_in.shape[1] - pool_width - 3 * rw
    assert (3 * rw) % col_tile == 0 and n_lora <= col_tile and pool_width == col_tile and rw % LANES == 0
    w_in_p = jnp.concatenate([w_in[:, pool_width:pool_width + 3 * rw],
                              _pad_cols(w_in[:, pool_width + 3 * rw:], col_tile),
                              w_in[:, :pool_width]], axis=1).astype(BF16)
    mu_p = jnp.concatenate([mu_shift, jnp.zeros((col_tile - n_lora + pool_width,), F32)])
    p_all = _inproj(x2, mod3, row(norm_pre[1]), w_in_p, row(mu_p), seq=seq, tn=col_tile)
    lora_block = 3 * rw // col_tile

    y_pool = _pool(p_all, lora_block + 1, pool_w.astype(BF16), row(pool_scale), seq=seq)

    n_w, n_a = w2.shape[0], a2.shape[0]
    w2p = jnp.pad(w2, ((0, col_tile - n_w), (0, 0))).astype(BF16)
    a2p = jnp.pad(a2, ((n_w, col_tile - n_w - n_a), (0, 0))).astype(BF16)
    g2p = jnp.pad(g2, ((n_w + n_a, col_tile - n_lora), (0, 0))).astype(BF16)
    ld, a, g = _lora(p_all, lora_block, w2p, a2p, g2p, row(w0), row(a0))

    y_rwkv = _scan(p_all, ld, a, g, row(k_k), row(k_a), row(r_k), row(lnx_w), row(lnx_b),
                   batch=batch, seq=seq, r_block=0, k_block=rw // LANES, v_block=2 * rw // LANES)

    x2 = _outproj(x2, y_pool, y_rwkv, mod3, row(norm_post[1]), w_out[:pool_width].astype(BF16),
                  w_out[pool_width:].astype(BF16), seq=seq)

    x2 = _ffn(x2, mod3, row(norm_pre[2]), row(norm_post[2]), f2g.astype(BF16), f2u.astype(BF16),
              f2d.astype(BF16), sub=2, seq=seq)
    return x2


def kernel(x, c, w_ada, b_ada, norm_pre, norm_post, ffn1_w_gate, ffn1_w_up, ffn1_w_down, w_in, mu_shift,
           pool_w, pool_scale, w0, w2, a0, a2, g2, k_k, k_a, r_k, lnx_w, lnx_b, w_out, ffn2_w_gate,
           ffn2_w_up, ffn2_w_down):
    batch, seq, d = x.shape
    x2 = x.reshape(batch * seq, d)
    for l in range(w_ada.shape[0]):
        x2 = _layer(x2, c, batch, seq, w_ada[l], b_ada[l], norm_pre[l], norm_post[l], ffn1_w_gate[l],
                    ffn1_w_up[l], ffn1_w_down[l], w_in[l], mu_shift[l], pool_w[l], pool_scale[l], w0[l],
                    w2[l], a0[l], a2[l], g2[l], k_k[l], k_a[l], r_k[l], lnx_w[l], lnx_b[l], w_out[l],
                    ffn2_w_gate[l], ffn2_w_up[l], ffn2_w_down[l])
    return x2.reshape(batch, seq, d)
```

```python
import functools

import jax
import jax.numpy as jnp
from jax import lax
from jax.experimental import pallas as pl
from jax.experimental.pallas import tpu as pltpu

F32 = jnp.float32
BF16 = jnp.bfloat16

NORM_EPS = 1e-6
HEAD_SIZE = 64
LN_X_EPS = 1e-5 * HEAD_SIZE
POOL_WINDOWS = (2, 4, 8, 16)
POOL_GROUP = 128
MACARON_WEIGHT = 0.5
N_MOD = 3

LANES = 128
CHUNK = 64
VMEM_LIMIT = 56 * 1024 * 1024


def _cparams(sem):
    return pltpu.CompilerParams(dimension_semantics=sem, vmem_limit_bytes=VMEM_LIMIT)


def _sigmoid(z):
    return 1.0 / (1.0 + jnp.exp(-z))


def _rms(x, gain):
    ms = jnp.mean(x * x, axis=-1, keepdims=True)
    return x * lax.rsqrt(ms + NORM_EPS) * gain


def _norm_mod(x, gain, mod_ref, sub):
    shift = mod_ref[0, N_MOD * sub:N_MOD * sub + 1, :]
    scale = mod_ref[0, N_MOD * sub + 1:N_MOD * sub + 2, :]
    return _rms(x, gain) * (1.0 + scale) + shift


def _bdot(a, b):
    return jnp.dot(a.astype(BF16), b.astype(BF16), preferred_element_type=F32)


def _ada_kernel(c_ref, w_ref, b_ref, o_ref):
    c = c_ref[...]
    s = c * _sigmoid(c)
    o_ref[...] = jnp.dot(s, w_ref[...], preferred_element_type=F32,
                         precision=lax.Precision.HIGHEST) + b_ref[...]


def _ada(c, w, b, tn=1024):
    bsz, d = c.shape
    n = w.shape[1]
    return pl.pallas_call(
        _ada_kernel,
        grid=(n // tn,),
        in_specs=[pl.BlockSpec((bsz, d), lambda j: (0, 0)),
                  pl.BlockSpec((d, tn), lambda j: (0, j)),
                  pl.BlockSpec((1, tn), lambda j: (0, j))],
        out_specs=pl.BlockSpec((bsz, tn), lambda j: (0, j)),
        out_shape=jax.ShapeDtypeStruct((bsz, n), F32),
        compiler_params=_cparams(("arbitrary",)),
        name="ada",
    )(c, w, b.reshape(1, n))


def _ffn_kernel(x_ref, mod_ref, gpre_ref, gpost_ref, wg_ref, wu_ref, wd_ref, o_ref, h_ref, act_ref, *, sub):
    j = pl.program_id(1)
    nf = act_ref.shape[0]
    tn = wd_ref.shape[1]
    nn = o_ref.shape[1] // tn

    @pl.when(j == 0)
    def _():
        h_ref[...] = _norm_mod(x_ref[...], gpre_ref[...], mod_ref, sub).astype(BF16)

    @pl.when(j < nf)
    def _():
        h = h_ref[...]
        g = jnp.dot(h, wg_ref[...], preferred_element_type=F32)
        u = jnp.dot(h, wu_ref[...], preferred_element_type=F32)
        act_ref[j] = (g * _sigmoid(g) * u).astype(BF16)

    for n in range(nn):
        @pl.when(j == nf + n)
        def _(n=n):
            act = jnp.concatenate([act_ref[c] for c in range(nf)], axis=1)
            o_ref[:, n * tn:(n + 1) * tn] = jnp.dot(act, wd_ref[...], preferred_element_type=F32)

    @pl.when(j == nf + nn - 1)
    def _():
        gate = mod_ref[0, N_MOD * sub + 2:N_MOD * sub + 3, :]
        y = _rms(o_ref[...], gpost_ref[...])
        o_ref[...] = x_ref[...] + MACARON_WEIGHT * (1.0 + gate) * y


def _ffn(x2, mod3, gpre, gpost, wg, wu, wd, *, sub, seq, tm=512, tf=512, tn=512):
    t, d = x2.shape
    f = wg.shape[1]
    tps = seq // tm
    nf, nn = f // tf, d // tn
    up = lambda i, j: (0, jnp.minimum(j, nf - 1))
    down = lambda i, j: (0, jnp.clip(j - nf, 0, nn - 1))
    return pl.pallas_call(
        functools.partial(_ffn_kernel, sub=sub),
        grid=(t // tm, nf + nn),
        in_specs=[pl.BlockSpec((tm, d), lambda i, j: (i, 0)),
                  pl.BlockSpec((1,) + mod3.shape[1:], lambda i, j: (i // tps, 0, 0)),
                  pl.BlockSpec((1, d), lambda i, j: (0, 0)),
                  pl.BlockSpec((1, d), lambda i, j: (0, 0)),
                  pl.BlockSpec((d, tf), up),
                  pl.BlockSpec((d, tf), up),
                  pl.BlockSpec((f, tn), down)],
        out_specs=pl.BlockSpec((tm, d), lambda i, j: (i, 0)),
        out_shape=jax.ShapeDtypeStruct((t, d), F32),
        scratch_shapes=[pltpu.VMEM((tm, d), BF16), pltpu.VMEM((nf, tm, tf), BF16)],
        compiler_params=_cparams(("arbitrary", "arbitrary")),
        name=f"ffn{sub}",
    )(x2, mod3, gpre, gpost, wg, wu, wd)


def _inproj_kernel(x_ref, mod_ref, gpre_ref, w_ref, mu_ref, o_ref, h_ref, carry_ref, *, tps):
    i = pl.program_id(0)
    j = pl.program_id(1)

    @pl.when(j == 0)
    def _():
        h_ref[...] = _norm_mod(x_ref[...], gpre_ref[...], mod_ref, 1).astype(BF16)

    res = jnp.dot(h_ref[...], w_ref[...], preferred_element_type=F32)
    tm, tn = res.shape
    carried = carry_ref[j][0:1, :]
    first = jnp.where(i % tps == 0, jnp.zeros_like(carried), carried)
    row = lax.broadcasted_iota(jnp.int32, (tm, 1), 0)
    prev = jnp.where(row == 0, first, pltpu.roll(res, 1, 0))
    carry_ref[j] = jnp.broadcast_to(res[tm - 1:tm, :], (8, tn))
    o_ref[...] = res + mu_ref[...] * (prev - res)


def _inproj(x2, mod3, gpre, w, mu, *, seq, tm=512, tn=512):
    t, d = x2.shape
    n = w.shape[1]
    tps = seq // tm
    return pl.pallas_call(
        functools.partial(_inproj_kernel, tps=tps),
        grid=(t // tm, n // tn),
        in_specs=[pl.BlockSpec((tm, d), lambda i, j: (i, 0)),
                  pl.BlockSpec((1,) + mod3.shape[1:], lambda i, j: (i // tps, 0, 0)),
                  pl.BlockSpec((1, d), lambda i, j: (0, 0)),
                  pl.BlockSpec((d, tn), lambda i, j: (0, j)),
                  pl.BlockSpec((1, tn), lambda i, j: (0, j))],
        out_specs=pl.BlockSpec((tm, tn), lambda i, j: (i, j)),
        out_shape=jax.ShapeDtypeStruct((t, n), F32),
        scratch_shapes=[pltpu.VMEM((tm, d), BF16), pltpu.VMEM((n // tn, 8, tn), F32)],
        compiler_params=_cparams(("arbitrary", "arbitrary")),
        name="inproj",
    )(x2, mod3, gpre, w, mu)


def _pool_kernel(u_ref, w_ref, s_ref, o_ref, ext_ref, *, tps):
    i = pl.program_id(0)
    tm = u_ref.shape[0]
    pad = max(POOL_WINDOWS)

    @pl.when(i % tps == 0)
    def _():
        ext_ref[0:pad, :] = jnp.zeros((pad, ext_ref.shape[1]), F32)

    ext_ref[pad:pad + tm, :] = u_ref[...]
    t_in_seq = (i % tps) * tm + lax.broadcasted_iota(jnp.int32, (tm, 1), 0)
    tpos = (t_in_seq + 1).astype(F32)
    for gi, win in enumerate(POOL_WINDOWS):
        lo = gi * POOL_GROUP
        acc = ext_ref[pad:pad + tm, lo:lo + POOL_GROUP]
        u_g = acc
        for dlt in range(1, win):
            acc = acc + ext_ref[pad - dlt:pad - dlt + tm, lo:lo + POOL_GROUP]
        pooled = acc / jnp.minimum(tpos, float(win)) - u_g
        mixed = _bdot(pooled, w_ref[gi])
        o_ref[:, lo:lo + POOL_GROUP] = mixed * s_ref[:, lo:lo + POOL_GROUP]
    ext_ref[0:pad, :] = ext_ref[tm:tm + pad, :]


def _pool(p_all, col_block, pool_w, pool_scale, *, seq, tm=512):
    t = p_all.shape[0]
    pw = pool_scale.shape[1]
    tps = seq // tm
    return pl.pallas_call(
        functools.partial(_pool_kernel, tps=tps),
        grid=(t // tm,),
        in_specs=[pl.BlockSpec((tm, pw), lambda i: (i, col_block)),
                  pl.BlockSpec(pool_w.shape, lambda i: (0, 0, 0)),
                  pl.BlockSpec((1, pw), lambda i: (0, 0))],
        out_specs=pl.BlockSpec((tm, pw), lambda i: (i, 0)),
        out_shape=jax.ShapeDtypeStruct((t, pw), F32),
        scratch_shapes=[pltpu.VMEM((tm + max(POOL_WINDOWS), pw), F32)],
        compiler_params=_cparams(("arbitrary",)),
        name="pool",
    )(p_all, pool_w, pool_scale)


def _lora_kernel(p_ref, w2_ref, a2_ref, g2_ref, w0_ref, a0_ref, ld_ref, a_ref, g_ref):
    p = p_ref[...]
    z = w0_ref[...] + _bdot(jnp.tanh(p), w2_ref[...])
    w_log = -(jnp.maximum(-z, 0.0) + jnp.log(1.0 + jnp.exp(-jnp.abs(z)))) - 0.5
    ld_ref[...] = -jnp.exp(w_log)
    a_ref[...] = _sigmoid(a0_ref[...] + _bdot(p, a2_ref[...]))
    g_ref[...] = _bdot(_sigmoid(p), g2_ref[...])


def _lora(p_all, col_block, w2p, a2p, g2p, w0, a0, *, tm=512):
    t = p_all.shape[0]
    kw, r = w2p.shape
    full = lambda shape: pl.BlockSpec(shape, lambda i: (0, 0))
    out = jax.ShapeDtypeStruct((t, r), F32)
    return pl.pallas_call(
        _lora_kernel,
        grid=(t // tm,),
        in_specs=[pl.BlockSpec((tm, kw), lambda i: (i, col_block)),
                  full((kw, r)), full((kw, r)), full((kw, r)), full((1, r)), full((1, r))],
        out_specs=[pl.BlockSpec((tm, r), lambda i: (i, 0))] * 3,
        out_shape=[out, out, out],
        compiler_params=_cparams(("arbitrary",)),
        name="lora",
    )(p_all, w2p, a2p, g2p, w0, a0)


def _split_dot(x, m, parts):
    acc = None
    rem = x
    for _ in range(parts):
        piece = rem.astype(BF16)
        rem = rem - piece.astype(F32)
        term = jnp.dot(piece, m, preferred_element_type=F32)
        acc = term if acc is None else acc + term
    return acc


def _scan_kernel(r_ref, k_ref, v_ref, ld_ref, a_ref, g_ref, kkw_ref, ka_ref, rk_ref, lw_ref, lb_ref,
                 o_ref, h_ref):
    C = CHUNK
    P = LANES
    N = HEAD_SIZE

    @pl.when(pl.program_id(2) == 0)
    def _():
        h_ref[...] = jnp.zeros_like(h_ref)

    lane = lax.broadcasted_iota(jnp.int32, (1, P), 1)
    in_a = lane < N
    ri = lax.broadcasted_iota(jnp.int32, (2 * C, 2 * C), 0)
    ci = lax.broadcasted_iota(jnp.int32, (2 * C, 2 * C), 1)

    def same(b):
        sh = b.bit_length() - 1
        return (ri >> sh) == (ci >> sh)

    strict = (ri > ci) & same(C)
    incl = (ri >= ci) & same(C)
    eye = ri == ci
    blk16 = strict & same(16)
    blk32 = strict & same(32) & jnp.logical_not(same(16))
    blk64 = strict & jnp.logical_not(same(32))
    eye_f = eye.astype(F32)
    head_ones = same(N).astype(BF16)
    tri = (lax.broadcasted_iota(jnp.int32, (C, C), 0)
           >= lax.broadcasted_iota(jnp.int32, (C, C), 1)).astype(BF16)

    def stack(x):
        return jnp.concatenate([jnp.where(in_a, x, 0.0), jnp.where(in_a, 0.0, x)], axis=0)

    def headsum(x):
        return _split_dot(x, head_ones, 2)

    def pair_chunk(c, q):
        sl = pl.ds(c * C, C)
        ln = pl.ds(q * P, P)
        kkw = kkw_ref[:, ln]
        ka = ka_ref[:, ln]
        rk = rk_ref[:, ln]
        lw = lw_ref[:, ln]
        lb = lb_ref[:, ln]
        r = r_ref[sl, ln]
        k = k_ref[sl, ln]
        v = v_ref[sl, ln]
        ld = ld_ref[sl, ln]
        a = a_ref[sl, ln]

        kkraw = k * kkw
        kmod = k * (1.0 + (a - 1.0) * ka)
        sums = headsum(jnp.concatenate([kkraw * kkraw, r * kmod * rk], axis=0))
        cum = jnp.dot(tri, ld.astype(BF16), preferred_element_type=F32)
        rem = ld - ld.astype(BF16).astype(F32)
        cum = cum + jnp.dot(tri, rem.astype(BF16), preferred_element_type=F32)
        rem = rem - rem.astype(BF16).astype(F32)
        cum = cum + jnp.dot(tri, rem.astype(BF16), preferred_element_type=F32)
        yield
        kk = kkraw / jnp.maximum(jnp.sqrt(sums[:C]), 1e-12)
        bonus = sums[C:]
        cum_last = cum[C - 1:C, :]

        inv_g = jnp.exp(-cum)
        to_end = jnp.exp(cum_last - cum)
        kb = kk * a
        rt = r * jnp.exp(cum)
        at = -kk * jnp.exp(cum - ld)
        bt = kb * inv_g
        kt = kmod * inv_g
        g_end = jnp.exp(cum_last)

        s_v = stack(v).astype(BF16)
        lhs = jnp.concatenate([stack(at), stack(rt)], axis=0).astype(BF16)
        rhs = jnp.concatenate([stack(bt), stack(kt)], axis=0).astype(BF16)
        scores = lax.dot_general(lhs, rhs, (((1,), (1,)), ((), ())), preferred_element_type=F32)
        h = h_ref[q]
        ar = _bdot(jnp.concatenate([at, rt], axis=0), h)
        yield
        s_ab = scores[:2 * C, :2 * C]
        s_ak = jnp.where(strict, scores[:2 * C, 2 * C:], 0.0)
        s_rb = jnp.where(incl, scores[2 * C:, :2 * C], 0.0)
        s_rk = jnp.where(incl, scores[2 * C:, 2 * C:], 0.0)

        x1 = jnp.where(blk16, s_ab, 0.0)
        x2 = _bdot(x1, x1)
        rhs_sa = stack(ar[:C]) + jnp.dot(s_ak.astype(BF16), s_v, preferred_element_type=F32)
        yield
        x4 = _bdot(x2, x2)
        tinv = eye_f + x1
        tinv = tinv + _bdot(tinv, x2)
        yield
        x8 = _bdot(x4, x4)
        tinv = tinv + _bdot(tinv, x4)
        yield
        tinv = tinv + _bdot(tinv, x8)
        yield
        for blk in (blk32, blk64):
            off = jnp.where(blk, s_ab, 0.0)
            part = _bdot(tinv, off)
            yield
            tinv = tinv + _bdot(part, tinv)
            yield

        s_sa = _bdot(tinv, rhs_sa)
        yield
        sa_v = jnp.concatenate([s_sa.astype(BF16), s_v], axis=0)
        y_s = stack(ar[C:]) + jnp.dot(jnp.concatenate([s_rb, s_rk], axis=1).astype(BF16), sa_v,
                                      preferred_element_type=F32)
        upd_l = jnp.concatenate([stack(kb * to_end), stack(kmod * to_end)], axis=0).astype(BF16)
        upd = lax.dot_general(upd_l, sa_v, (((0,), (0,)), ((), ())), preferred_element_type=F32)
        yield
        y = y_s[:C] + y_s[C:]
        g_col = jnp.sum(jnp.where(eye, g_end, 0.0), axis=1, keepdims=True)
        h_ref[q] = h * g_col + upd

        mean = headsum(y) * (1.0 / N)
        yield
        dev = y - mean
        var = headsum(dev * dev) * (1.0 / N)
        yield
        yn = dev * lax.rsqrt(var + LN_X_EPS) * lw + lb
        o_ref[sl, ln] = (yn + bonus * v) * g_ref[sl, ln]

    for c in range(r_ref.shape[0] // C):
        live = [pair_chunk(c, q) for q in range(r_ref.shape[1] // P)]
        while live:
            advanced = []
            for gen in live:
                try:
                    next(gen)
                    advanced.append(gen)
                except StopIteration:
                    pass
            live = advanced


def _scan(p_all, ld, a, g, kkw, ka, rk, lw, lb, *, batch, seq, r_block, k_block, v_block, tb=CHUNK, pairs=12):
    t, rw = ld.shape
    width = pairs * LANES
    tpb = seq // tb
    tok = lambda blk0: pl.BlockSpec((tb, width), lambda b, p, s: (b * tpb + s, blk0 // pairs + p))
    par = pl.BlockSpec((1, width), lambda b, p, s: (0, p))
    return pl.pallas_call(
        _scan_kernel,
        grid=(batch, rw // width, tpb),
        in_specs=[tok(r_block), tok(k_block), tok(v_block), tok(0), tok(0), tok(0),
                  par, par, par, par, par],
        out_specs=tok(0),
        out_shape=jax.ShapeDtypeStruct((t, rw), F32),
        scratch_shapes=[pltpu.VMEM((pairs, LANES, LANES), F32)],
        compiler_params=_cparams(("arbitrary", "arbitrary", "arbitrary")),
        name="scan",
    )(p_all, p_all, p_all, ld, a, g, kkw, ka, rk, lw, lb)


def _outproj_kernel(x_ref, yp_ref, yr_ref, mod_ref, gpost_ref, wp_ref, wr_ref, o_ref):
    y = (jnp.dot(yp_ref[...].astype(BF16), wp_ref[...], preferred_element_type=F32)
         + jnp.dot(yr_ref[...].astype(BF16), wr_ref[...], preferred_element_type=F32))
    gate = mod_ref[0, N_MOD + 2:N_MOD + 3, :]
    o_ref[...] = x_ref[...] + (1.0 + gate) * _rms(y, gpost_ref[...])


def _outproj(x2, y_pool, y_rwkv, mod3, gpost, wp, wr, *, seq, tm=512):
    t, d = x2.shape
    tps = seq // tm
    return pl.pallas_call(
        _outproj_kernel,
        grid=(t // tm,),
        in_specs=[pl.BlockSpec((tm, d), lambda i: (i, 0)),
                  pl.BlockSpec((tm, y_pool.shape[1]), lambda i: (i, 0)),
                  pl.BlockSpec((tm, y_rwkv.shape[1]), lambda i: (i, 0)),
                  pl.BlockSpec((1,) + mod3.shape[1:], lambda i: (i // tps, 0, 0)),
                  pl.BlockSpec((1, d), lambda i: (0, 0)),
                  pl.BlockSpec(wp.shape, lambda i: (0, 0)),
                  pl.BlockSpec(wr.shape, lambda i: (0, 0))],
        out_specs=pl.BlockSpec((tm, d), lambda i: (i, 0)),
        out_shape=jax.ShapeDtypeStruct((t, d), F32),
        compiler_params=_cparams(("arbitrary",)),
        name="outproj",
    )(x2, y_pool, y_rwkv, mod3, gpost, wp, wr)


def _pad_cols(w, n):
    return jnp.pad(w, ((0, 0), (0, n - w.shape[1])))


def _layer(x2, c, batch, seq, w_ada, b_ada, norm_pre, norm_post, f1g, f1u, f1d, w_in, mu_shift, pool_w,
           pool_scale, w0, w2, a0, a2, g2, k_k, k_a, r_k, lnx_w, lnx_b, w_out, f2g, f2u, f2d):
    d = x2.shape[1]
    pool_width = pool_scale.shape[0]
    rw = w0.shape[0]
    n_sub = norm_pre.shape[0]
    col_tile = 512

    mod3 = _ada(c, w_ada, b_ada).reshape(batch, n_sub * N_MOD, d)
    row = lambda vec: vec.reshape(1, -1)

    x2 = _ffn(x2, mod3, row(norm_pre[0]), row(norm_post[0]), f1g.astype(BF16), f1u.astype(BF16),
              f1d.astype(BF16), sub=0, seq=seq)

    n_lora = w_in.shape[1] - pool_width - 3 * rw
    assert (3 * rw) % col_tile == 0 and n_lora <= col_tile and pool_width == col_tile and rw % LANES == 0
    w_in_p = jnp.concatenate([w_in[:, pool_width:pool_width + 3 * rw],
                              _pad_cols(w_in[:, pool_width + 3 * rw:], col_tile),
                              w_in[:, :pool_width]], axis=1).astype(BF16)
    mu_p = jnp.concatenate([mu_shift, jnp.zeros((col_tile - n_lora + pool_width,), F32)])
    p_all = _inproj(x2, mod3, row(norm_pre[1]), w_in_p, row(mu_p), seq=seq, tm=1024,
                    tn=w_in_p.shape[1] // 4)
    lora_block = 3 * rw // col_tile

    y_pool = _pool(p_all, lora_block + 1, pool_w.astype(BF16), row(pool_scale), seq=seq)

    n_w, n_a = w2.shape[0], a2.shape[0]
    w2p = jnp.pad(w2, ((0, col_tile - n_w), (0, 0))).astype(BF16)
    a2p = jnp.pad(a2, ((n_w, col_tile - n_w - n_a), (0, 0))).astype(BF16)
    g2p = jnp.pad(g2, ((n_w + n_a, col_tile - n_lora), (0, 0))).astype(BF16)
    ld, a, g = _lora(p_all, lora_block, w2p, a2p, g2p, row(w0), row(a0))

    y_rwkv = _scan(p_all, ld, a, g, row(k_k), row(k_a), row(r_k), row(lnx_w), row(lnx_b),
                   batch=batch, seq=seq, r_block=0, k_block=rw // LANES, v_block=2 * rw // LANES)

    x2 = _outproj(x2, y_pool, y_rwkv, mod3, row(norm_post[1]), w_out[:pool_width].astype(BF16),
                  w_out[pool_width:].astype(BF16), seq=seq)

    x2 = _ffn(x2, mod3, row(norm_pre[2]), row(norm_post[2]), f2g.astype(BF16), f2u.astype(BF16),
              f2d.astype(BF16), sub=2, seq=seq)
    return x2


def kernel(x, c, w_ada, b_ada, norm_pre, norm_post, ffn1_w_gate, ffn1_w_up, ffn1_w_down, w_in, mu_shift,
           pool_w, pool_scale, w0, w2, a0, a2, g2, k_k, k_a, r_k, lnx_w, lnx_b, w_out, ffn2_w_gate,
           ffn2_w_up, ffn2_w_down):
    batch, seq, d = x.shape
    x2 = x.reshape(batch * seq, d)
    for l in range(w_ada.shape[0]):
        x2 = _layer(x2, c, batch, seq, w_ada[l], b_ada[l], norm_pre[l], norm_post[l], ffn1_w_gate[l],
                    ffn1_w_up[l], ffn1_w_down[l], w_in[l], mu_shift[l], pool_w[l], pool_scale[l], w0[l],
                    w2[l], a0[l], a2[l], g2[l], k_k[l], k_a[l], r_k[l], lnx_w[l], lnx_b[l], w_out[l],
                    ffn2_w_gate[l], ffn2_w_up[l], ffn2_w_down[l])
    return x2.reshape(batch, seq, d)
```

```python
import functools

import jax
import jax.numpy as jnp
from jax import lax
from jax.experimental import pallas as pl
from jax.experimental.pallas import tpu as pltpu

F32 = jnp.float32
BF16 = jnp.bfloat16

NORM_EPS = 1e-6
HEAD_SIZE = 64
LN_X_EPS = 1e-5 * HEAD_SIZE
POOL_WINDOWS = (2, 4, 8, 16)
POOL_GROUP = 128
MACARON_WEIGHT = 0.5
N_MOD = 3

LANES = 128
CHUNK = 64
VMEM_LIMIT = 56 * 1024 * 1024


def _cparams(sem):
    return pltpu.CompilerParams(dimension_semantics=sem, vmem_limit_bytes=VMEM_LIMIT)


def _sigmoid(z):
    return 1.0 / (1.0 + jnp.exp(-z))


def _rms(x, gain):
    ms = jnp.mean(x * x, axis=-1, keepdims=True)
    return x * lax.rsqrt(ms + NORM_EPS) * gain


def _norm_mod(x, gain, mod_ref, sub):
    shift = mod_ref[0, N_MOD * sub:N_MOD * sub + 1, :]
    scale = mod_ref[0, N_MOD * sub + 1:N_MOD * sub + 2, :]
    return _rms(x, gain) * (1.0 + scale) + shift


def _bdot(a, b):
    return jnp.dot(a.astype(BF16), b.astype(BF16), preferred_element_type=F32)


def _ada_kernel(c_ref, w_ref, b_ref, o_ref):
    c = c_ref[...]
    s = c * _sigmoid(c)
    o_ref[...] = jnp.dot(s, w_ref[...], preferred_element_type=F32,
                         precision=lax.Precision.HIGHEST) + b_ref[...]


def _ada(c, w, b, tn=1024):
    bsz, d = c.shape
    n = w.shape[1]
    return pl.pallas_call(
        _ada_kernel,
        grid=(n // tn,),
        in_specs=[pl.BlockSpec((bsz, d), lambda j: (0, 0)),
                  pl.BlockSpec((d, tn), lambda j: (0, j)),
                  pl.BlockSpec((1, tn), lambda j: (0, j))],
        out_specs=pl.BlockSpec((bsz, tn), lambda j: (0, j)),
        out_shape=jax.ShapeDtypeStruct((bsz, n), F32),
        compiler_params=_cparams(("arbitrary",)),
        name="ada",
    )(c, w, b.reshape(1, n))


def _ffn_kernel(x_ref, mod_ref, gpre_ref, gpost_ref, wg_ref, wu_ref, wd_ref, o_ref, h_ref, *, sub):
    j = pl.program_id(1)

    @pl.when(j == 0)
    def _():
        h_ref[...] = _norm_mod(x_ref[...], gpre_ref[...], mod_ref, sub).astype(BF16)
        o_ref[...] = jnp.zeros_like(o_ref)

    h = h_ref[...]
    g = jnp.dot(h, wg_ref[...], preferred_element_type=F32)
    u = jnp.dot(h, wu_ref[...], preferred_element_type=F32)
    act = (g * _sigmoid(g) * u).astype(BF16)
    o_ref[...] += jnp.dot(act, wd_ref[...], preferred_element_type=F32)

    @pl.when(j == pl.num_programs(1) - 1)
    def _():
        gate = mod_ref[0, N_MOD * sub + 2:N_MOD * sub + 3, :]
        y = _rms(o_ref[...], gpost_ref[...])
        o_ref[...] = x_ref[...] + MACARON_WEIGHT * (1.0 + gate) * y


def _ffn(x2, mod3, gpre, gpost, wg, wu, wd, *, sub, seq, tm=512, tf=512):
    t, d = x2.shape
    f = wg.shape[1]
    tps = seq // tm
    return pl.pallas_call(
        functools.partial(_ffn_kernel, sub=sub),
        grid=(t // tm, f // tf),
        in_specs=[pl.BlockSpec((tm, d), lambda i, j: (i, 0)),
                  pl.BlockSpec((1,) + mod3.shape[1:], lambda i, j: (i // tps, 0, 0)),
                  pl.BlockSpec((1, d), lambda i, j: (0, 0)),
                  pl.BlockSpec((1, d), lambda i, j: (0, 0)),
                  pl.BlockSpec((d, tf), lambda i, j: (0, j)),
                  pl.BlockSpec((d, tf), lambda i, j: (0, j)),
                  pl.BlockSpec((tf, d), lambda i, j: (j, 0))],
        out_specs=pl.BlockSpec((tm, d), lambda i, j: (i, 0)),
        out_shape=jax.ShapeDtypeStruct((t, d), F32),
        scratch_shapes=[pltpu.VMEM((tm, d), BF16)],
        compiler_params=_cparams(("arbitrary", "arbitrary")),
        name=f"ffn{sub}",
    )(x2, mod3, gpre, gpost, wg, wu, wd)


def _inproj_kernel(x_ref, mod_ref, gpre_ref, w_ref, mu_ref, o_ref, h_ref, carry_ref, *, tps):
    i = pl.program_id(0)
    j = pl.program_id(1)

    @pl.when(j == 0)
    def _():
        h_ref[...] = _norm_mod(x_ref[...], gpre_ref[...], mod_ref, 1).astype(BF16)

    res = jnp.dot(h_ref[...], w_ref[...], preferred_element_type=F32)
    tm, tn = res.shape
    carried = carry_ref[j][0:1, :]
    first = jnp.where(i % tps == 0, jnp.zeros_like(carried), carried)
    row = lax.broadcasted_iota(jnp.int32, (tm, 1), 0)
    prev = jnp.where(row == 0, first, pltpu.roll(res, 1, 0))
    carry_ref[j] = jnp.broadcast_to(res[tm - 1:tm, :], (8, tn))
    o_ref[...] = res + mu_ref[...] * (prev - res)


def _inproj(x2, mod3, gpre, w, mu, *, seq, tm=512, tn=512):
    t, d = x2.shape
    n = w.shape[1]
    tps = seq // tm
    return pl.pallas_call(
        functools.partial(_inproj_kernel, tps=tps),
        grid=(t // tm, n // tn),
        in_specs=[pl.BlockSpec((tm, d), lambda i, j: (i, 0)),
                  pl.BlockSpec((1,) + mod3.shape[1:], lambda i, j: (i // tps, 0, 0)),
                  pl.BlockSpec((1, d), lambda i, j: (0, 0)),
                  pl.BlockSpec((d, tn), lambda i, j: (0, j)),
                  pl.BlockSpec((1, tn), lambda i, j: (0, j))],
        out_specs=pl.BlockSpec((tm, tn), lambda i, j: (i, j)),
        out_shape=jax.ShapeDtypeStruct((t, n), F32),
        scratch_shapes=[pltpu.VMEM((tm, d), BF16), pltpu.VMEM((n // tn, 8, tn), F32)],
        compiler_params=_cparams(("arbitrary", "arbitrary")),
        name="inproj",
    )(x2, mod3, gpre, w, mu)


def _pool_kernel(u_ref, w_ref, s_ref, o_ref, ext_ref, *, tps):
    i = pl.program_id(0)
    tm = u_ref.shape[0]
    pad = max(POOL_WINDOWS)

    @pl.when(i % tps == 0)
    def _():
        ext_ref[0:pad, :] = jnp.zeros((pad, ext_ref.shape[1]), F32)

    ext_ref[pad:pad + tm, :] = u_ref[...]
    t_in_seq = (i % tps) * tm + lax.broadcasted_iota(jnp.int32, (tm, 1), 0)
    tpos = (t_in_seq + 1).astype(F32)
    for gi, win in enumerate(POOL_WINDOWS):
        lo = gi * POOL_GROUP
        acc = ext_ref[pad:pad + tm, lo:lo + POOL_GROUP]
        u_g = acc
        for dlt in range(1, win):
            acc = acc + ext_ref[pad - dlt:pad - dlt + tm, lo:lo + POOL_GROUP]
        pooled = acc / jnp.minimum(tpos, float(win)) - u_g
        mixed = _bdot(pooled, w_ref[gi])
        o_ref[:, lo:lo + POOL_GROUP] = mixed * s_ref[:, lo:lo + POOL_GROUP]
    ext_ref[0:pad, :] = ext_ref[tm:tm + pad, :]


def _pool(p_all, col_block, pool_w, pool_scale, *, seq, tm=512):
    t = p_all.shape[0]
    pw = pool_scale.shape[1]
    tps = seq // tm
    return pl.pallas_call(
        functools.partial(_pool_kernel, tps=tps),
        grid=(t // tm,),
        in_specs=[pl.BlockSpec((tm, pw), lambda i: (i, col_block)),
                  pl.BlockSpec(pool_w.shape, lambda i: (0, 0, 0)),
                  pl.BlockSpec((1, pw), lambda i: (0, 0))],
        out_specs=pl.BlockSpec((tm, pw), lambda i: (i, 0)),
        out_shape=jax.ShapeDtypeStruct((t, pw), F32),
        scratch_shapes=[pltpu.VMEM((tm + max(POOL_WINDOWS), pw), F32)],
        compiler_params=_cparams(("arbitrary",)),
        name="pool",
    )(p_all, pool_w, pool_scale)


def _lora_kernel(p_ref, w2_ref, a2_ref, g2_ref, w0_ref, a0_ref, ld_ref, a_ref, g_ref):
    p = p_ref[...]
    z = w0_ref[...] + _bdot(jnp.tanh(p), w2_ref[...])
    w_log = -(jnp.maximum(-z, 0.0) + jnp.log(1.0 + jnp.exp(-jnp.abs(z)))) - 0.5
    ld_ref[...] = -jnp.exp(w_log)
    a_ref[...] = _sigmoid(a0_ref[...] + _bdot(p, a2_ref[...]))
    g_ref[...] = _bdot(_sigmoid(p), g2_ref[...])


def _lora(p_all, col_block, w2p, a2p, g2p, w0, a0, *, tm=512):
    t = p_all.shape[0]
    kw, r = w2p.shape
    full = lambda shape: pl.BlockSpec(shape, lambda i: (0, 0))
    out = jax.ShapeDtypeStruct((t, r), F32)
    return pl.pallas_call(
        _lora_kernel,
        grid=(t // tm,),
        in_specs=[pl.BlockSpec((tm, kw), lambda i: (i, col_block)),
                  full((kw, r)), full((kw, r)), full((kw, r)), full((1, r)), full((1, r))],
        out_specs=[pl.BlockSpec((tm, r), lambda i: (i, 0))] * 3,
        out_shape=[out, out, out],
        compiler_params=_cparams(("arbitrary",)),
        name="lora",
    )(p_all, w2p, a2p, g2p, w0, a0)


def _split_dot(x, m2):
    hi = x.astype(BF16)
    lo = (x - hi.astype(F32)).astype(BF16)
    return jnp.dot(jnp.concatenate([hi, lo], axis=1), m2, preferred_element_type=F32)


def _scan_kernel(r_ref, k_ref, v_ref, ld_ref, a_ref, g_ref, kkw_ref, ka_ref, rk_ref, lw_ref, lb_ref,
                 o_ref, h_ref):
    C = CHUNK
    P = LANES
    N = HEAD_SIZE

    @pl.when(pl.program_id(2) == 0)
    def _():
        h_ref[...] = jnp.zeros_like(h_ref)

    lane = lax.broadcasted_iota(jnp.int32, (1, P), 1)
    in_a = lane < N
    ri = lax.broadcasted_iota(jnp.int32, (2 * C, 2 * C), 0)
    ci = lax.broadcasted_iota(jnp.int32, (2 * C, 2 * C), 1)

    def same(b):
        sh = b.bit_length() - 1
        return (ri >> sh) == (ci >> sh)

    strict = (ri > ci) & same(C)
    incl = (ri >= ci) & same(C)
    eye = ri == ci
    blk16 = strict & same(16)
    blk32 = strict & same(32) & jnp.logical_not(same(16))
    blk64 = strict & jnp.logical_not(same(32))
    eye_f = eye.astype(F32)
    head_ones = same(N).astype(BF16)
    head_ones2 = jnp.concatenate([head_ones, head_ones], axis=0)
    tri3 = (lax.broadcasted_iota(jnp.int32, (C, 3 * C), 0)
            >= (lax.broadcasted_iota(jnp.int32, (C, 3 * C), 1) & (C - 1))).astype(BF16)

    def stack(x):
        return jnp.concatenate([jnp.where(in_a, x, 0.0), jnp.where(in_a, 0.0, x)], axis=0)

    def headsum(x):
        return _split_dot(x, head_ones2)

    def pair_chunk(c, q):
        sl = pl.ds(c * C, C)
        ln = pl.ds(q * P, P)
        kkw = kkw_ref[:, ln]
        ka = ka_ref[:, ln]
        rk = rk_ref[:, ln]
        lw = lw_ref[:, ln]
        lb = lb_ref[:, ln]
        r = r_ref[sl, ln]
        k = k_ref[sl, ln]
        v = v_ref[sl, ln]
        ld = ld_ref[sl, ln]
        a = a_ref[sl, ln]

        kkraw = k * kkw
        kmod = k * (1.0 + (a - 1.0) * ka)
        sums = headsum(jnp.concatenate([kkraw * kkraw, r * kmod * rk], axis=0))
        ld_hi = ld.astype(BF16)
        rem = ld - ld_hi.astype(F32)
        ld_mid = rem.astype(BF16)
        ld_lo = (rem - ld_mid.astype(F32)).astype(BF16)
        cum = jnp.dot(tri3, jnp.concatenate([ld_hi, ld_mid, ld_lo], axis=0), preferred_element_type=F32)
        yield
        kk = kkraw / jnp.maximum(jnp.sqrt(sums[:C]), 1e-12)
        bonus = sums[C:]
        cum_last = cum[C - 1:C, :]

        inv_g = jnp.exp(-cum)
        to_end = jnp.exp(cum_last - cum)
        kb = kk * a
        rt = r * jnp.exp(cum)
        at = -kk * jnp.exp(cum - ld)
        bt = kb * inv_g
        kt = kmod * inv_g
        g_end = jnp.exp(cum_last)

        s_v = stack(v).astype(BF16)
        lhs = jnp.concatenate([stack(at), stack(rt)], axis=0).astype(BF16)
        rhs = jnp.concatenate([stack(bt), stack(kt)], axis=0).astype(BF16)
        scores = lax.dot_general(lhs, rhs, (((1,), (1,)), ((), ())), preferred_element_type=F32)
        h = h_ref[q]
        h_b = h.astype(BF16)
        s_at = lhs[:2 * C]
        s_rt = lhs[2 * C:]
        yield
        s_ab = scores[:2 * C, :2 * C]
        s_ak = jnp.where(strict, scores[:2 * C, 2 * C:], 0.0)
        s_rb = jnp.where(incl, scores[2 * C:, :2 * C], 0.0)
        s_rk = jnp.where(incl, scores[2 * C:, 2 * C:], 0.0)

        x1 = jnp.where(blk16, s_ab, 0.0)
        x2 = _bdot(x1, x1)
        rhs_sa = jnp.dot(jnp.concatenate([s_at, s_ak.astype(BF16)], axis=1),
                         jnp.concatenate([h_b, s_v], axis=0), preferred_element_type=F32)
        yield
        x4 = _bdot(x2, x2)
        tinv = eye_f + x1
        tinv = tinv + _bdot(tinv, x2)
        yield
        x8 = _bdot(x4, x4)
        tinv = tinv + _bdot(tinv, x4)
        yield
        tinv = tinv + _bdot(tinv, x8)
        yield
        for blk in (blk32, blk64):
            off = jnp.where(blk, s_ab, 0.0)
            part = _bdot(tinv, off)
            yield
            tinv = tinv + _bdot(part, tinv)
            yield

        s_sa = _bdot(tinv, rhs_sa)
        yield
        sa_v = jnp.concatenate([s_sa.astype(BF16), s_v], axis=0)
        y_s = jnp.dot(jnp.concatenate([s_rt, s_rb.astype(BF16), s_rk.astype(BF16)], axis=1),
                      jnp.concatenate([h_b, sa_v], axis=0), preferred_element_type=F32)
        upd_l = jnp.concatenate([stack(kb * to_end), stack(kmod * to_end)], axis=0).astype(BF16)
        upd = lax.dot_general(upd_l, sa_v, (((0,), (0,)), ((), ())), preferred_element_type=F32)
        yield
        y = y_s[:C] + y_s[C:]
        g_col = jnp.sum(jnp.where(eye, g_end, 0.0), axis=1, keepdims=True)
        h_ref[q] = h * g_col + upd

        mean = headsum(y) * (1.0 / N)
        yield
        dev = y - mean
        var = headsum(dev * dev) * (1.0 / N)
        yield
        yn = dev * lax.rsqrt(var + LN_X_EPS) * lw + lb
        o_ref[sl, ln] = (yn + bonus * v) * g_ref[sl, ln]

    for c in range(r_ref.shape[0] // C):
        live = [pair_chunk(c, q) for q in range(r_ref.shape[1] // P)]
        while live:
            advanced = []
            for gen in live:
                try:
                    next(gen)
                    advanced.append(gen)
                except StopIteration:
                    pass
            live = advanced


def _scan(p_all, ld, a, g, kkw, ka, rk, lw, lb, *, batch, seq, r_block, k_block, v_block, tb=CHUNK, pairs=12):
    t, rw = ld.shape
    width = pairs * LANES
    tpb = seq // tb
    tok = lambda blk0: pl.BlockSpec((tb, width), lambda b, p, s: (b * tpb + s, blk0 // pairs + p))
    par = pl.BlockSpec((1, width), lambda b, p, s: (0, p))
    return pl.pallas_call(
        _scan_kernel,
        grid=(batch, rw // width, tpb),
        in_specs=[tok(r_block), tok(k_block), tok(v_block), tok(0), tok(0), tok(0),
                  par, par, par, par, par],
        out_specs=tok(0),
        out_shape=jax.ShapeDtypeStruct((t, rw), F32),
        scratch_shapes=[pltpu.VMEM((pairs, LANES, LANES), F32)],
        compiler_params=_cparams(("arbitrary", "arbitrary", "arbitrary")),
        name="scan",
    )(p_all, p_all, p_all, ld, a, g, kkw, ka, rk, lw, lb)


def _outproj_kernel(x_ref, yp_ref, yr_ref, mod_ref, gpost_ref, wp_ref, wr_ref, o_ref):
    y = (jnp.dot(yp_ref[...].astype(BF16), wp_ref[...], preferred_element_type=F32)
         + jnp.dot(yr_ref[...].astype(BF16), wr_ref[...], preferred_element_type=F32))
    gate = mod_ref[0, N_MOD + 2:N_MOD + 3, :]
    o_ref[...] = x_ref[...] + (1.0 + gate) * _rms(y, gpost_ref[...])


def _outproj(x2, y_pool, y_rwkv, mod3, gpost, wp, wr, *, seq, tm=512):
    t, d = x2.shape
    tps = seq // tm
    return pl.pallas_call(
        _outproj_kernel,
        grid=(t // tm,),
        in_specs=[pl.BlockSpec((tm, d), lambda i: (i, 0)),
                  pl.BlockSpec((tm, y_pool.shape[1]), lambda i: (i, 0)),
                  pl.BlockSpec((tm, y_rwkv.shape[1]), lambda i: (i, 0)),
                  pl.BlockSpec((1,) + mod3.shape[1:], lambda i: (i // tps, 0, 0)),
                  pl.BlockSpec((1, d), lambda i: (0, 0)),
                  pl.BlockSpec(wp.shape, lambda i: (0, 0)),
                  pl.BlockSpec(wr.shape, lambda i: (0, 0))],
        out_specs=pl.BlockSpec((tm, d), lambda i: (i, 0)),
        out_shape=jax.ShapeDtypeStruct((t, d), F32),
        compiler_params=_cparams(("arbitrary",)),
        name="outproj",
    )(x2, y_pool, y_rwkv, mod3, gpost, wp, wr)


def _pad_cols(w, n):
    return jnp.pad(w, ((0, 0), (0, n - w.shape[1])))


def _layer(x2, c, batch, seq, w_ada, b_ada, norm_pre, norm_post, f1g, f1u, f1d, w_in, mu_shift, pool_w,
           pool_scale, w0, w2, a0, a2, g2, k_k, k_a, r_k, lnx_w, lnx_b, w_out, f2g, f2u, f2d):
    d = x2.shape[1]
    pool_width = pool_scale.shape[0]
    rw = w0.shape[0]
    n_sub = norm_pre.shape[0]
    col_tile = 512

    mod3 = _ada(c, w_ada, b_ada).reshape(batch, n_sub * N_MOD, d)
    row = lambda vec: vec.reshape(1, -1)

    x2 = _ffn(x2, mod3, row(norm_pre[0]), row(norm_post[0]), f1g.astype(BF16), f1u.astype(BF16),
              f1d.astype(BF16), sub=0, seq=seq)

    n_lora = w_in.shape[1] - pool_width - 3 * rw
    assert (3 * rw) % col_tile == 0 and n_lora <= col_tile and pool_width == col_tile and rw % LANES == 0
    w_in_p = jnp.concatenate([w_in[:, pool_width:pool_width + 3 * rw],
                              _pad_cols(w_in[:, pool_width + 3 * rw:], col_tile),
                              w_in[:, :pool_width]], axis=1).astype(BF16)
    mu_p = jnp.concatenate([mu_shift, jnp.zeros((col_tile - n_lora + pool_width,), F32)])
    p_all = _inproj(x2, mod3, row(norm_pre[1]), w_in_p, row(mu_p), seq=seq, tm=1024,
                    tn=w_in_p.shape[1] // 4)
    lora_block = 3 * rw // col_tile

    y_pool = _pool(p_all, lora_block + 1, pool_w.astype(BF16), row(pool_scale), seq=seq)

    n_w, n_a = w2.shape[0], a2.shape[0]
    w2p = jnp.pad(w2, ((0, col_tile - n_w), (0, 0))).astype(BF16)
    a2p = jnp.pad(a2, ((n_w, col_tile - n_w - n_a), (0, 0))).astype(BF16)
    g2p = jnp.pad(g2, ((n_w + n_a, col_tile - n_lora), (0, 0))).astype(BF16)
    ld, a, g = _lora(p_all, lora_block, w2p, a2p, g2p, row(w0), row(a0))

    y_rwkv = _scan(p_all, ld, a, g, row(k_k), row(k_a), row(r_k), row(lnx_w), row(lnx_b),
                   batch=batch, seq=seq, r_block=0, k_block=rw // LANES, v_block=2 * rw // LANES)

    x2 = _outproj(x2, y_pool, y_rwkv, mod3, row(norm_post[1]), w_out[:pool_width].astype(BF16),
                  w_out[pool_width:].astype(BF16), seq=seq)

    x2 = _ffn(x2, mod3, row(norm_pre[2]), row(norm_post[2]), f2g.astype(BF16), f2u.astype(BF16),
              f2d.astype(BF16), sub=2, seq=seq)
    return x2


def kernel(x, c, w_ada, b_ada, norm_pre, norm_post, ffn1_w_gate, ffn1_w_up, ffn1_w_down, w_in, mu_shift,
           pool_w, pool_scale, w0, w2, a0, a2, g2, k_k, k_a, r_k, lnx_w, lnx_b, w_out, ffn2_w_gate,
           ffn2_w_up, ffn2_w_down):
    batch, seq, d = x.shape
    x2 = x.reshape(batch * seq, d)
    for l in range(w_ada.shape[0]):
        x2 = _layer(x2, c, batch, seq, w_ada[l], b_ada[l], norm_pre[l], norm_post[l], ffn1_w_gate[l],
                    ffn1_w_up[l], ffn1_w_down[l], w_in[l], mu_shift[l], pool_w[l], pool_scale[l], w0[l],
                    w2[l], a0[l], a2[l], g2[l], k_k[l], k_a[l], r_k[l], lnx_w[l], lnx_b[l], w_out[l],
                    ffn2_w_gate[l], ffn2_w_up[l], ffn2_w_down[l])
    return x2.reshape(batch, seq, d)
```

```python
import functools

import jax
import jax.numpy as jnp
from jax import lax
from jax.experimental import pallas as pl
from jax.experimental.pallas import tpu as pltpu

F32 = jnp.float32
BF16 = jnp.bfloat16

NORM_EPS = 1e-6
HEAD_SIZE = 64
LN_X_EPS = 1e-5 * HEAD_SIZE
POOL_WINDOWS = (2, 4, 8, 16)
POOL_GROUP = 128
MACARON_WEIGHT = 0.5
N_MOD = 3

LANES = 128
CHUNK = 64
VMEM_LIMIT = 56 * 1024 * 1024


def _cparams(sem):
    return pltpu.CompilerParams(dimension_semantics=sem, vmem_limit_bytes=VMEM_LIMIT)


def _sigmoid(z):
    return 1.0 / (1.0 + jnp.exp(-z))


def _rms(x, gain):
    ms = jnp.mean(x * x, axis=-1, keepdims=True)
    return x * lax.rsqrt(ms + NORM_EPS) * gain


def _norm_mod(x, gain, mod_ref, sub):
    shift = mod_ref[0, N_MOD * sub:N_MOD * sub + 1, :]
    scale = mod_ref[0, N_MOD * sub + 1:N_MOD * sub + 2, :]
    return _rms(x, gain) * (1.0 + scale) + shift


def _bdot(a, b):
    return jnp.dot(a.astype(BF16), b.astype(BF16), preferred_element_type=F32)


def _ada_kernel(c_ref, w_ref, b_ref, o_ref):
    c = c_ref[...]
    s = c * _sigmoid(c)
    o_ref[...] = jnp.dot(s, w_ref[...], preferred_element_type=F32,
                         precision=lax.Precision.HIGHEST) + b_ref[...]


def _ada(c, w, b, tn=1024):
    bsz, d = c.shape
    n = w.shape[1]
    return pl.pallas_call(
        _ada_kernel,
        grid=(n // tn,),
        in_specs=[pl.BlockSpec((bsz, d), lambda j: (0, 0)),
                  pl.BlockSpec((d, tn), lambda j: (0, j)),
                  pl.BlockSpec((1, tn), lambda j: (0, j))],
        out_specs=pl.BlockSpec((bsz, tn), lambda j: (0, j)),
        out_shape=jax.ShapeDtypeStruct((bsz, n), F32),
        compiler_params=_cparams(("arbitrary",)),
        name="ada",
    )(c, w, b.reshape(1, n))


def _ffn_kernel(x_ref, xp_ref, mod_ref, modp_ref, gpre_ref, gpost_ref, wg_ref, wu_ref, wd_ref, o_ref,
                h_ref, act_ref, *, sub, nt):
    i = pl.program_id(0)
    j = pl.program_id(1)
    nf = pl.num_programs(1)
    cur = (i * nf + j) % 2
    prv = 1 - cur

    def up_chunk(h):
        g = jnp.dot(h, wg_ref[...], preferred_element_type=F32)
        u = jnp.dot(h, wu_ref[...], preferred_element_type=F32)
        return (g * _sigmoid(g) * u).astype(BF16)

    def down_chunk():
        return jnp.dot(act_ref[prv], wd_ref[...], preferred_element_type=F32)

    @pl.when((i == 0) & (j == 0))
    def _():
        act_ref[...] = jnp.zeros_like(act_ref)
        o_ref[...] = jnp.zeros_like(o_ref)

    @pl.when(j == 0)
    def _():
        h = _norm_mod(x_ref[...], gpre_ref[...], mod_ref, sub).astype(BF16)
        h_ref[...] = h
        acc = o_ref[...] + down_chunk()
        gate = modp_ref[0, N_MOD * sub + 2:N_MOD * sub + 3, :]
        o_ref[...] = xp_ref[...] + MACARON_WEIGHT * (1.0 + gate) * _rms(acc, gpost_ref[...])
        act_ref[cur] = up_chunk(h)

    @pl.when((j == 1) & (i < nt))
    def _():
        o_ref[...] = down_chunk()
        act_ref[cur] = up_chunk(h_ref[...])

    @pl.when((j > 1) & (i < nt))
    def _():
        o_ref[...] += down_chunk()
        act_ref[cur] = up_chunk(h_ref[...])


def _ffn(x2, mod3, gpre, gpost, wg, wu, wd, *, sub, seq, tm=512, tf=512):
    t, d = x2.shape
    f = wg.shape[1]
    tps = seq // tm
    nt, nf = t // tm, f // tf
    this = lambda i: jnp.minimum(i, nt - 1)
    prev = lambda i: jnp.clip(i - 1, 0, nt - 1)
    modspec = lambda tile: pl.BlockSpec((1,) + mod3.shape[1:], lambda i, j: (tile(i) // tps, 0, 0))
    up_idx = lambda i, j: (0, jnp.where(i < nt, j, 0))
    down_idx = lambda i, j: (jnp.where((i == nt) & (j > 0), nf - 1, (j + nf - 1) % nf), 0)
    out_idx = lambda i, j: (jnp.clip(i - (j == 0).astype(jnp.int32), 0, nt - 1), 0)
    return pl.pallas_call(
        functools.partial(_ffn_kernel, sub=sub, nt=nt),
        grid=(nt + 1, nf),
        in_specs=[pl.BlockSpec((tm, d), lambda i, j: (this(i), 0)),
                  pl.BlockSpec((tm, d), lambda i, j: (prev(i), 0)),
                  modspec(this), modspec(prev),
                  pl.BlockSpec((1, d), lambda i, j: (0, 0)),
                  pl.BlockSpec((1, d), lambda i, j: (0, 0)),
                  pl.BlockSpec((d, tf), up_idx),
                  pl.BlockSpec((d, tf), up_idx),
                  pl.BlockSpec((tf, d), down_idx)],
        out_specs=pl.BlockSpec((tm, d), out_idx),
        out_shape=jax.ShapeDtypeStruct((t, d), F32),
        scratch_shapes=[pltpu.VMEM((tm, d), BF16), pltpu.VMEM((2, tm, tf), BF16)],
        compiler_params=_cparams(("arbitrary", "arbitrary")),
        name=f"ffn{sub}",
    )(x2, x2, mod3, mod3, gpre, gpost, wg, wu, wd)


def _inproj_kernel(x_ref, mod_ref, gpre_ref, w_ref, mu_ref, o_ref, h_ref, carry_ref, *, tps):
    i = pl.program_id(0)
    j = pl.program_id(1)

    @pl.when(j == 0)
    def _():
        h_ref[...] = _norm_mod(x_ref[...], gpre_ref[...], mod_ref, 1).astype(BF16)

    res = jnp.dot(h_ref[...], w_ref[...], preferred_element_type=F32)
    tm, tn = res.shape
    carried = carry_ref[j][0:1, :]
    first = jnp.where(i % tps == 0, jnp.zeros_like(carried), carried)
    row = lax.broadcasted_iota(jnp.int32, (tm, 1), 0)
    prev = jnp.where(row == 0, first, pltpu.roll(res, 1, 0))
    carry_ref[j] = jnp.broadcast_to(res[tm - 1:tm, :], (8, tn))
    o_ref[...] = res + mu_ref[...] * (prev - res)


def _inproj(x2, mod3, gpre, w, mu, *, seq, tm=512, tn=512):
    t, d = x2.shape
    n = w.shape[1]
    tps = seq // tm
    return pl.pallas_call(
        functools.partial(_inproj_kernel, tps=tps),
        grid=(t // tm, n // tn),
        in_specs=[pl.BlockSpec((tm, d), lambda i, j: (i, 0)),
                  pl.BlockSpec((1,) + mod3.shape[1:], lambda i, j: (i // tps, 0, 0)),
                  pl.BlockSpec((1, d), lambda i, j: (0, 0)),
                  pl.BlockSpec((d, tn), lambda i, j: (0, j)),
                  pl.BlockSpec((1, tn), lambda i, j: (0, j))],
        out_specs=pl.BlockSpec((tm, tn), lambda i, j: (i, j)),
        out_shape=jax.ShapeDtypeStruct((t, n), F32),
        scratch_shapes=[pltpu.VMEM((tm, d), BF16), pltpu.VMEM((n // tn, 8, tn), F32)],
        compiler_params=_cparams(("arbitrary", "arbitrary")),
        name="inproj",
    )(x2, mod3, gpre, w, mu)


def _pool_kernel(u_ref, w_ref, s_ref, o_ref, ext_ref, *, tps):
    i = pl.program_id(0)
    tm = u_ref.shape[0]
    pad = max(POOL_WINDOWS)

    @pl.when(i % tps == 0)
    def _():
        ext_ref[0:pad, :] = jnp.zeros((pad, ext_ref.shape[1]), F32)

    ext_ref[pad:pad + tm, :] = u_ref[...]
    t_in_seq = (i % tps) * tm + lax.broadcasted_iota(jnp.int32, (tm, 1), 0)
    tpos = (t_in_seq + 1).astype(F32)
    for gi, win in enumerate(POOL_WINDOWS):
        lo = gi * POOL_GROUP
        acc = ext_ref[pad:pad + tm, lo:lo + POOL_GROUP]
        u_g = acc
        for dlt in range(1, win):
            acc = acc + ext_ref[pad - dlt:pad - dlt + tm, lo:lo + POOL_GROUP]
        pooled = acc / jnp.minimum(tpos, float(win)) - u_g
        mixed = _bdot(pooled, w_ref[gi])
        o_ref[:, lo:lo + POOL_GROUP] = mixed * s_ref[:, lo:lo + POOL_GROUP]
    ext_ref[0:pad, :] = ext_ref[tm:tm + pad, :]


def _pool(p_all, col_block, pool_w, pool_scale, *, seq, tm=512):
    t = p_all.shape[0]
    pw = pool_scale.shape[1]
    tps = seq // tm
    return pl.pallas_call(
        functools.partial(_pool_kernel, tps=tps),
        grid=(t // tm,),
        in_specs=[pl.BlockSpec((tm, pw), lambda i: (i, col_block)),
                  pl.BlockSpec(pool_w.shape, lambda i: (0, 0, 0)),
                  pl.BlockSpec((1, pw), lambda i: (0, 0))],
        out_specs=pl.BlockSpec((tm, pw), lambda i: (i, 0)),
        out_shape=jax.ShapeDtypeStruct((t, pw), F32),
        scratch_shapes=[pltpu.VMEM((tm + max(POOL_WINDOWS), pw), F32)],
        compiler_params=_cparams(("arbitrary",)),
        name="pool",
    )(p_all, pool_w, pool_scale)


def _lora_kernel(p_ref, w2_ref, a2_ref, g2_ref, w0_ref, a0_ref, ld_ref, a_ref, g_ref):
    p = p_ref[...]
    z = w0_ref[...] + _bdot(jnp.tanh(p), w2_ref[...])
    w_log = -(jnp.maximum(-z, 0.0) + jnp.log(1.0 + jnp.exp(-jnp.abs(z)))) - 0.5
    ld_ref[...] = -jnp.exp(w_log)
    a_ref[...] = _sigmoid(a0_ref[...] + _bdot(p, a2_ref[...]))
    g_ref[...] = _bdot(_sigmoid(p), g2_ref[...])


def _lora(p_all, col_block, w2p, a2p, g2p, w0, a0, *, tm=512):
    t = p_all.shape[0]
    kw, r = w2p.shape
    full = lambda shape: pl.BlockSpec(shape, lambda i: (0, 0))
    out = jax.ShapeDtypeStruct((t, r), F32)
    return pl.pallas_call(
        _lora_kernel,
        grid=(t // tm,),
        in_specs=[pl.BlockSpec((tm, kw), lambda i: (i, col_block)),
                  full((kw, r)), full((kw, r)), full((kw, r)), full((1, r)), full((1, r))],
        out_specs=[pl.BlockSpec((tm, r), lambda i: (i, 0))] * 3,
        out_shape=[out, out, out],
        compiler_params=_cparams(("arbitrary",)),
        name="lora",
    )(p_all, w2p, a2p, g2p, w0, a0)


def _split_dot(x, m2):
    hi = x.astype(BF16)
    lo = (x - hi.astype(F32)).astype(BF16)
    return jnp.dot(jnp.concatenate([hi, lo], axis=1), m2, preferred_element_type=F32)


def _scan_kernel(r_ref, k_ref, v_ref, ld_ref, a_ref, g_ref, kkw_ref, ka_ref, rk_ref, lw_ref, lb_ref,
                 o_ref, h_ref):
    C = CHUNK
    P = LANES
    N = HEAD_SIZE

    @pl.when(pl.program_id(2) == 0)
    def _():
        h_ref[...] = jnp.zeros_like(h_ref)

    lane = lax.broadcasted_iota(jnp.int32, (1, P), 1)
    in_a = lane < N
    ri = lax.broadcasted_iota(jnp.int32, (2 * C, 2 * C), 0)
    ci = lax.broadcasted_iota(jnp.int32, (2 * C, 2 * C), 1)

    def same(b):
        sh = b.bit_length() - 1
        return (ri >> sh) == (ci >> sh)

    strict = (ri > ci) & same(C)
    incl = (ri >= ci) & same(C)
    eye = ri == ci
    blk16 = strict & same(16)
    blk32 = strict & same(32) & jnp.logical_not(same(16))
    blk64 = strict & jnp.logical_not(same(32))
    eye_f = eye.astype(F32)
    head_ones = same(N).astype(BF16)
    head_ones2 = jnp.concatenate([head_ones, head_ones], axis=0)
    tri3 = (lax.broadcasted_iota(jnp.int32, (C, 3 * C), 0)
            >= (lax.broadcasted_iota(jnp.int32, (C, 3 * C), 1) & (C - 1))).astype(BF16)

    def stack(x):
        return jnp.concatenate([jnp.where(in_a, x, 0.0), jnp.where(in_a, 0.0, x)], axis=0)

    def headsum(x):
        return _split_dot(x, head_ones2)

    def pair_chunk(c, q):
        sl = pl.ds(c * C, C)
        ln = pl.ds(q * P, P)
        kkw = kkw_ref[:, ln]
        ka = ka_ref[:, ln]
        rk = rk_ref[:, ln]
        lw = lw_ref[:, ln]
        lb = lb_ref[:, ln]
        r = r_ref[sl, ln]
        k = k_ref[sl, ln]
        v = v_ref[sl, ln]
        ld = ld_ref[sl, ln]
        a = a_ref[sl, ln]

        kkraw = k * kkw
        kmod = k * (1.0 + (a - 1.0) * ka)
        sums = headsum(jnp.concatenate([kkraw * kkraw, r * kmod * rk], axis=0))
        ld_hi = ld.astype(BF16)
        rem = ld - ld_hi.astype(F32)
        ld_mid = rem.astype(BF16)
        ld_lo = (rem - ld_mid.astype(F32)).astype(BF16)
        cum = jnp.dot(tri3, jnp.concatenate([ld_hi, ld_mid, ld_lo], axis=0), preferred_element_type=F32)
        yield
        kk = kkraw / jnp.maximum(jnp.sqrt(sums[:C]), 1e-12)
        bonus = sums[C:]
        cum_last = cum[C - 1:C, :]

        inv_g = jnp.exp(-cum)
        to_end = jnp.exp(cum_last - cum)
        kb = kk * a
        rt = r * jnp.exp(cum)
        at = -kk * jnp.exp(cum - ld)
        bt = kb * inv_g
        kt = kmod * inv_g
        g_end = jnp.exp(cum_last)

        s_v = stack(v).astype(BF16)
        lhs = jnp.concatenate([stack(at), stack(rt)], axis=0).astype(BF16)
        rhs = jnp.concatenate([stack(bt), stack(kt)], axis=0).astype(BF16)
        scores = lax.dot_general(lhs, rhs, (((1,), (1,)), ((), ())), preferred_element_type=F32)
        h = h_ref[q]
        h_b = h.astype(BF16)
        s_at = lhs[:2 * C]
        s_rt = lhs[2 * C:]
        yield
        s_ab = scores[:2 * C, :2 * C]
        s_ak = jnp.where(strict, scores[:2 * C, 2 * C:], 0.0)
        s_rb = jnp.where(incl, scores[2 * C:, :2 * C], 0.0)
        s_rk = jnp.where(incl, scores[2 * C:, 2 * C:], 0.0)

        x1 = jnp.where(blk16, s_ab, 0.0)
        x2 = _bdot(x1, x1)
        rhs_sa = jnp.dot(jnp.concatenate([s_at, s_ak.astype(BF16)], axis=1),
                         jnp.concatenate([h_b, s_v], axis=0), preferred_element_type=F32)
        yield
        x4 = _bdot(x2, x2)
        tinv = eye_f + x1
        tinv = tinv + _bdot(tinv, x2)
        yield
        x8 = _bdot(x4, x4)
        tinv = tinv + _bdot(tinv, x4)
        yield
        tinv = tinv + _bdot(tinv, x8)
        yield
        for blk in (blk32, blk64):
            off = jnp.where(blk, s_ab, 0.0)
            part = _bdot(tinv, off)
            yield
            tinv = tinv + _bdot(part, tinv)
            yield

        s_sa = _bdot(tinv, rhs_sa)
        yield
        sa_v = jnp.concatenate([s_sa.astype(BF16), s_v], axis=0)
        y_s = jnp.dot(jnp.concatenate([s_rt, s_rb.astype(BF16), s_rk.astype(BF16)], axis=1),
                      jnp.concatenate([h_b, sa_v], axis=0), preferred_element_type=F32)
        upd_l = jnp.concatenate([stack(kb * to_end), stack(kmod * to_end)], axis=0).astype(BF16)
        upd = lax.dot_general(upd_l, sa_v, (((0,), (0,)), ((), ())), preferred_element_type=F32)
        yield
        y = y_s[:C] + y_s[C:]
        g_col = jnp.sum(jnp.where(eye, g_end, 0.0), axis=1, keepdims=True)
        h_ref[q] = h * g_col + upd

        mean = headsum(y) * (1.0 / N)
        yield
        dev = y - mean
        var = headsum(dev * dev) * (1.0 / N)
        yield
        yn = dev * lax.rsqrt(var + LN_X_EPS) * lw + lb
        o_ref[sl, ln] = (yn + bonus * v) * g_ref[sl, ln]

    for c in range(r_ref.shape[0] // C):
        live = [pair_chunk(c, q) for q in range(r_ref.shape[1] // P)]
        while live:
            advanced = []
            for gen in live:
                try:
                    next(gen)
                    advanced.append(gen)
                except StopIteration:
                    pass
            live = advanced


def _scan(p_all, ld, a, g, kkw, ka, rk, lw, lb, *, batch, seq, r_block, k_block, v_block, tb=CHUNK, pairs=12):
    t, rw = ld.shape
    width = pairs * LANES
    tpb = seq // tb
    tok = lambda blk0: pl.BlockSpec((tb, width), lambda b, p, s: (b * tpb + s, blk0 // pairs + p))
    par = pl.BlockSpec((1, width), lambda b, p, s: (0, p))
    return pl.pallas_call(
        _scan_kernel,
        grid=(batch, rw // width, tpb),
        in_specs=[tok(r_block), tok(k_block), tok(v_block), tok(0), tok(0), tok(0),
                  par, par, par, par, par],
        out_specs=tok(0),
        out_shape=jax.ShapeDtypeStruct((t, rw), F32),
        scratch_shapes=[pltpu.VMEM((pairs, LANES, LANES), F32)],
        compiler_params=_cparams(("arbitrary", "arbitrary", "arbitrary")),
        name="scan",
    )(p_all, p_all, p_all, ld, a, g, kkw, ka, rk, lw, lb)


def _outproj_kernel(x_ref, yp_ref, yr_ref, mod_ref, gpost_ref, wp_ref, wr_ref, o_ref):
    y = (jnp.dot(yp_ref[...].astype(BF16), wp_ref[...], preferred_element_type=F32)
         + jnp.dot(yr_ref[...].astype(BF16), wr_ref[...], preferred_element_type=F32))
    gate = mod_ref[0, N_MOD + 2:N_MOD + 3, :]
    o_ref[...] = x_ref[...] + (1.0 + gate) * _rms(y, gpost_ref[...])


def _outproj(x2, y_pool, y_rwkv, mod3, gpost, wp, wr, *, seq, tm=512):
    t, d = x2.shape
    tps = seq // tm
    return pl.pallas_call(
        _outproj_kernel,
        grid=(t // tm,),
        in_specs=[pl.BlockSpec((tm, d), lambda i: (i, 0)),
                  pl.BlockSpec((tm, y_pool.shape[1]), lambda i: (i, 0)),
                  pl.BlockSpec((tm, y_rwkv.shape[1]), lambda i: (i, 0)),
                  pl.BlockSpec((1,) + mod3.shape[1:], lambda i: (i // tps, 0, 0)),
                  pl.BlockSpec((1, d), lambda i: (0, 0)),
                  pl.BlockSpec(wp.shape, lambda i: (0, 0)),
                  pl.BlockSpec(wr.shape, lambda i: (0, 0))],
        out_specs=pl.BlockSpec((tm, d), lambda i: (i, 0)),
        out_shape=jax.ShapeDtypeStruct((t, d), F32),
        compiler_params=_cparams(("arbitrary",)),
        name="outproj",
    )(x2, y_pool, y_rwkv, mod3, gpost, wp, wr)


def _pad_cols(w, n):
    return jnp.pad(w, ((0, 0), (0, n - w.shape[1])))


def _layer(x2, c, batch, seq, w_ada, b_ada, norm_pre, norm_post, f1g, f1u, f1d, w_in, mu_shift, pool_w,
           pool_scale, w0, w2, a0, a2, g2, k_k, k_a, r_k, lnx_w, lnx_b, w_out, f2g, f2u, f2d):
    d = x2.shape[1]
    pool_width = pool_scale.shape[0]
    rw = w0.shape[0]
    n_sub = norm_pre.shape[0]
    col_tile = 512

    mod3 = _ada(c, w_ada, b_ada).reshape(batch, n_sub * N_MOD, d)
    row = lambda vec: vec.reshape(1, -1)

    x2 = _ffn(x2, mod3, row(norm_pre[0]), row(norm_post[0]), f1g.astype(BF16), f1u.astype(BF16),
              f1d.astype(BF16), sub=0, seq=seq)

    n_lora = w_in.shape[1] - pool_width - 3 * rw
    assert (3 * rw) % col_tile == 0 and n_lora <= col_tile and pool_width == col_tile and rw % LANES == 0
    w_in_p = jnp.concatenate([w_in[:, pool_width:pool_width + 3 * rw],
                              _pad_cols(w_in[:, pool_width + 3 * rw:], col_tile),
                              w_in[:, :pool_width]], axis=1).astype(BF16)
    mu_p = jnp.concatenate([mu_shift, jnp.zeros((col_tile - n_lora + pool_width,), F32)])
    p_all = _inproj(x2, mod3, row(norm_pre[1]), w_in_p, row(mu_p), seq=seq, tm=1024,
                    tn=w_in_p.shape[1] // 4)
    lora_block = 3 * rw // col_tile

    y_pool = _pool(p_all, lora_block + 1, pool_w.astype(BF16), row(pool_scale), seq=seq)

    n_w, n_a = w2.shape[0], a2.shape[0]
    w2p = jnp.pad(w2, ((0, col_tile - n_w), (0, 0))).astype(BF16)
    a2p = jnp.pad(a2, ((n_w, col_tile - n_w - n_a), (0, 0))).astype(BF16)
    g2p = jnp.pad(g2, ((n_w + n_a, col_tile - n_lora), (0, 0))).astype(BF16)
    ld, a, g = _lora(p_all, lora_block, w2p, a2p, g2p, row(w0), row(a0))

    y_rwkv = _scan(p_all, ld, a, g, row(k_k), row(k_a), row(r_k), row(lnx_w), row(lnx_b),
                   batch=batch, seq=seq, r_block=0, k_block=rw // LANES, v_block=2 * rw // LANES)

    x2 = _outproj(x2, y_pool, y_rwkv, mod3, row(norm_post[1]), w_out[:pool_width].astype(BF16),
                  w_out[pool_width:].astype(BF16), seq=seq)

    x2 = _ffn(x2, mod3, row(norm_pre[2]), row(norm_post[2]), f2g.astype(BF16), f2u.astype(BF16),
              f2d.astype(BF16), sub=2, seq=seq)
    return x2


def kernel(x, c, w_ada, b_ada, norm_pre, norm_post, ffn1_w_gate, ffn1_w_up, ffn1_w_down, w_in, mu_shift,
           pool_w, pool_scale, w0, w2, a0, a2, g2, k_k, k_a, r_k, lnx_w, lnx_b, w_out, ffn2_w_gate,
           ffn2_w_up, ffn2_w_down):
    batch, seq, d = x.shape
    x2 = x.reshape(batch * seq, d)
    for l in range(w_ada.shape[0]):
        x2 = _layer(x2, c, batch, seq, w_ada[l], b_ada[l], norm_pre[l], norm_post[l], ffn1_w_gate[l],
                    ffn1_w_up[l], ffn1_w_down[l], w_in[l], mu_shift[l], pool_w[l], pool_scale[l], w0[l],
                    w2[l], a0[l], a2[l], g2[l], k_k[l], k_a[l], r_k[l], lnx_w[l], lnx_b[l], w_out[l],
                    ffn2_w_gate[l], ffn2_w_up[l], ffn2_w_down[l])
    return x2.reshape(batch, seq, d)
```

```python
import functools

import jax
import jax.numpy as jnp
from jax import lax
from jax.experimental import pallas as pl
from jax.experimental.pallas import tpu as pltpu

F32 = jnp.float32
BF16 = jnp.bfloat16

NORM_EPS = 1e-6
HEAD_SIZE = 64
LN_X_EPS = 1e-5 * HEAD_SIZE
POOL_WINDOWS = (2, 4, 8, 16)
POOL_GROUP = 128
MACARON_WEIGHT = 0.5
N_MOD = 3

LANES = 128
CHUNK = 64
VMEM_LIMIT = 56 * 1024 * 1024


def _cparams(sem):
    return pltpu.CompilerParams(dimension_semantics=sem, vmem_limit_bytes=VMEM_LIMIT)


def _sigmoid(z):
    return 1.0 / (1.0 + jnp.exp(-z))


def _rms(x, gain):
    ms = jnp.mean(x * x, axis=-1, keepdims=True)
    return x * lax.rsqrt(ms + NORM_EPS) * gain


def _norm_mod(x, gain, mod_ref, sub):
    shift = mod_ref[0, N_MOD * sub:N_MOD * sub + 1, :]
    scale = mod_ref[0, N_MOD * sub + 1:N_MOD * sub + 2, :]
    return _rms(x, gain) * (1.0 + scale) + shift


def _bdot(a, b):
    return jnp.dot(a.astype(BF16), b.astype(BF16), preferred_element_type=F32)


def _ada_kernel(c_ref, w_ref, b_ref, o_ref):
    c = c_ref[...]
    s = c * _sigmoid(c)
    o_ref[...] = jnp.dot(s, w_ref[...], preferred_element_type=F32,
                         precision=lax.Precision.HIGHEST) + b_ref[...]


def _ada(c, w, b, tn=1024):
    bsz, d = c.shape
    n = w.shape[1]
    return pl.pallas_call(
        _ada_kernel,
        grid=(n // tn,),
        in_specs=[pl.BlockSpec((bsz, d), lambda j: (0, 0)),
                  pl.BlockSpec((d, tn), lambda j: (0, j)),
                  pl.BlockSpec((1, tn), lambda j: (0, j))],
        out_specs=pl.BlockSpec((bsz, tn), lambda j: (0, j)),
        out_shape=jax.ShapeDtypeStruct((bsz, n), F32),
        compiler_params=_cparams(("arbitrary",)),
        name="ada",
    )(c, w, b.reshape(1, n))


def _ffn_kernel(x_ref, mod_ref, gpre_ref, gpost_ref, wg_ref, wu_ref, wd_ref, o_ref, h_ref, *, sub):
    j = pl.program_id(1)

    @pl.when(j == 0)
    def _():
        h_ref[...] = _norm_mod(x_ref[...], gpre_ref[...], mod_ref, sub).astype(BF16)
        o_ref[...] = jnp.zeros_like(o_ref)

    h = h_ref[...]
    g = jnp.dot(h, wg_ref[...], preferred_element_type=F32)
    u = jnp.dot(h, wu_ref[...], preferred_element_type=F32)
    act = (g * _sigmoid(g) * u).astype(BF16)
    o_ref[...] += jnp.dot(act, wd_ref[...], preferred_element_type=F32)

    @pl.when(j == pl.num_programs(1) - 1)
    def _():
        gate = mod_ref[0, N_MOD * sub + 2:N_MOD * sub + 3, :]
        y = _rms(o_ref[...], gpost_ref[...])
        o_ref[...] = x_ref[...] + MACARON_WEIGHT * (1.0 + gate) * y


def _chunk_major(w, tf):
    d, f = w.shape
    return w.reshape(d, f // tf, tf).transpose(1, 0, 2).astype(BF16)


def _ffn(x2, mod3, gpre, gpost, wg, wu, wd, *, sub, seq, tm=512, tf=512):
    t, d = x2.shape
    f = wg.shape[1]
    tps = seq // tm
    wg, wu = _chunk_major(wg, tf), _chunk_major(wu, tf)
    return pl.pallas_call(
        functools.partial(_ffn_kernel, sub=sub),
        grid=(t // tm, f // tf),
        in_specs=[pl.BlockSpec((tm, d), lambda i, j: (i, 0)),
                  pl.BlockSpec((1,) + mod3.shape[1:], lambda i, j: (i // tps, 0, 0)),
                  pl.BlockSpec((1, d), lambda i, j: (0, 0)),
                  pl.BlockSpec((1, d), lambda i, j: (0, 0)),
                  pl.BlockSpec((None, d, tf), lambda i, j: (j, 0, 0)),
                  pl.BlockSpec((None, d, tf), lambda i, j: (j, 0, 0)),
                  pl.BlockSpec((tf, d), lambda i, j: (j, 0))],
        out_specs=pl.BlockSpec((tm, d), lambda i, j: (i, 0)),
        out_shape=jax.ShapeDtypeStruct((t, d), F32),
        scratch_shapes=[pltpu.VMEM((tm, d), BF16)],
        compiler_params=_cparams(("arbitrary", "arbitrary")),
        name=f"ffn{sub}",
    )(x2, mod3, gpre, gpost, wg, wu, wd)


def _inproj_kernel(x_ref, mod_ref, gpre_ref, w_ref, mu_ref, o_ref, h_ref, carry_ref, *, tps):
    i = pl.program_id(0)
    j = pl.program_id(1)

    @pl.when(j == 0)
    def _():
        h_ref[...] = _norm_mod(x_ref[...], gpre_ref[...], mod_ref, 1).astype(BF16)

    res = jnp.dot(h_ref[...], w_ref[...], preferred_element_type=F32)
    tm, tn = res.shape
    carried = carry_ref[j][0:1, :]
    first = jnp.where(i % tps == 0, jnp.zeros_like(carried), carried)
    row = lax.broadcasted_iota(jnp.int32, (tm, 1), 0)
    prev = jnp.where(row == 0, first, pltpu.roll(res, 1, 0))
    carry_ref[j] = jnp.broadcast_to(res[tm - 1:tm, :], (8, tn))
    o_ref[...] = res + mu_ref[...] * (prev - res)


def _inproj(x2, mod3, gpre, w, mu, *, seq, tm=512, tn=512):
    t, d = x2.shape
    n = w.shape[1]
    tps = seq // tm
    return pl.pallas_call(
        functools.partial(_inproj_kernel, tps=tps),
        grid=(t // tm, n // tn),
        in_specs=[pl.BlockSpec((tm, d), lambda i, j: (i, 0)),
                  pl.BlockSpec((1,) + mod3.shape[1:], lambda i, j: (i // tps, 0, 0)),
                  pl.BlockSpec((1, d), lambda i, j: (0, 0)),
                  pl.BlockSpec((d, tn), lambda i, j: (0, j)),
                  pl.BlockSpec((1, tn), lambda i, j: (0, j))],
        out_specs=pl.BlockSpec((tm, tn), lambda i, j: (i, j)),
        out_shape=jax.ShapeDtypeStruct((t, n), F32),
        scratch_shapes=[pltpu.VMEM((tm, d), BF16), pltpu.VMEM((n // tn, 8, tn), F32)],
        compiler_params=_cparams(("arbitrary", "arbitrary")),
        name="inproj",
    )(x2, mod3, gpre, w, mu)


def _pool_kernel(u_ref, w_ref, s_ref, o_ref, ext_ref, *, tps):
    i = pl.program_id(0)
    tm = u_ref.shape[0]
    pad = max(POOL_WINDOWS)

    @pl.when(i % tps == 0)
    def _():
        ext_ref[0:pad, :] = jnp.zeros((pad, ext_ref.shape[1]), F32)

    ext_ref[pad:pad + tm, :] = u_ref[...]
    t_in_seq = (i % tps) * tm + lax.broadcasted_iota(jnp.int32, (tm, 1), 0)
    tpos = (t_in_seq + 1).astype(F32)
    for gi, win in enumerate(POOL_WINDOWS):
        lo = gi * POOL_GROUP
        acc = ext_ref[pad:pad + tm, lo:lo + POOL_GROUP]
        u_g = acc
        for dlt in range(1, win):
            acc = acc + ext_ref[pad - dlt:pad - dlt + tm, lo:lo + POOL_GROUP]
        pooled = acc / jnp.minimum(tpos, float(win)) - u_g
        mixed = _bdot(pooled, w_ref[gi])
        o_ref[:, lo:lo + POOL_GROUP] = mixed * s_ref[:, lo:lo + POOL_GROUP]
    ext_ref[0:pad, :] = ext_ref[tm:tm + pad, :]


def _pool(p_all, col_block, pool_w, pool_scale, *, seq, tm=512):
    t = p_all.shape[0]
    pw = pool_scale.shape[1]
    tps = seq // tm
    return pl.pallas_call(
        functools.partial(_pool_kernel, tps=tps),
        grid=(t // tm,),
        in_specs=[pl.BlockSpec((tm, pw), lambda i: (i, col_block)),
                  pl.BlockSpec(pool_w.shape, lambda i: (0, 0, 0)),
                  pl.BlockSpec((1, pw), lambda i: (0, 0))],
        out_specs=pl.BlockSpec((tm, pw), lambda i: (i, 0)),
        out_shape=jax.ShapeDtypeStruct((t, pw), F32),
        scratch_shapes=[pltpu.VMEM((tm + max(POOL_WINDOWS), pw), F32)],
        compiler_params=_cparams(("arbitrary",)),
        name="pool",
    )(p_all, pool_w, pool_scale)


def _lora_kernel(p_ref, w2_ref, a2_ref, g2_ref, w0_ref, a0_ref, ld_ref, a_ref, g_ref):
    p = p_ref[...]
    z = w0_ref[...] + _bdot(jnp.tanh(p), w2_ref[...])
    w_log = -(jnp.maximum(-z, 0.0) + jnp.log(1.0 + jnp.exp(-jnp.abs(z)))) - 0.5
    ld_ref[...] = -jnp.exp(w_log)
    a_ref[...] = _sigmoid(a0_ref[...] + _bdot(p, a2_ref[...]))
    g_ref[...] = _bdot(_sigmoid(p), g2_ref[...])


def _lora(p_all, col_block, w2p, a2p, g2p, w0, a0, *, tm=512):
    t = p_all.shape[0]
    kw, r = w2p.shape
    full = lambda shape: pl.BlockSpec(shape, lambda i: (0, 0))
    out = jax.ShapeDtypeStruct((t, r), F32)
    return pl.pallas_call(
        _lora_kernel,
        grid=(t // tm,),
        in_specs=[pl.BlockSpec((tm, kw), lambda i: (i, col_block)),
                  full((kw, r)), full((kw, r)), full((kw, r)), full((1, r)), full((1, r))],
        out_specs=[pl.BlockSpec((tm, r), lambda i: (i, 0))] * 3,
        out_shape=[out, out, out],
        compiler_params=_cparams(("arbitrary",)),
        name="lora",
    )(p_all, w2p, a2p, g2p, w0, a0)


def _split_dot(x, m2):
    hi = x.astype(BF16)
    lo = (x - hi.astype(F32)).astype(BF16)
    return jnp.dot(jnp.concatenate([hi, lo], axis=1), m2, preferred_element_type=F32)


def _scan_kernel(r_ref, k_ref, v_ref, ld_ref, a_ref, g_ref, kkw_ref, ka_ref, rk_ref, lw_ref, lb_ref,
                 o_ref, h_ref):
    C = CHUNK
    P = LANES
    N = HEAD_SIZE

    @pl.when(pl.program_id(2) == 0)
    def _():
        h_ref[...] = jnp.zeros_like(h_ref)

    lane = lax.broadcasted_iota(jnp.int32, (1, P), 1)
    in_a = lane < N
    ri = lax.broadcasted_iota(jnp.int32, (2 * C, 2 * C), 0)
    ci = lax.broadcasted_iota(jnp.int32, (2 * C, 2 * C), 1)

    def same(b):
        sh = b.bit_length() - 1
        return (ri >> sh) == (ci >> sh)

    strict = (ri > ci) & same(C)
    incl = (ri >= ci) & same(C)
    eye = ri == ci
    blk16 = strict & same(16)
    blk32 = strict & same(32) & jnp.logical_not(same(16))
    blk64 = strict & jnp.logical_not(same(32))
    eye_f = eye.astype(F32)
    head_ones = same(N).astype(BF16)
    head_ones2 = jnp.concatenate([head_ones, head_ones], axis=0)
    tri3 = (lax.broadcasted_iota(jnp.int32, (C, 3 * C), 0)
            >= (lax.broadcasted_iota(jnp.int32, (C, 3 * C), 1) & (C - 1))).astype(BF16)

    def stack(x):
        return jnp.concatenate([jnp.where(in_a, x, 0.0), jnp.where(in_a, 0.0, x)], axis=0)

    def headsum(x):
        return _split_dot(x, head_ones2)

    def pair_chunk(c, q):
        sl = pl.ds(c * C, C)
        ln = pl.ds(q * P, P)
        kkw = kkw_ref[:, ln]
        ka = ka_ref[:, ln]
        rk = rk_ref[:, ln]
        lw = lw_ref[:, ln]
        lb = lb_ref[:, ln]
        r = r_ref[sl, ln]
        k = k_ref[sl, ln]
        v = v_ref[sl, ln]
        ld = ld_ref[sl, ln]
        a = a_ref[sl, ln]

        kkraw = k * kkw
        kmod = k * (1.0 + (a - 1.0) * ka)
        sums = headsum(jnp.concatenate([kkraw * kkraw, r * kmod * rk], axis=0))
        ld_hi = ld.astype(BF16)
        rem = ld - ld_hi.astype(F32)
        ld_mid = rem.astype(BF16)
        ld_lo = (rem - ld_mid.astype(F32)).astype(BF16)
        cum = jnp.dot(tri3, jnp.concatenate([ld_hi, ld_mid, ld_lo], axis=0), preferred_element_type=F32)
        yield
        kk = kkraw / jnp.maximum(jnp.sqrt(sums[:C]), 1e-12)
        bonus = sums[C:]
        cum_last = cum[C - 1:C, :]

        inv_g = jnp.exp(-cum)
        to_end = jnp.exp(cum_last - cum)
        kb = kk * a
        rt = r * jnp.exp(cum)
        at = -kk * jnp.exp(cum - ld)
        bt = kb * inv_g
        kt = kmod * inv_g
        g_end = jnp.exp(cum_last)

        s_v = stack(v).astype(BF16)
        lhs = jnp.concatenate([stack(at), stack(rt)], axis=0).astype(BF16)
        rhs = jnp.concatenate([stack(bt), stack(kt)], axis=0).astype(BF16)
        scores = lax.dot_general(lhs, rhs, (((1,), (1,)), ((), ())), preferred_element_type=F32)
        h = h_ref[q]
        h_b = h.astype(BF16)
        s_at = lhs[:2 * C]
        s_rt = lhs[2 * C:]
        yield
        s_ab = scores[:2 * C, :2 * C]
        s_ak = jnp.where(strict, scores[:2 * C, 2 * C:], 0.0)
        s_rb = jnp.where(incl, scores[2 * C:, :2 * C], 0.0)
        s_rk = jnp.where(incl, scores[2 * C:, 2 * C:], 0.0)

        x1 = jnp.where(blk16, s_ab, 0.0)
        x2 = _bdot(x1, x1)
        rhs_sa = jnp.dot(jnp.concatenate([s_at, s_ak.astype(BF16)], axis=1),
                         jnp.concatenate([h_b, s_v], axis=0), preferred_element_type=F32)
        yield
        x4 = _bdot(x2, x2)
        tinv = eye_f + x1
        tinv = tinv + _bdot(tinv, x2)
        yield
        x8 = _bdot(x4, x4)
        tinv = tinv + _bdot(tinv, x4)
        yield
        tinv = tinv + _bdot(tinv, x8)
        yield
        for blk in (blk32, blk64):
            off = jnp.where(blk, s_ab, 0.0)
            part = _bdot(tinv, off)
            yield
            tinv = tinv + _bdot(part, tinv)
            yield

        s_sa = _bdot(tinv, rhs_sa)
        yield
        sa_v = jnp.concatenate([s_sa.astype(BF16), s_v], axis=0)
        y_s = jnp.dot(jnp.concatenate([s_rt, s_rb.astype(BF16), s_rk.astype(BF16)], axis=1),
                      jnp.concatenate([h_b, sa_v], axis=0), preferred_element_type=F32)
        upd_l = jnp.concatenate([stack(kb * to_end), stack(kmod * to_end)], axis=0).astype(BF16)
        upd = lax.dot_general(upd_l, sa_v, (((0,), (0,)), ((), ())), preferred_element_type=F32)
        yield
        y = y_s[:C] + y_s[C:]
        g_col = jnp.sum(jnp.where(eye, g_end, 0.0), axis=1, keepdims=True)
        h_ref[q] = h * g_col + upd

        mean = headsum(y) * (1.0 / N)
        yield
        dev = y - mean
        var = headsum(dev * dev) * (1.0 / N)
        yield
        yn = dev * lax.rsqrt(var + LN_X_EPS) * lw + lb
        o_ref[sl, ln] = (yn + bonus * v) * g_ref[sl, ln]

    for c in range(r_ref.shape[0] // C):
        live = [pair_chunk(c, q) for q in range(r_ref.shape[1] // P)]
        while live:
            advanced = []
            for gen in live:
                try:
                    next(gen)
                    advanced.append(gen)
                except StopIteration:
                    pass
            live = advanced


def _scan(p_all, ld, a, g, kkw, ka, rk, lw, lb, *, batch, seq, r_block, k_block, v_block, tb=CHUNK, pairs=12):
    t, rw = ld.shape
    width = pairs * LANES
    tpb = seq // tb
    tok = lambda blk0: pl.BlockSpec((tb, width), lambda b, p, s: (b * tpb + s, blk0 // pairs + p))
    par = pl.BlockSpec((1, width), lambda b, p, s: (0, p))
    return pl.pallas_call(
        _scan_kernel,
        grid=(batch, rw // width, tpb),
        in_specs=[tok(r_block), tok(k_block), tok(v_block), tok(0), tok(0), tok(0),
                  par, par, par, par, par],
        out_specs=tok(0),
        out_shape=jax.ShapeDtypeStruct((t, rw), F32),
        scratch_shapes=[pltpu.VMEM((pairs, LANES, LANES), F32)],
        compiler_params=_cparams(("arbitrary", "arbitrary", "arbitrary")),
        name="scan",
    )(p_all, p_all, p_all, ld, a, g, kkw, ka, rk, lw, lb)


def _outproj_kernel(x_ref, yp_ref, yr_ref, mod_ref, gpost_ref, wp_ref, wr_ref, o_ref):
    y = (jnp.dot(yp_ref[...].astype(BF16), wp_ref[...], preferred_element_type=F32)
         + jnp.dot(yr_ref[...].astype(BF16), wr_ref[...], preferred_element_type=F32))
    gate = mod_ref[0, N_MOD + 2:N_MOD + 3, :]
    o_ref[...] = x_ref[...] + (1.0 + gate) * _rms(y, gpost_ref[...])


def _outproj(x2, y_pool, y_rwkv, mod3, gpost, wp, wr, *, seq, tm=512):
    t, d = x2.shape
    tps = seq // tm
    return pl.pallas_call(
        _outproj_kernel,
        grid=(t // tm,),
        in_specs=[pl.BlockSpec((tm, d), lambda i: (i, 0)),
                  pl.BlockSpec((tm, y_pool.shape[1]), lambda i: (i, 0)),
                  pl.BlockSpec((tm, y_rwkv.shape[1]), lambda i: (i, 0)),
                  pl.BlockSpec((1,) + mod3.shape[1:], lambda i: (i // tps, 0, 0)),
                  pl.BlockSpec((1, d), lambda i: (0, 0)),
                  pl.BlockSpec(wp.shape, lambda i: (0, 0)),
                  pl.BlockSpec(wr.shape, lambda i: (0, 0))],
        out_specs=pl.BlockSpec((tm, d), lambda i: (i, 0)),
        out_shape=jax.ShapeDtypeStruct((t, d), F32),
        compiler_params=_cparams(("arbitrary",)),
        name="outproj",
    )(x2, y_pool, y_rwkv, mod3, gpost, wp, wr)


def _pad_cols(w, n):
    return jnp.pad(w, ((0, 0), (0, n - w.shape[1])))


def _layer(x2, c, batch, seq, w_ada, b_ada, norm_pre, norm_post, f1g, f1u, f1d, w_in, mu_shift, pool_w,
           pool_scale, w0, w2, a0, a2, g2, k_k, k_a, r_k, lnx_w, lnx_b, w_out, f2g, f2u, f2d):
    d = x2.shape[1]
    pool_width = pool_scale.shape[0]
    rw = w0.shape[0]
    n_sub = norm_pre.shape[0]
    col_tile = 512

    mod3 = _ada(c, w_ada, b_ada).reshape(batch, n_sub * N_MOD, d)
    row = lambda vec: vec.reshape(1, -1)

    x2 = _ffn(x2, mod3, row(norm_pre[0]), row(norm_post[0]), f1g, f1u, f1d.astype(BF16), sub=0, seq=seq)

    n_lora = w_in.shape[1] - pool_width - 3 * rw
    assert (3 * rw) % col_tile == 0 and n_lora <= col_tile and pool_width == col_tile and rw % LANES == 0
    w_in_p = jnp.concatenate([w_in[:, pool_width:pool_width + 3 * rw],
                              _pad_cols(w_in[:, pool_width + 3 * rw:], col_tile),
                              w_in[:, :pool_width]], axis=1).astype(BF16)
    mu_p = jnp.concatenate([mu_shift, jnp.zeros((col_tile - n_lora + pool_width,), F32)])
    p_all = _inproj(x2, mod3, row(norm_pre[1]), w_in_p, row(mu_p), seq=seq, tm=1024,
                    tn=w_in_p.shape[1] // 4)
    lora_block = 3 * rw // col_tile

    y_pool = _pool(p_all, lora_block + 1, pool_w.astype(BF16), row(pool_scale), seq=seq)

    n_w, n_a = w2.shape[0], a2.shape[0]
    w2p = jnp.pad(w2, ((0, col_tile - n_w), (0, 0))).astype(BF16)
    a2p = jnp.pad(a2, ((n_w, col_tile - n_w - n_a), (0, 0))).astype(BF16)
    g2p = jnp.pad(g2, ((n_w + n_a, col_tile - n_lora), (0, 0))).astype(BF16)
    ld, a, g = _lora(p_all, lora_block, w2p, a2p, g2p, row(w0), row(a0))

    y_rwkv = _scan(p_all, ld, a, g, row(k_k), row(k_a), row(r_k), row(lnx_w), row(lnx_b),
                   batch=batch, seq=seq, r_block=0, k_block=rw // LANES, v_block=2 * rw // LANES)

    x2 = _outproj(x2, y_pool, y_rwkv, mod3, row(norm_post[1]), w_out[:pool_width].astype(BF16),
                  w_out[pool_width:].astype(BF16), seq=seq)

    x2 = _ffn(x2, mod3, row(norm_pre[2]), row(norm_post[2]), f2g, f2u, f2d.astype(BF16), sub=2, seq=seq)
    return x2


def kernel(x, c, w_ada, b_ada, norm_pre, norm_post, ffn1_w_gate, ffn1_w_up, ffn1_w_down, w_in, mu_shift,
           pool_w, pool_scale, w0, w2, a0, a2, g2, k_k, k_a, r_k, lnx_w, lnx_b, w_out, ffn2_w_gate,
           ffn2_w_up, ffn2_w_down):
    batch, seq, d = x.shape
    x2 = x.reshape(batch * seq, d)
    for l in range(w_ada.shape[0]):
        x2 = _layer(x2, c, batch, seq, w_ada[l], b_ada[l], norm_pre[l], norm_post[l], ffn1_w_gate[l],
                    ffn1_w_up[l], ffn1_w_down[l], w_in[l], mu_shift[l], pool_w[l], pool_scale[l], w0[l],
                    w2[l], a0[l], a2[l], g2[l], k_k[l], k_a[l], r_k[l], lnx_w[l], lnx_b[l], w_out[l],
                    ffn2_w_gate[l], ffn2_w_up[l], ffn2_w_down[l])
    return x2.reshape(batch, seq, d)
```

```python
import functools

import jax
import jax.numpy as jnp
from jax import lax
from jax.experimental import pallas as pl
from jax.experimental.pallas import tpu as pltpu

F32 = jnp.float32
BF16 = jnp.bfloat16

NORM_EPS = 1e-6
HEAD_SIZE = 64
LN_X_EPS = 1e-5 * HEAD_SIZE
POOL_WINDOWS = (2, 4, 8, 16)
POOL_GROUP = 128
MACARON_WEIGHT = 0.5
N_MOD = 3

LANES = 128
CHUNK = 64
VMEM_LIMIT = 56 * 1024 * 1024


def _cparams(sem):
    return pltpu.CompilerParams(dimension_semantics=sem, vmem_limit_bytes=VMEM_LIMIT)


def _sigmoid(z):
    return 1.0 / (1.0 + jnp.exp(-z))


def _rms(x, gain):
    ms = jnp.mean(x * x, axis=-1, keepdims=True)
    return x * lax.rsqrt(ms + NORM_EPS) * gain


def _norm_mod(x, gain, mod_ref, sub):
    shift = mod_ref[0, N_MOD * sub:N_MOD * sub + 1, :]
    scale = mod_ref[0, N_MOD * sub + 1:N_MOD * sub + 2, :]
    return _rms(x, gain) * (1.0 + scale) + shift


def _bdot(a, b):
    return jnp.dot(a.astype(BF16), b.astype(BF16), preferred_element_type=F32)


def _ada_kernel(c_ref, w_ref, b_ref, o_ref):
    c = c_ref[...]
    s = c * _sigmoid(c)
    o_ref[...] = jnp.dot(s, w_ref[...], preferred_element_type=F32,
                         precision=lax.Precision.HIGHEST) + b_ref[...]


def _ada(c, w, b, tn=1024):
    bsz, d = c.shape
    n = w.shape[1]
    return pl.pallas_call(
        _ada_kernel,
        grid=(n // tn,),
        in_specs=[pl.BlockSpec((bsz, d), lambda j: (0, 0)),
                  pl.BlockSpec((d, tn), lambda j: (0, j)),
                  pl.BlockSpec((1, tn), lambda j: (0, j))],
        out_specs=pl.BlockSpec((bsz, tn), lambda j: (0, j)),
        out_shape=jax.ShapeDtypeStruct((bsz, n), F32),
        compiler_params=_cparams(("arbitrary",)),
        name="ada",
    )(c, w, b.reshape(1, n))


def _ffn_kernel(x_ref, xp_ref, mod_ref, modp_ref, gpre_ref, gpost_ref, wg_ref, wu_ref, wd_ref, o_ref,
                h_ref, act_ref, *, sub, nt):
    i = pl.program_id(0)
    j = pl.program_id(1)
    nf = pl.num_programs(1)
    cur = (i * nf + j) % 2
    prv = 1 - cur

    def up_chunk(h):
        g = jnp.dot(h, wg_ref[...], preferred_element_type=F32)
        u = jnp.dot(h, wu_ref[...], preferred_element_type=F32)
        return (g * _sigmoid(g) * u).astype(BF16)

    def down_chunk():
        return jnp.dot(act_ref[prv], wd_ref[...], preferred_element_type=F32)

    @pl.when((i == 0) & (j == 0))
    def _():
        act_ref[...] = jnp.zeros_like(act_ref)
        o_ref[...] = jnp.zeros_like(o_ref)

    @pl.when(j == 0)
    def _():
        h = _norm_mod(x_ref[...], gpre_ref[...], mod_ref, sub).astype(BF16)
        h_ref[...] = h
        acc = o_ref[...] + down_chunk()
        gate = modp_ref[0, N_MOD * sub + 2:N_MOD * sub + 3, :]
        o_ref[...] = xp_ref[...] + MACARON_WEIGHT * (1.0 + gate) * _rms(acc, gpost_ref[...])
        act_ref[cur] = up_chunk(h)

    @pl.when((j == 1) & (i < nt))
    def _():
        o_ref[...] = down_chunk()
        act_ref[cur] = up_chunk(h_ref[...])

    @pl.when((j > 1) & (i < nt))
    def _():
        o_ref[...] += down_chunk()
        act_ref[cur] = up_chunk(h_ref[...])


def _chunk_major(w, tf):
    d, f = w.shape
    return w.reshape(d, f // tf, tf).transpose(1, 0, 2).astype(BF16)


def _ffn(x2, mod3, gpre, gpost, wg, wu, wd, *, sub, seq, tm=512, tf=512):
    t, d = x2.shape
    f = wg.shape[1]
    tps = seq // tm
    nt, nf = t // tm, f // tf
    wg, wu = _chunk_major(wg, tf), _chunk_major(wu, tf)
    this = lambda i: jnp.minimum(i, nt - 1)
    prev = lambda i: jnp.clip(i - 1, 0, nt - 1)
    modspec = lambda tile: pl.BlockSpec((1,) + mod3.shape[1:], lambda i, j: (tile(i) // tps, 0, 0))
    up_idx = lambda i, j: (jnp.where(i < nt, j, 0), 0, 0)
    down_idx = lambda i, j: (jnp.where((i == nt) & (j > 0), nf - 1, (j + nf - 1) % nf), 0)
    out_idx = lambda i, j: (jnp.clip(i - (j == 0).astype(jnp.int32), 0, nt - 1), 0)
    return pl.pallas_call(
        functools.partial(_ffn_kernel, sub=sub, nt=nt),
        grid=(nt + 1, nf),
        in_specs=[pl.BlockSpec((tm, d), lambda i, j: (this(i), 0)),
                  pl.BlockSpec((tm, d), lambda i, j: (prev(i), 0)),
                  modspec(this), modspec(prev),
                  pl.BlockSpec((1, d), lambda i, j: (0, 0)),
                  pl.BlockSpec((1, d), lambda i, j: (0, 0)),
                  pl.BlockSpec((None, d, tf), up_idx),
                  pl.BlockSpec((None, d, tf), up_idx),
                  pl.BlockSpec((tf, d), down_idx)],
        out_specs=pl.BlockSpec((tm, d), out_idx),
        out_shape=jax.ShapeDtypeStruct((t, d), F32),
        scratch_shapes=[pltpu.VMEM((tm, d), BF16), pltpu.VMEM((2, tm, tf), BF16)],
        compiler_params=_cparams(("arbitrary", "arbitrary")),
        name=f"ffn{sub}",
    )(x2, x2, mod3, mod3, gpre, gpost, wg, wu, wd)


def _inproj_kernel(x_ref, mod_ref, gpre_ref, w_ref, mu_ref, o_ref, h_ref, carry_ref, *, tps):
    i = pl.program_id(0)
    j = pl.program_id(1)

    @pl.when(j == 0)
    def _():
        h_ref[...] = _norm_mod(x_ref[...], gpre_ref[...], mod_ref, 1).astype(BF16)

    res = jnp.dot(h_ref[...], w_ref[...], preferred_element_type=F32)
    tm, tn = res.shape
    carried = carry_ref[j][0:1, :]
    first = jnp.where(i % tps == 0, jnp.zeros_like(carried), carried)
    row = lax.broadcasted_iota(jnp.int32, (tm, 1), 0)
    prev = jnp.where(row == 0, first, pltpu.roll(res, 1, 0))
    carry_ref[j] = jnp.broadcast_to(res[tm - 1:tm, :], (8, tn))
    o_ref[...] = res + mu_ref[...] * (prev - res)


def _inproj(x2, mod3, gpre, w, mu, *, seq, tm=512, tn=512):
    t, d = x2.shape
    n = w.shape[1]
    tps = seq // tm
    return pl.pallas_call(
        functools.partial(_inproj_kernel, tps=tps),
        grid=(t // tm, n // tn),
        in_specs=[pl.BlockSpec((tm, d), lambda i, j: (i, 0)),
                  pl.BlockSpec((1,) + mod3.shape[1:], lambda i, j: (i // tps, 0, 0)),
                  pl.BlockSpec((1, d), lambda i, j: (0, 0)),
                  pl.BlockSpec((d, tn), lambda i, j: (0, j)),
                  pl.BlockSpec((1, tn), lambda i, j: (0, j))],
        out_specs=pl.BlockSpec((tm, tn), lambda i, j: (i, j)),
        out_shape=jax.ShapeDtypeStruct((t, n), F32),
        scratch_shapes=[pltpu.VMEM((tm, d), BF16), pltpu.VMEM((n // tn, 8, tn), F32)],
        compiler_params=_cparams(("arbitrary", "arbitrary")),
        name="inproj",
    )(x2, mod3, gpre, w, mu)


def _pool_kernel(u_ref, w_ref, s_ref, o_ref, ext_ref, *, tps):
    i = pl.program_id(0)
    tm = u_ref.shape[0]
    pad = max(POOL_WINDOWS)

    @pl.when(i % tps == 0)
    def _():
        ext_ref[0:pad, :] = jnp.zeros((pad, ext_ref.shape[1]), F32)

    ext_ref[pad:pad + tm, :] = u_ref[...]
    t_in_seq = (i % tps) * tm + lax.broadcasted_iota(jnp.int32, (tm, 1), 0)
    tpos = (t_in_seq + 1).astype(F32)
    for gi, win in enumerate(POOL_WINDOWS):
        lo = gi * POOL_GROUP
        acc = ext_ref[pad:pad + tm, lo:lo + POOL_GROUP]
        u_g = acc
        for dlt in range(1, win):
            acc = acc + ext_ref[pad - dlt:pad - dlt + tm, lo:lo + POOL_GROUP]
        pooled = acc / jnp.minimum(tpos, float(win)) - u_g
        mixed = _bdot(pooled, w_ref[gi])
        o_ref[:, lo:lo + POOL_GROUP] = mixed * s_ref[:, lo:lo + POOL_GROUP]
    ext_ref[0:pad, :] = ext_ref[tm:tm + pad, :]


def _pool(p_all, col_block, pool_w, pool_scale, *, seq, tm=512):
    t = p_all.shape[0]
    pw = pool_scale.shape[1]
    tps = seq // tm
    return pl.pallas_call(
        functools.partial(_pool_kernel, tps=tps),
        grid=(t // tm,),
        in_specs=[pl.BlockSpec((tm, pw), lambda i: (i, col_block)),
                  pl.BlockSpec(pool_w.shape, lambda i: (0, 0, 0)),
                  pl.BlockSpec((1, pw), lambda i: (0, 0))],
        out_specs=pl.BlockSpec((tm, pw), lambda i: (i, 0)),
        out_shape=jax.ShapeDtypeStruct((t, pw), F32),
        scratch_shapes=[pltpu.VMEM((tm + max(POOL_WINDOWS), pw), F32)],
        compiler_params=_cparams(("arbitrary",)),
        name="pool",
    )(p_all, pool_w, pool_scale)


def _lora_kernel(p_ref, w2_ref, a2_ref, g2_ref, w0_ref, a0_ref, ld_ref, a_ref, g_ref):
    p = p_ref[...]
    z = w0_ref[...] + _bdot(jnp.tanh(p), w2_ref[...])
    w_log = -(jnp.maximum(-z, 0.0) + jnp.log(1.0 + jnp.exp(-jnp.abs(z)))) - 0.5
    ld_ref[...] = -jnp.exp(w_log)
    a_ref[...] = _sigmoid(a0_ref[...] + _bdot(p, a2_ref[...]))
    g_ref[...] = _bdot(_sigmoid(p), g2_ref[...])


def _lora(p_all, col_block, w2p, a2p, g2p, w0, a0, *, tm=512):
    t = p_all.shape[0]
    kw, r = w2p.shape
    full = lambda shape: pl.BlockSpec(shape, lambda i: (0, 0))
    out = jax.ShapeDtypeStruct((t, r), F32)
    return pl.pallas_call(
        _lora_kernel,
        grid=(t // tm,),
        in_specs=[pl.BlockSpec((tm, kw), lambda i: (i, col_block)),
                  full((kw, r)), full((kw, r)), full((kw, r)), full((1, r)), full((1, r))],
        out_specs=[pl.BlockSpec((tm, r), lambda i: (i, 0))] * 3,
        out_shape=[out, out, out],
        compiler_params=_cparams(("arbitrary",)),
        name="lora",
    )(p_all, w2p, a2p, g2p, w0, a0)


def _split_dot(x, m2):
    hi = x.astype(BF16)
    lo = (x - hi.astype(F32)).astype(BF16)
    return jnp.dot(jnp.concatenate([hi, lo], axis=1), m2, preferred_element_type=F32)


def _scan_kernel(r_ref, k_ref, v_ref, ld_ref, a_ref, g_ref, kkw_ref, ka_ref, rk_ref, lw_ref, lb_ref,
                 o_ref, h_ref):
    C = CHUNK
    P = LANES
    N = HEAD_SIZE

    @pl.when(pl.program_id(2) == 0)
    def _():
        h_ref[...] = jnp.zeros_like(h_ref)

    lane = lax.broadcasted_iota(jnp.int32, (1, P), 1)
    in_a = lane < N
    ri = lax.broadcasted_iota(jnp.int32, (2 * C, 2 * C), 0)
    ci = lax.broadcasted_iota(jnp.int32, (2 * C, 2 * C), 1)

    def same(b):
        sh = b.bit_length() - 1
        return (ri >> sh) == (ci >> sh)

    strict = (ri > ci) & same(C)
    incl = (ri >= ci) & same(C)
    eye = ri == ci
    blk16 = strict & same(16)
    blk32 = strict & same(32) & jnp.logical_not(same(16))
    blk64 = strict & jnp.logical_not(same(32))
    eye_f = eye.astype(F32)
    head_ones = same(N).astype(BF16)
    head_ones2 = jnp.concatenate([head_ones, head_ones], axis=0)
    tri3 = (lax.broadcasted_iota(jnp.int32, (C, 3 * C), 0)
            >= (lax.broadcasted_iota(jnp.int32, (C, 3 * C), 1) & (C - 1))).astype(BF16)

    def stack(x):
        return jnp.concatenate([jnp.where(in_a, x, 0.0), jnp.where(in_a, 0.0, x)], axis=0)

    def headsum(x):
        return _split_dot(x, head_ones2)

    def pair_chunk(c, q):
        sl = pl.ds(c * C, C)
        ln = pl.ds(q * P, P)
        kkw = kkw_ref[:, ln]
        ka = ka_ref[:, ln]
        rk = rk_ref[:, ln]
        lw = lw_ref[:, ln]
        lb = lb_ref[:, ln]
        r = r_ref[sl, ln]
        k = k_ref[sl, ln]
        v = v_ref[sl, ln]
        ld = ld_ref[sl, ln]
        a = a_ref[sl, ln]

        kkraw = k * kkw
        kmod = k * (1.0 + (a - 1.0) * ka)
        sums = headsum(jnp.concatenate([kkraw * kkraw, r * kmod * rk], axis=0))
        ld_hi = ld.astype(BF16)
        rem = ld - ld_hi.astype(F32)
        ld_mid = rem.astype(BF16)
        ld_lo = (rem - ld_mid.astype(F32)).astype(BF16)
        cum = jnp.dot(tri3, jnp.concatenate([ld_hi, ld_mid, ld_lo], axis=0), preferred_element_type=F32)
        yield
        kk = kkraw / jnp.maximum(jnp.sqrt(sums[:C]), 1e-12)
        bonus = sums[C:]
        cum_last = cum[C - 1:C, :]

        inv_g = jnp.exp(-cum)
        to_end = jnp.exp(cum_last - cum)
        kb = kk * a
        rt = r * jnp.exp(cum)
        at = -kk * jnp.exp(cum - ld)
        bt = kb * inv_g
        kt = kmod * inv_g
        g_end = jnp.exp(cum_last)

        s_v = stack(v).astype(BF16)
        lhs = jnp.concatenate([stack(at), stack(rt)], axis=0).astype(BF16)
        rhs = jnp.concatenate([stack(bt), stack(kt)], axis=0).astype(BF16)
        scores = lax.dot_general(lhs, rhs, (((1,), (1,)), ((), ())), preferred_element_type=F32)
        h = h_ref[q]
        h_b = h.astype(BF16)
        s_at = lhs[:2 * C]
        s_rt = lhs[2 * C:]
        yield
        s_ab = scores[:2 * C, :2 * C]
        s_ak = jnp.where(strict, scores[:2 * C, 2 * C:], 0.0)
        s_rb = jnp.where(incl, scores[2 * C:, :2 * C], 0.0)
        s_rk = jnp.where(incl, scores[2 * C:, 2 * C:], 0.0)

        x1 = jnp.where(blk16, s_ab, 0.0)
        x2 = _bdot(x1, x1)
        rhs_sa = jnp.dot(jnp.concatenate([s_at, s_ak.astype(BF16)], axis=1),
                         jnp.concatenate([h_b, s_v], axis=0), preferred_element_type=F32)
        yield
        x4 = _bdot(x2, x2)
        tinv = eye_f + x1
        tinv = tinv + _bdot(tinv, x2)
        yield
        x8 = _bdot(x4, x4)
        tinv = tinv + _bdot(tinv, x4)
        yield
        tinv = tinv + _bdot(tinv, x8)
        yield
        for blk in (blk32, blk64):
            off = jnp.where(blk, s_ab, 0.0)
            part = _bdot(tinv, off)
            yield
            tinv = tinv + _bdot(part, tinv)
            yield

        s_sa = _bdot(tinv, rhs_sa)
        yield
        sa_v = jnp.concatenate([s_sa.astype(BF16), s_v], axis=0)
        y_s = jnp.dot(jnp.concatenate([s_rt, s_rb.astype(BF16), s_rk.astype(BF16)], axis=1),
                      jnp.concatenate([h_b, sa_v], axis=0), preferred_element_type=F32)
        upd_l = jnp.concatenate([stack(kb * to_end), stack(kmod * to_end)], axis=0).astype(BF16)
        upd = lax.dot_general(upd_l, sa_v, (((0,), (0,)), ((), ())), preferred_element_type=F32)
        yield
        y = y_s[:C] + y_s[C:]
        g_col = jnp.sum(jnp.where(eye, g_end, 0.0), axis=1, keepdims=True)
        h_ref[q] = h * g_col + upd

        mean = headsum(y) * (1.0 / N)
        yield
        dev = y - mean
        var = headsum(dev * dev) * (1.0 / N)
        yield
        yn = dev * lax.rsqrt(var + LN_X_EPS) * lw + lb
        o_ref[sl, ln] = (yn + bonus * v) * g_ref[sl, ln]

    for c in range(r_ref.shape[0] // C):
        live = [pair_chunk(c, q) for q in range(r_ref.shape[1] // P)]
        while live:
            advanced = []
            for gen in live:
                try:
                    next(gen)
                    advanced.append(gen)
                except StopIteration:
                    pass
            live = advanced


def _scan(p_all, ld, a, g, kkw, ka, rk, lw, lb, *, batch, seq, r_block, k_block, v_block, tb=CHUNK, pairs=12):
    t, rw = ld.shape
    width = pairs * LANES
    tpb = seq // tb
    tok = lambda blk0: pl.BlockSpec((tb, width), lambda b, p, s: (b * tpb + s, blk0 // pairs + p))
    par = pl.BlockSpec((1, width), lambda b, p, s: (0, p))
    return pl.pallas_call(
        _scan_kernel,
        grid=(batch, rw // width, tpb),
        in_specs=[tok(r_block), tok(k_block), tok(v_block), tok(0), tok(0), tok(0),
                  par, par, par, par, par],
        out_specs=tok(0),
        out_shape=jax.ShapeDtypeStruct((t, rw), F32),
        scratch_shapes=[pltpu.VMEM((pairs, LANES, LANES), F32)],
        compiler_params=_cparams(("arbitrary", "arbitrary", "arbitrary")),
        name="scan",
    )(p_all, p_all, p_all, ld, a, g, kkw, ka, rk, lw, lb)


def _outproj_kernel(x_ref, yp_ref, yr_ref, mod_ref, gpost_ref, wp_ref, wr_ref, o_ref):
    y = (jnp.dot(yp_ref[...].astype(BF16), wp_ref[...], preferred_element_type=F32)
         + jnp.dot(yr_ref[...].astype(BF16), wr_ref[...], preferred_element_type=F32))
    gate = mod_ref[0, N_MOD + 2:N_MOD + 3, :]
    o_ref[...] = x_ref[...] + (1.0 + gate) * _rms(y, gpost_ref[...])


def _outproj(x2, y_pool, y_rwkv, mod3, gpost, wp, wr, *, seq, tm=512):
    t, d = x2.shape
    tps = seq // tm
    return pl.pallas_call(
        _outproj_kernel,
        grid=(t // tm,),
        in_specs=[pl.BlockSpec((tm, d), lambda i: (i, 0)),
                  pl.BlockSpec((tm, y_pool.shape[1]), lambda i: (i, 0)),
                  pl.BlockSpec((tm, y_rwkv.shape[1]), lambda i: (i, 0)),
                  pl.BlockSpec((1,) + mod3.shape[1:], lambda i: (i // tps, 0, 0)),
                  pl.BlockSpec((1, d), lambda i: (0, 0)),
                  pl.BlockSpec(wp.shape, lambda i: (0, 0)),
                  pl.BlockSpec(wr.shape, lambda i: (0, 0))],
        out_specs=pl.BlockSpec((tm, d), lambda i: (i, 0)),
        out_shape=jax.ShapeDtypeStruct((t, d), F32),
        compiler_params=_cparams(("arbitrary",)),
        name="outproj",
    )(x2, y_pool, y_rwkv, mod3, gpost, wp, wr)


def _pad_cols(w, n):
    return jnp.pad(w, ((0, 0), (0, n - w.shape[1])))


def _layer(x2, c, batch, seq, w_ada, b_ada, norm_pre, norm_post, f1g, f1u, f1d, w_in, mu_shift, pool_w,
           pool_scale, w0, w2, a0, a2, g2, k_k, k_a, r_k, lnx_w, lnx_b, w_out, f2g, f2u, f2d):
    d = x2.shape[1]
    pool_width = pool_scale.shape[0]
    rw = w0.shape[0]
    n_sub = norm_pre.shape[0]
    col_tile = 512

    mod3 = _ada(c, w_ada, b_ada).reshape(batch, n_sub * N_MOD, d)
    row = lambda vec: vec.reshape(1, -1)

    x2 = _ffn(x2, mod3, row(norm_pre[0]), row(norm_post[0]), f1g, f1u, f1d.astype(BF16), sub=0, seq=seq)

    n_lora = w_in.shape[1] - pool_width - 3 * rw
    assert (3 * rw) % col_tile == 0 and n_lora <= col_tile and pool_width == col_tile and rw % LANES == 0
    w_in_p = jnp.concatenate([w_in[:, pool_width:pool_width + 3 * rw],
                              _pad_cols(w_in[:, pool_width + 3 * rw:], col_tile),
                              w_in[:, :pool_width]], axis=1).astype(BF16)
    mu_p = jnp.concatenate([mu_shift, jnp.zeros((col_tile - n_lora + pool_width,), F32)])
    p_all = _inproj(x2, mod3, row(norm_pre[1]), w_in_p, row(mu_p), seq=seq, tm=1024,
                    tn=w_in_p.shape[1] // 4)
    lora_block = 3 * rw // col_tile

    y_pool = _pool(p_all, lora_block + 1, pool_w.astype(BF16), row(pool_scale), seq=seq)

    n_w, n_a = w2.shape[0], a2.shape[0]
    w2p = jnp.pad(w2, ((0, col_tile - n_w), (0, 0))).astype(BF16)
    a2p = jnp.pad(a2, ((n_w, col_tile - n_w - n_a), (0, 0))).astype(BF16)
    g2p = jnp.pad(g2, ((n_w + n_a, col_tile - n_lora), (0, 0))).astype(BF16)
    ld, a, g = _lora(p_all, lora_block, w2p, a2p, g2p, row(w0), row(a0))

    y_rwkv = _scan(p_all, ld, a, g, row(k_k), row(k_a), row(r_k), row(lnx_w), row(lnx_b),
                   batch=batch, seq=seq, r_block=0, k_block=rw // LANES, v_block=2 * rw // LANES)

    x2 = _outproj(x2, y_pool, y_rwkv, mod3, row(norm_post[1]), w_out[:pool_width].astype(BF16),
                  w_out[pool_width:].astype(BF16), seq=seq)

    x2 = _ffn(x2, mod3, row(norm_pre[2]), row(norm_post[2]), f2g, f2u, f2d.astype(BF16), sub=2, seq=seq)
    return x2


def kernel(x, c, w_ada, b_ada, norm_pre, norm_post, ffn1_w_gate, ffn1_w_up, ffn1_w_down, w_in, mu_shift,
           pool_w, pool_scale, w0, w2, a0, a2, g2, k_k, k_a, r_k, lnx_w, lnx_b, w_out, ffn2_w_gate,
           ffn2_w_up, ffn2_w_down):
    batch, seq, d = x.shape
    x2 = x.reshape(batch * seq, d)
    for l in range(w_ada.shape[0]):
        x2 = _layer(x2, c, batch, seq, w_ada[l], b_ada[l], norm_pre[l], norm_post[l], ffn1_w_gate[l],
                    ffn1_w_up[l], ffn1_w_down[l], w_in[l], mu_shift[l], pool_w[l], pool_scale[l], w0[l],
                    w2[l], a0[l], a2[l], g2[l], k_k[l], k_a[l], r_k[l], lnx_w[l], lnx_b[l], w_out[l],
                    ffn2_w_gate[l], ffn2_w_up[l], ffn2_w_down[l])
    return x2.reshape(batch, seq, d)
```

```python
import functools

import jax
import jax.numpy as jnp
from jax import lax
from jax.experimental import pallas as pl
from jax.experimental.pallas import tpu as pltpu

F32 = jnp.float32
BF16 = jnp.bfloat16

NORM_EPS = 1e-6
HEAD_SIZE = 64
LN_X_EPS = 1e-5 * HEAD_SIZE
POOL_WINDOWS = (2, 4, 8, 16)
POOL_GROUP = 128
MACARON_WEIGHT = 0.5
N_MOD = 3

LANES = 128
CHUNK = 64
ROW_BLOCK = 16
VMEM_LIMIT = 56 * 1024 * 1024


def _cparams(sem):
    return pltpu.CompilerParams(dimension_semantics=sem, vmem_limit_bytes=VMEM_LIMIT)


def _sigmoid(z):
    return 1.0 / (1.0 + jnp.exp(-z))


def _row_blocks(n_rows, fn):
    for b in range(n_rows // ROW_BLOCK):
        fn(pl.ds(b * ROW_BLOCK, ROW_BLOCK))


def _norm_mod_into(h_ref, x_ref, gain_ref, mod_ref, sub):
    shift = mod_ref[0, N_MOD * sub:N_MOD * sub + 1, :]
    scale = mod_ref[0, N_MOD * sub + 1:N_MOD * sub + 2, :]
    mult = gain_ref[...] * (1.0 + scale)

    def block(rows):
        x = x_ref[rows, :]
        inv = lax.rsqrt(jnp.mean(x * x, axis=-1, keepdims=True) + NORM_EPS)
        h_ref[rows, :] = (x * inv * mult + shift).astype(BF16)

    _row_blocks(x_ref.shape[0], block)


def _residual_into(o_ref, x_ref, gain_ref, mod_ref, sub, weight):
    gate = mod_ref[0, N_MOD * sub + 2:N_MOD * sub + 3, :]
    mult = gain_ref[...] * (weight * (1.0 + gate))

    def block(rows):
        y = o_ref[rows, :]
        inv = lax.rsqrt(jnp.mean(y * y, axis=-1, keepdims=True) + NORM_EPS)
        o_ref[rows, :] = x_ref[rows, :] + y * inv * mult

    _row_blocks(o_ref.shape[0], block)


def _bdot(a, b):
    return jnp.dot(a.astype(BF16), b.astype(BF16), preferred_element_type=F32)


def _ada_kernel(c_ref, w_ref, b_ref, o_ref):
    c = c_ref[...]
    s = c * _sigmoid(c)
    o_ref[...] = jnp.dot(s, w_ref[...], preferred_element_type=F32,
                         precision=lax.Precision.HIGHEST) + b_ref[...]


def _ada(c, w, b, tn=1024):
    bsz, d = c.shape
    n = w.shape[1]
    return pl.pallas_call(
        _ada_kernel,
        grid=(n // tn,),
        in_specs=[pl.BlockSpec((bsz, d), lambda j: (0, 0)),
                  pl.BlockSpec((d, tn), lambda j: (0, j)),
                  pl.BlockSpec((1, tn), lambda j: (0, j))],
        out_specs=pl.BlockSpec((bsz, tn), lambda j: (0, j)),
        out_shape=jax.ShapeDtypeStruct((bsz, n), F32),
        compiler_params=_cparams(("arbitrary",)),
        name="ada",
    )(c, w, b.reshape(1, n))


def _ffn_kernel(x_ref, mod_ref, gpre_ref, gpost_ref, wg_ref, wu_ref, wd_ref, o_ref, h_ref, *, sub):
    j = pl.program_id(1)

    @pl.when(j == 0)
    def _():
        _norm_mod_into(h_ref, x_ref, gpre_ref, mod_ref, sub)
        o_ref[...] = jnp.zeros_like(o_ref)

    h = h_ref[...]
    g = jnp.dot(h, wg_ref[...], preferred_element_type=F32)
    u = jnp.dot(h, wu_ref[...], preferred_element_type=F32)
    act = (g * _sigmoid(g) * u).astype(BF16)
    o_ref[...] += jnp.dot(act, wd_ref[...], preferred_element_type=F32)

    @pl.when(j == pl.num_programs(1) - 1)
    def _():
        _residual_into(o_ref, x_ref, gpost_ref, mod_ref, sub, MACARON_WEIGHT)


def _ffn(x2, mod3, gpre, gpost, wg, wu, wd, *, sub, seq, tm=512, tf=512):
    t, d = x2.shape
    f = wg.shape[1]
    tps = seq // tm
    return pl.pallas_call(
        functools.partial(_ffn_kernel, sub=sub),
        grid=(t // tm, f // tf),
        in_specs=[pl.BlockSpec((tm, d), lambda i, j: (i, 0)),
                  pl.BlockSpec((1,) + mod3.shape[1:], lambda i, j: (i // tps, 0, 0)),
                  pl.BlockSpec((1, d), lambda i, j: (0, 0)),
                  pl.BlockSpec((1, d), lambda i, j: (0, 0)),
                  pl.BlockSpec((d, tf), lambda i, j: (0, j)),
                  pl.BlockSpec((d, tf), lambda i, j: (0, j)),
                  pl.BlockSpec((tf, d), lambda i, j: (j, 0))],
        out_specs=pl.BlockSpec((tm, d), lambda i, j: (i, 0)),
        out_shape=jax.ShapeDtypeStruct((t, d), F32),
        scratch_shapes=[pltpu.VMEM((tm, d), BF16)],
        compiler_params=_cparams(("arbitrary", "arbitrary")),
        name=f"ffn{sub}",
    )(x2, mod3, gpre, gpost, wg, wu, wd)


def _inproj_kernel(x_ref, mod_ref, gpre_ref, w_ref, mu_ref, o_ref, h_ref, carry_ref, *, tps):
    i = pl.program_id(0)
    j = pl.program_id(1)

    @pl.when(j == 0)
    def _():
        _norm_mod_into(h_ref, x_ref, gpre_ref, mod_ref, 1)

    res = jnp.dot(h_ref[...], w_ref[...], preferred_element_type=F32)
    tm, tn = res.shape
    carried = carry_ref[j][0:1, :]
    first = jnp.where(i % tps == 0, jnp.zeros_like(carried), carried)
    row = lax.broadcasted_iota(jnp.int32, (tm, 1), 0)
    prev = jnp.where(row == 0, first, pltpu.roll(res, 1, 0))
    carry_ref[j] = jnp.broadcast_to(res[tm - 1:tm, :], (8, tn))
    o_ref[...] = res + mu_ref[...] * (prev - res)


def _inproj(x2, mod3, gpre, w, mu, *, seq, tm=512, tn=512):
    t, d = x2.shape
    n = w.shape[1]
    tps = seq // tm
    return pl.pallas_call(
        functools.partial(_inproj_kernel, tps=tps),
        grid=(t // tm, n // tn),
        in_specs=[pl.BlockSpec((tm, d), lambda i, j: (i, 0)),
                  pl.BlockSpec((1,) + mod3.shape[1:], lambda i, j: (i // tps, 0, 0)),
                  pl.BlockSpec((1, d), lambda i, j: (0, 0)),
                  pl.BlockSpec((d, tn), lambda i, j: (0, j)),
                  pl.BlockSpec((1, tn), lambda i, j: (0, j))],
        out_specs=pl.BlockSpec((tm, tn), lambda i, j: (i, j)),
        out_shape=jax.ShapeDtypeStruct((t, n), F32),
        scratch_shapes=[pltpu.VMEM((tm, d), BF16), pltpu.VMEM((n // tn, 8, tn), F32)],
        compiler_params=_cparams(("arbitrary", "arbitrary")),
        name="inproj",
    )(x2, mod3, gpre, w, mu)


def _pool_kernel(u_ref, w_ref, s_ref, o_ref, ext_ref, *, tps):
    i = pl.program_id(0)
    tm = u_ref.shape[0]
    pad = max(POOL_WINDOWS)

    @pl.when(i % tps == 0)
    def _():
        ext_ref[0:pad, :] = jnp.zeros((pad, ext_ref.shape[1]), F32)

    ext_ref[pad:pad + tm, :] = u_ref[...]
    t_in_seq = (i % tps) * tm + lax.broadcasted_iota(jnp.int32, (tm, 1), 0)
    tpos = (t_in_seq + 1).astype(F32)
    for gi, win in enumerate(POOL_WINDOWS):
        lo = gi * POOL_GROUP
        acc = ext_ref[pad:pad + tm, lo:lo + POOL_GROUP]
        u_g = acc
        for dlt in range(1, win):
            acc = acc + ext_ref[pad - dlt:pad - dlt + tm, lo:lo + POOL_GROUP]
        pooled = acc / jnp.minimum(tpos, float(win)) - u_g
        mixed = _bdot(pooled, w_ref[gi])
        o_ref[:, lo:lo + POOL_GROUP] = mixed * s_ref[:, lo:lo + POOL_GROUP]
    ext_ref[0:pad, :] = ext_ref[tm:tm + pad, :]


def _pool(p_all, col_block, pool_w, pool_scale, *, seq, tm=512):
    t = p_all.shape[0]
    pw = pool_scale.shape[1]
    tps = seq // tm
    return pl.pallas_call(
        functools.partial(_pool_kernel, tps=tps),
        grid=(t // tm,),
        in_specs=[pl.BlockSpec((tm, pw), lambda i: (i, col_block)),
                  pl.BlockSpec(pool_w.shape, lambda i: (0, 0, 0)),
                  pl.BlockSpec((1, pw), lambda i: (0, 0))],
        out_specs=pl.BlockSpec((tm, pw), lambda i: (i, 0)),
        out_shape=jax.ShapeDtypeStruct((t, pw), F32),
        scratch_shapes=[pltpu.VMEM((tm + max(POOL_WINDOWS), pw), F32)],
        compiler_params=_cparams(("arbitrary",)),
        name="pool",
    )(p_all, pool_w, pool_scale)


def _lora_kernel(p_ref, w2_ref, a2_ref, g2_ref, w0_ref, a0_ref, ld_ref, a_ref, g_ref):
    p = p_ref[...]
    z = w0_ref[...] + _bdot(jnp.tanh(p), w2_ref[...])
    w_log = -(jnp.maximum(-z, 0.0) + jnp.log(1.0 + jnp.exp(-jnp.abs(z)))) - 0.5
    ld_ref[...] = -jnp.exp(w_log)
    a_ref[...] = _sigmoid(a0_ref[...] + _bdot(p, a2_ref[...]))
    g_ref[...] = _bdot(_sigmoid(p), g2_ref[...])


def _lora(p_all, col_block, w2p, a2p, g2p, w0, a0, *, tm=512):
    t = p_all.shape[0]
    kw, r = w2p.shape
    full = lambda shape: pl.BlockSpec(shape, lambda i: (0, 0))
    out = jax.ShapeDtypeStruct((t, r), F32)
    return pl.pallas_call(
        _lora_kernel,
        grid=(t // tm,),
        in_specs=[pl.BlockSpec((tm, kw), lambda i: (i, col_block)),
                  full((kw, r)), full((kw, r)), full((kw, r)), full((1, r)), full((1, r))],
        out_specs=[pl.BlockSpec((tm, r), lambda i: (i, 0))] * 3,
        out_shape=[out, out, out],
        compiler_params=_cparams(("arbitrary",)),
        name="lora",
    )(p_all, w2p, a2p, g2p, w0, a0)


def _split_dot(x, m2):
    hi = x.astype(BF16)
    lo = (x - hi.astype(F32)).astype(BF16)
    return jnp.dot(jnp.concatenate([hi, lo], axis=1), m2, preferred_element_type=F32)


def _scan_kernel(r_ref, k_ref, v_ref, ld_ref, a_ref, g_ref, kkw_ref, ka_ref, rk_ref, lw_ref, lb_ref,
                 o_ref, h_ref):
    C = CHUNK
    P = LANES
    N = HEAD_SIZE

    @pl.when(pl.program_id(2) == 0)
    def _():
        h_ref[...] = jnp.zeros_like(h_ref)

    lane = lax.broadcasted_iota(jnp.int32, (1, P), 1)
    in_a = lane < N
    ri = lax.broadcasted_iota(jnp.int32, (2 * C, 2 * C), 0)
    ci = lax.broadcasted_iota(jnp.int32, (2 * C, 2 * C), 1)

    def same(b):
        sh = b.bit_length() - 1
        return (ri >> sh) == (ci >> sh)

    strict = (ri > ci) & same(C)
    incl = (ri >= ci) & same(C)
    eye = ri == ci
    blk16 = strict & same(16)
    blk32 = strict & same(32) & jnp.logical_not(same(16))
    blk64 = strict & jnp.logical_not(same(32))
    eye_f = eye.astype(F32)
    head_ones = same(N).astype(BF16)
    head_ones2 = jnp.concatenate([head_ones, head_ones], axis=0)
    tri3 = (lax.broadcasted_iota(jnp.int32, (C, 3 * C), 0)
            >= (lax.broadcasted_iota(jnp.int32, (C, 3 * C), 1) & (C - 1))).astype(BF16)

    def stack(x):
        return jnp.concatenate([jnp.where(in_a, x, 0.0), jnp.where(in_a, 0.0, x)], axis=0)

    def headsum(x):
        return _split_dot(x, head_ones2)

    def pair_chunk(c, q):
        sl = pl.ds(c * C, C)
        ln = pl.ds(q * P, P)
        kkw = kkw_ref[:, ln]
        ka = ka_ref[:, ln]
        rk = rk_ref[:, ln]
        lw = lw_ref[:, ln]
        lb = lb_ref[:, ln]
        r = r_ref[sl, ln]
        k = k_ref[sl, ln]
        v = v_ref[sl, ln]
        ld = ld_ref[sl, ln]
        a = a_ref[sl, ln]

        kkraw = k * kkw
        kmod = k * (1.0 + (a - 1.0) * ka)
        sums = headsum(jnp.concatenate([kkraw * kkraw, r * kmod * rk], axis=0))
        ld_hi = ld.astype(BF16)
        rem = ld - ld_hi.astype(F32)
        ld_mid = rem.astype(BF16)
        ld_lo = (rem - ld_mid.astype(F32)).astype(BF16)
        cum = jnp.dot(tri3, jnp.concatenate([ld_hi, ld_mid, ld_lo], axis=0), preferred_element_type=F32)
        yield
        kk = kkraw / jnp.maximum(jnp.sqrt(sums[:C]), 1e-12)
        bonus = sums[C:]
        cum_last = cum[C - 1:C, :]

        inv_g = jnp.exp(-cum)
        to_end = jnp.exp(cum_last - cum)
        kb = kk * a
        rt = r * jnp.exp(cum)
        at = -kk * jnp.exp(cum - ld)
        bt = kb * inv_g
        kt = kmod * inv_g
        g_end = jnp.exp(cum_last)

        s_v = stack(v).astype(BF16)
        lhs = jnp.concatenate([stack(at), stack(rt)], axis=0).astype(BF16)
        rhs = jnp.concatenate([stack(bt), stack(kt)], axis=0).astype(BF16)
        scores = lax.dot_general(lhs, rhs, (((1,), (1,)), ((), ())), preferred_element_type=F32)
        h = h_ref[q]
        h_b = h.astype(BF16)
        s_at = lhs[:2 * C]
        s_rt = lhs[2 * C:]
        yield
        s_ab = scores[:2 * C, :2 * C]
        s_ak = jnp.where(strict, scores[:2 * C, 2 * C:], 0.0)
        s_rb = jnp.where(incl, scores[2 * C:, :2 * C], 0.0)
        s_rk = jnp.where(incl, scores[2 * C:, 2 * C:], 0.0)

        x1 = jnp.where(blk16, s_ab, 0.0)
        x2 = _bdot(x1, x1)
        rhs_sa = jnp.dot(jnp.concatenate([s_at, s_ak.astype(BF16)], axis=1),
                         jnp.concatenate([h_b, s_v], axis=0), preferred_element_type=F32)
        yield
        x4 = _bdot(x2, x2)
        tinv = eye_f + x1
        tinv = tinv + _bdot(tinv, x2)
        yield
        x8 = _bdot(x4, x4)
        tinv = tinv + _bdot(tinv, x4)
        yield
        tinv = tinv + _bdot(tinv, x8)
        yield
        for blk in (blk32, blk64):
            off = jnp.where(blk, s_ab, 0.0)
            part = _bdot(tinv, off)
            yield
            tinv = tinv + _bdot(part, tinv)
            yield

        s_sa = _bdot(tinv, rhs_sa)
        yield
        sa_v = jnp.concatenate([s_sa.astype(BF16), s_v], axis=0)
        y_s = jnp.dot(jnp.concatenate([s_rt, s_rb.astype(BF16), s_rk.astype(BF16)], axis=1),
                      jnp.concatenate([h_b, sa_v], axis=0), preferred_element_type=F32)
        upd_l = jnp.concatenate([stack(kb * to_end), stack(kmod * to_end)], axis=0).astype(BF16)
        upd = lax.dot_general(upd_l, sa_v, (((0,), (0,)), ((), ())), preferred_element_type=F32)
        yield
        y = y_s[:C] + y_s[C:]
        g_col = jnp.sum(jnp.where(eye, g_end, 0.0), axis=1, keepdims=True)
        h_ref[q] = h * g_col + upd

        mean = headsum(y) * (1.0 / N)
        yield
        dev = y - mean
        var = headsum(dev * dev) * (1.0 / N)
        yield
        yn = dev * lax.rsqrt(var + LN_X_EPS) * lw + lb
        o_ref[sl, ln] = (yn + bonus * v) * g_ref[sl, ln]

    for c in range(r_ref.shape[0] // C):
        live = [pair_chunk(c, q) for q in range(r_ref.shape[1] // P)]
        while live:
            advanced = []
            for gen in live:
                try:
                    next(gen)
                    advanced.append(gen)
                except StopIteration:
                    pass
            live = advanced


def _scan(p_all, ld, a, g, kkw, ka, rk, lw, lb, *, batch, seq, r_block, k_block, v_block, tb=CHUNK, pairs=12):
    t, rw = ld.shape
    width = pairs * LANES
    tpb = seq // tb
    tok = lambda blk0: pl.BlockSpec((tb, width), lambda b, p, s: (b * tpb + s, blk0 // pairs + p))
    par = pl.BlockSpec((1, width), lambda b, p, s: (0, p))
    return pl.pallas_call(
        _scan_kernel,
        grid=(batch, rw // width, tpb),
        in_specs=[tok(r_block), tok(k_block), tok(v_block), tok(0), tok(0), tok(0),
                  par, par, par, par, par],
        out_specs=tok(0),
        out_shape=jax.ShapeDtypeStruct((t, rw), F32),
        scratch_shapes=[pltpu.VMEM((pairs, LANES, LANES), F32)],
        compiler_params=_cparams(("arbitrary", "arbitrary", "arbitrary")),
        name="scan",
    )(p_all, p_all, p_all, ld, a, g, kkw, ka, rk, lw, lb)


def _outproj_kernel(x_ref, yp_ref, yr_ref, mod_ref, gpost_ref, wp_ref, wr_ref, o_ref):
    o_ref[...] = (jnp.dot(yp_ref[...].astype(BF16), wp_ref[...], preferred_element_type=F32)
                  + jnp.dot(yr_ref[...].astype(BF16), wr_ref[...], preferred_element_type=F32))
    _residual_into(o_ref, x_ref, gpost_ref, mod_ref, 1, 1.0)


def _outproj(x2, y_pool, y_rwkv, mod3, gpost, wp, wr, *, seq, tm=512):
    t, d = x2.shape
    tps = seq // tm
    return pl.pallas_call(
        _outproj_kernel,
        grid=(t // tm,),
        in_specs=[pl.BlockSpec((tm, d), lambda i: (i, 0)),
                  pl.BlockSpec((tm, y_pool.shape[1]), lambda i: (i, 0)),
                  pl.BlockSpec((tm, y_rwkv.shape[1]), lambda i: (i, 0)),
                  pl.BlockSpec((1,) + mod3.shape[1:], lambda i: (i // tps, 0, 0)),
                  pl.BlockSpec((1, d), lambda i: (0, 0)),
                  pl.BlockSpec(wp.shape, lambda i: (0, 0)),
                  pl.BlockSpec(wr.shape, lambda i: (0, 0))],
        out_specs=pl.BlockSpec((tm, d), lambda i: (i, 0)),
        out_shape=jax.ShapeDtypeStruct((t, d), F32),
        compiler_params=_cparams(("arbitrary",)),
        name="outproj",
    )(x2, y_pool, y_rwkv, mod3, gpost, wp, wr)


def _pad_cols(w, n):
    return jnp.pad(w, ((0, 0), (0, n - w.shape[1])))


def _layer(x2, c, batch, seq, w_ada, b_ada, norm_pre, norm_post, f1g, f1u, f1d, w_in, mu_shift, pool_w,
           pool_scale, w0, w2, a0, a2, g2, k_k, k_a, r_k, lnx_w, lnx_b, w_out, f2g, f2u, f2d):
    d = x2.shape[1]
    pool_width = pool_scale.shape[0]
    rw = w0.shape[0]
    n_sub = norm_pre.shape[0]
    col_tile = 512

    mod3 = _ada(c, w_ada, b_ada).reshape(batch, n_sub * N_MOD, d)
    row = lambda vec: vec.reshape(1, -1)

    x2 = _ffn(x2, mod3, row(norm_pre[0]), row(norm_post[0]), f1g.astype(BF16), f1u.astype(BF16),
              f1d.astype(BF16), sub=0, seq=seq)

    n_lora = w_in.shape[1] - pool_width - 3 * rw
    assert (3 * rw) % col_tile == 0 and n_lora <= col_tile and pool_width == col_tile and rw % LANES == 0
    w_in_p = jnp.concatenate([w_in[:, pool_width:pool_width + 3 * rw],
                              _pad_cols(w_in[:, pool_width + 3 * rw:], col_tile),
                              w_in[:, :pool_width]], axis=1).astype(BF16)
    mu_p = jnp.concatenate([mu_shift, jnp.zeros((col_tile - n_lora + pool_width,), F32)])
    p_all = _inproj(x2, mod3, row(norm_pre[1]), w_in_p, row(mu_p), seq=seq, tm=1024,
                    tn=w_in_p.shape[1] // 4)
    lora_block = 3 * rw // col_tile

    y_pool = _pool(p_all, lora_block + 1, pool_w.astype(BF16), row(pool_scale), seq=seq)

    n_w, n_a = w2.shape[0], a2.shape[0]
    w2p = jnp.pad(w2, ((0, col_tile - n_w), (0, 0))).astype(BF16)
    a2p = jnp.pad(a2, ((n_w, col_tile - n_w - n_a), (0, 0))).astype(BF16)
    g2p = jnp.pad(g2, ((n_w + n_a, col_tile - n_lora), (0, 0))).astype(BF16)
    ld, a, g = _lora(p_all, lora_block, w2p, a2p, g2p, row(w0), row(a0))

    y_rwkv = _scan(p_all, ld, a, g, row(k_k), row(k_a), row(r_k), row(lnx_w), row(lnx_b),
                   batch=batch, seq=seq, r_block=0, k_block=rw // LANES, v_block=2 * rw // LANES)

    x2 = _outproj(x2, y_pool, y_rwkv, mod3, row(norm_post[1]), w_out[:pool_width].astype(BF16),
                  w_out[pool_width:].astype(BF16), seq=seq)

    x2 = _ffn(x2, mod3, row(norm_pre[2]), row(norm_post[2]), f2g.astype(BF16), f2u.astype(BF16),
              f2d.astype(BF16), sub=2, seq=seq)
    return x2


def kernel(x, c, w_ada, b_ada, norm_pre, norm_post, ffn1_w_gate, ffn1_w_up, ffn1_w_down, w_in, mu_shift,
           pool_w, pool_scale, w0, w2, a0, a2, g2, k_k, k_a, r_k, lnx_w, lnx_b, w_out, ffn2_w_gate,
           ffn2_w_up, ffn2_w_down):
    batch, seq, d = x.shape
    x2 = x.reshape(batch * seq, d)
    for l in range(w_ada.shape[0]):
        x2 = _layer(x2, c, batch, seq, w_ada[l], b_ada[l], norm_pre[l], norm_post[l], ffn1_w_gate[l],
                    ffn1_w_up[l], ffn1_w_down[l], w_in[l], mu_shift[l], pool_w[l], pool_scale[l], w0[l],
                    w2[l], a0[l], a2[l], g2[l], k_k[l], k_a[l], r_k[l], lnx_w[l], lnx_b[l], w_out[l],
                    ffn2_w_gate[l], ffn2_w_up[l], ffn2_w_down[l])
    return x2.reshape(batch, seq, d)
```

```python
import functools

import jax
import jax.numpy as jnp
from jax import lax
from jax.experimental import pallas as pl
from jax.experimental.pallas import tpu as pltpu

F32 = jnp.float32
BF16 = jnp.bfloat16

NORM_EPS = 1e-6
HEAD_SIZE = 64
LN_X_EPS = 1e-5 * HEAD_SIZE
POOL_WINDOWS = (2, 4, 8, 16)
POOL_GROUP = 128
MACARON_WEIGHT = 0.5
N_MOD = 3

LANES = 128
CHUNK = 64
CHUNK_LAG = 8
ROW_BLOCK = 16
VMEM_LIMIT = 56 * 1024 * 1024


def _cparams(sem):
    return pltpu.CompilerParams(dimension_semantics=sem, vmem_limit_bytes=VMEM_LIMIT)


def _sigmoid(z):
    return 1.0 / (1.0 + jnp.exp(-z))


def _row_blocks(n_rows, fn):
    for b in range(n_rows // ROW_BLOCK):
        fn(pl.ds(b * ROW_BLOCK, ROW_BLOCK))


def _norm_mod_into(h_ref, x_ref, gain_ref, mod_ref, sub):
    shift = mod_ref[0, N_MOD * sub:N_MOD * sub + 1, :]
    scale = mod_ref[0, N_MOD * sub + 1:N_MOD * sub + 2, :]
    mult = gain_ref[...] * (1.0 + scale)

    def block(rows):
        x = x_ref[rows, :]
        inv = lax.rsqrt(jnp.mean(x * x, axis=-1, keepdims=True) + NORM_EPS)
        h_ref[rows, :] = (x * inv * mult + shift).astype(BF16)

    _row_blocks(x_ref.shape[0], block)


def _residual_into(o_ref, x_ref, gain_ref, mod_ref, sub, weight):
    gate = mod_ref[0, N_MOD * sub + 2:N_MOD * sub + 3, :]
    mult = gain_ref[...] * (weight * (1.0 + gate))

    def block(rows):
        y = o_ref[rows, :]
        inv = lax.rsqrt(jnp.mean(y * y, axis=-1, keepdims=True) + NORM_EPS)
        o_ref[rows, :] = x_ref[rows, :] + y * inv * mult

    _row_blocks(o_ref.shape[0], block)


def _bdot(a, b):
    return jnp.dot(a.astype(BF16), b.astype(BF16), preferred_element_type=F32)


def _ada_kernel(c_ref, w_ref, b_ref, o_ref):
    c = c_ref[...]
    s = c * _sigmoid(c)
    o_ref[...] = jnp.dot(s, w_ref[...], preferred_element_type=F32,
                         precision=lax.Precision.HIGHEST) + b_ref[...]


def _ada(c, w, b, tn=1024):
    bsz, d = c.shape
    n = w.shape[1]
    return pl.pallas_call(
        _ada_kernel,
        grid=(n // tn,),
        in_specs=[pl.BlockSpec((bsz, d), lambda j: (0, 0)),
                  pl.BlockSpec((d, tn), lambda j: (0, j)),
                  pl.BlockSpec((1, tn), lambda j: (0, j))],
        out_specs=pl.BlockSpec((bsz, tn), lambda j: (0, j)),
        out_shape=jax.ShapeDtypeStruct((bsz, n), F32),
        compiler_params=_cparams(("arbitrary",)),
        name="ada",
    )(c, w, b.reshape(1, n))


def _ffn_kernel(x_ref, mod_ref, gpre_ref, gpost_ref, wg_ref, wu_ref, wd_ref, o_ref, h_ref, *, sub):
    j = pl.program_id(1)

    @pl.when(j == 0)
    def _():
        _norm_mod_into(h_ref, x_ref, gpre_ref, mod_ref, sub)
        o_ref[...] = jnp.zeros_like(o_ref)

    h = h_ref[...]
    g = jnp.dot(h, wg_ref[...], preferred_element_type=F32)
    u = jnp.dot(h, wu_ref[...], preferred_element_type=F32)
    act = (g * _sigmoid(g) * u).astype(BF16)
    o_ref[...] += jnp.dot(act, wd_ref[...], preferred_element_type=F32)

    @pl.when(j == pl.num_programs(1) - 1)
    def _():
        _residual_into(o_ref, x_ref, gpost_ref, mod_ref, sub, MACARON_WEIGHT)


def _ffn(x2, mod3, gpre, gpost, wg, wu, wd, *, sub, seq, tm=512, tf=512):
    t, d = x2.shape
    f = wg.shape[1]
    tps = seq // tm
    return pl.pallas_call(
        functools.partial(_ffn_kernel, sub=sub),
        grid=(t // tm, f // tf),
        in_specs=[pl.BlockSpec((tm, d), lambda i, j: (i, 0)),
                  pl.BlockSpec((1,) + mod3.shape[1:], lambda i, j: (i // tps, 0, 0)),
                  pl.BlockSpec((1, d), lambda i, j: (0, 0)),
                  pl.BlockSpec((1, d), lambda i, j: (0, 0)),
                  pl.BlockSpec((d, tf), lambda i, j: (0, j)),
                  pl.BlockSpec((d, tf), lambda i, j: (0, j)),
                  pl.BlockSpec((tf, d), lambda i, j: (j, 0))],
        out_specs=pl.BlockSpec((tm, d), lambda i, j: (i, 0)),
        out_shape=jax.ShapeDtypeStruct((t, d), F32),
        scratch_shapes=[pltpu.VMEM((tm, d), BF16)],
        compiler_params=_cparams(("arbitrary", "arbitrary")),
        name=f"ffn{sub}",
    )(x2, mod3, gpre, gpost, wg, wu, wd)


def _inproj_kernel(x_ref, mod_ref, gpre_ref, w_ref, mu_ref, o_ref, h_ref, carry_ref, *, tps):
    i = pl.program_id(0)
    j = pl.program_id(1)

    @pl.when(j == 0)
    def _():
        _norm_mod_into(h_ref, x_ref, gpre_ref, mod_ref, 1)

    res = jnp.dot(h_ref[...], w_ref[...], preferred_element_type=F32)
    tm, tn = res.shape
    carried = carry_ref[j][0:1, :]
    first = jnp.where(i % tps == 0, jnp.zeros_like(carried), carried)
    row = lax.broadcasted_iota(jnp.int32, (tm, 1), 0)
    prev = jnp.where(row == 0, first, pltpu.roll(res, 1, 0))
    carry_ref[j] = jnp.broadcast_to(res[tm - 1:tm, :], (8, tn))
    o_ref[...] = res + mu_ref[...] * (prev - res)


def _inproj(x2, mod3, gpre, w, mu, *, seq, tm=512, tn=512):
    t, d = x2.shape
    n = w.shape[1]
    tps = seq // tm
    return pl.pallas_call(
        functools.partial(_inproj_kernel, tps=tps),
        grid=(t // tm, n // tn),
        in_specs=[pl.BlockSpec((tm, d), lambda i, j: (i, 0)),
                  pl.BlockSpec((1,) + mod3.shape[1:], lambda i, j: (i // tps, 0, 0)),
                  pl.BlockSpec((1, d), lambda i, j: (0, 0)),
                  pl.BlockSpec((d, tn), lambda i, j: (0, j)),
                  pl.BlockSpec((1, tn), lambda i, j: (0, j))],
        out_specs=pl.BlockSpec((tm, tn), lambda i, j: (i, j)),
        out_shape=jax.ShapeDtypeStruct((t, n), F32),
        scratch_shapes=[pltpu.VMEM((tm, d), BF16), pltpu.VMEM((n // tn, 8, tn), F32)],
        compiler_params=_cparams(("arbitrary", "arbitrary")),
        name="inproj",
    )(x2, mod3, gpre, w, mu)


def _pool_kernel(u_ref, w_ref, s_ref, o_ref, ext_ref, *, tps):
    i = pl.program_id(0)
    tm = u_ref.shape[0]
    pad = max(POOL_WINDOWS)

    @pl.when(i % tps == 0)
    def _():
        ext_ref[0:pad, :] = jnp.zeros((pad, ext_ref.shape[1]), F32)

    ext_ref[pad:pad + tm, :] = u_ref[...]
    t_in_seq = (i % tps) * tm + lax.broadcasted_iota(jnp.int32, (tm, 1), 0)
    tpos = (t_in_seq + 1).astype(F32)
    for gi, win in enumerate(POOL_WINDOWS):
        lo = gi * POOL_GROUP
        acc = ext_ref[pad:pad + tm, lo:lo + POOL_GROUP]
        u_g = acc
        for dlt in range(1, win):
            acc = acc + ext_ref[pad - dlt:pad - dlt + tm, lo:lo + POOL_GROUP]
        pooled = acc / jnp.minimum(tpos, float(win)) - u_g
        mixed = _bdot(pooled, w_ref[gi])
        o_ref[:, lo:lo + POOL_GROUP] = mixed * s_ref[:, lo:lo + POOL_GROUP]
    ext_ref[0:pad, :] = ext_ref[tm:tm + pad, :]


def _pool(p_all, col_block, pool_w, pool_scale, *, seq, tm=512):
    t = p_all.shape[0]
    pw = pool_scale.shape[1]
    tps = seq // tm
    return pl.pallas_call(
        functools.partial(_pool_kernel, tps=tps),
        grid=(t // tm,),
        in_specs=[pl.BlockSpec((tm, pw), lambda i: (i, col_block)),
                  pl.BlockSpec(pool_w.shape, lambda i: (0, 0, 0)),
                  pl.BlockSpec((1, pw), lambda i: (0, 0))],
        out_specs=pl.BlockSpec((tm, pw), lambda i: (i, 0)),
        out_shape=jax.ShapeDtypeStruct((t, pw), F32),
        scratch_shapes=[pltpu.VMEM((tm + max(POOL_WINDOWS), pw), F32)],
        compiler_params=_cparams(("arbitrary",)),
        name="pool",
    )(p_all, pool_w, pool_scale)


def _lora_kernel(p_ref, w2_ref, a2_ref, g2_ref, w0_ref, a0_ref, ld_ref, a_ref, g_ref):
    p = p_ref[...]
    z = w0_ref[...] + _bdot(jnp.tanh(p), w2_ref[...])
    w_log = -(jnp.maximum(-z, 0.0) + jnp.log(1.0 + jnp.exp(-jnp.abs(z)))) - 0.5
    ld_ref[...] = -jnp.exp(w_log)
    a_ref[...] = _sigmoid(a0_ref[...] + _bdot(p, a2_ref[...]))
    g_ref[...] = _bdot(_sigmoid(p), g2_ref[...])


def _lora(p_all, col_block, w2p, a2p, g2p, w0, a0, *, tm=512):
    t = p_all.shape[0]
    kw, r = w2p.shape
    full = lambda shape: pl.BlockSpec(shape, lambda i: (0, 0))
    out = jax.ShapeDtypeStruct((t, r), F32)
    return pl.pallas_call(
        _lora_kernel,
        grid=(t // tm,),
        in_specs=[pl.BlockSpec((tm, kw), lambda i: (i, col_block)),
                  full((kw, r)), full((kw, r)), full((kw, r)), full((1, r)), full((1, r))],
        out_specs=[pl.BlockSpec((tm, r), lambda i: (i, 0))] * 3,
        out_shape=[out, out, out],
        compiler_params=_cparams(("arbitrary",)),
        name="lora",
    )(p_all, w2p, a2p, g2p, w0, a0)


def _scan_kernel(r_ref, k_ref, v_ref, ld_ref, a_ref, g_ref, kkw_ref, ka_ref, rk_ref, lw_ref, lb_ref,
                 o_ref, h_ref):
    C = CHUNK
    P = LANES
    N = HEAD_SIZE

    @pl.when(pl.program_id(2) == 0)
    def _():
        h_ref[...] = jnp.zeros_like(h_ref)

    lane = lax.broadcasted_iota(jnp.int32, (1, P), 1)
    in_a = lane < N
    ri = lax.broadcasted_iota(jnp.int32, (2 * C, 2 * C), 0)
    ci = lax.broadcasted_iota(jnp.int32, (2 * C, 2 * C), 1)

    def same(b):
        sh = b.bit_length() - 1
        return (ri >> sh) == (ci >> sh)

    strict = (ri > ci) & same(C)
    incl = (ri >= ci) & same(C)
    eye = ri == ci
    blk16 = strict & same(16)
    blk32 = strict & same(32) & jnp.logical_not(same(16))
    blk64 = strict & jnp.logical_not(same(32))
    eye_f = eye.astype(F32)
    tri3 = (lax.broadcasted_iota(jnp.int32, (C, 3 * C), 0)
            >= (lax.broadcasted_iota(jnp.int32, (C, 3 * C), 1) & (C - 1))).astype(BF16)

    def stack(x):
        return jnp.concatenate([jnp.where(in_a, x, 0.0), jnp.where(in_a, 0.0, x)], axis=0)

    def headsum(x):
        first = jnp.sum(jnp.where(in_a, x, 0.0), axis=-1, keepdims=True)
        second = jnp.sum(jnp.where(in_a, 0.0, x), axis=-1, keepdims=True)
        return jnp.where(in_a, first, second)

    def pair_chunk(c, q):
        sl = pl.ds(c * C, C)
        ln = pl.ds(q * P, P)
        kkw = kkw_ref[:, ln]
        ka = ka_ref[:, ln]
        rk = rk_ref[:, ln]
        lw = lw_ref[:, ln]
        lb = lb_ref[:, ln]
        r = r_ref[sl, ln]
        k = k_ref[sl, ln]
        v = v_ref[sl, ln]
        ld = ld_ref[sl, ln]
        a = a_ref[sl, ln]

        kkraw = k * kkw
        kmod = k * (1.0 + (a - 1.0) * ka)
        sums = headsum(jnp.concatenate([kkraw * kkraw, r * kmod * rk], axis=0))
        ld_hi = ld.astype(BF16)
        rem = ld - ld_hi.astype(F32)
        ld_mid = rem.astype(BF16)
        ld_lo = (rem - ld_mid.astype(F32)).astype(BF16)
        cum = jnp.dot(tri3, jnp.concatenate([ld_hi, ld_mid, ld_lo], axis=0), preferred_element_type=F32)
        yield
        kk = kkraw / jnp.maximum(jnp.sqrt(sums[:C]), 1e-12)
        bonus = sums[C:]
        cum_last = cum[C - 1:C, :]

        inv_g = jnp.exp(-cum)
        to_end = jnp.exp(cum_last - cum)
        kb = kk * a
        rt = r * jnp.exp(cum)
        at = -kk * jnp.exp(cum - ld)
        bt = kb * inv_g
        kt = kmod * inv_g
        g_end = jnp.exp(cum_last)

        s_v = stack(v).astype(BF16)
        lhs = jnp.concatenate([stack(at), stack(rt)], axis=0).astype(BF16)
        rhs = jnp.concatenate([stack(bt), stack(kt)], axis=0).astype(BF16)
        scores = lax.dot_general(lhs, rhs, (((1,), (1,)), ((), ())), preferred_element_type=F32)
        s_at = lhs[:2 * C]
        s_rt = lhs[2 * C:]
        yield
        s_ab = scores[:2 * C, :2 * C]
        s_ak = jnp.where(strict, scores[:2 * C, 2 * C:], 0.0)
        s_rb = jnp.where(incl, scores[2 * C:, :2 * C], 0.0)
        s_rk = jnp.where(incl, scores[2 * C:, 2 * C:], 0.0)

        x1 = jnp.where(blk16, s_ab, 0.0)
        x2 = _bdot(x1, x1)
        yield
        x4 = _bdot(x2, x2)
        tinv = eye_f + x1
        tinv = tinv + _bdot(tinv, x2)
        yield
        x8 = _bdot(x4, x4)
        tinv = tinv + _bdot(tinv, x4)
        yield
        tinv = tinv + _bdot(tinv, x8)
        yield
        for blk in (blk32, blk64):
            off = jnp.where(blk, s_ab, 0.0)
            part = _bdot(tinv, off)
            yield
            tinv = tinv + _bdot(part, tinv)
            yield

        h = h_ref[q]
        h_b = h.astype(BF16)
        rhs_sa = jnp.dot(jnp.concatenate([s_at, s_ak.astype(BF16)], axis=1),
                         jnp.concatenate([h_b, s_v], axis=0), preferred_element_type=F32)
        yield
        s_sa = _bdot(tinv, rhs_sa)
        yield
        sa_v = jnp.concatenate([s_sa.astype(BF16), s_v], axis=0)
        y_s = jnp.dot(jnp.concatenate([s_rt, s_rb.astype(BF16), s_rk.astype(BF16)], axis=1),
                      jnp.concatenate([h_b, sa_v], axis=0), preferred_element_type=F32)
        upd_l = jnp.concatenate([stack(kb * to_end), stack(kmod * to_end)], axis=0).astype(BF16)
        upd = lax.dot_general(upd_l, sa_v, (((0,), (0,)), ((), ())), preferred_element_type=F32)
        yield
        y = y_s[:C] + y_s[C:]
        g_col = jnp.sum(jnp.where(eye, g_end, 0.0), axis=1, keepdims=True)
        h_ref[q] = h * g_col + upd

        mean = headsum(y) * (1.0 / N)
        yield
        dev = y - mean
        var = headsum(dev * dev) * (1.0 / N)
        yield
        yn = dev * lax.rsqrt(var + LN_X_EPS) * lw + lb
        o_ref[sl, ln] = (yn + bonus * v) * g_ref[sl, ln]

    n_chunks = r_ref.shape[0] // C
    chains = [[pair_chunk(c, q) for q in range(r_ref.shape[1] // P)] for c in range(n_chunks)]
    sweep = 0
    while any(chains):
        for c in range(min(n_chunks, sweep // CHUNK_LAG + 1)):
            advanced = []
            for gen in chains[c]:
                try:
                    next(gen)
                    advanced.append(gen)
                except StopIteration:
                    pass
            chains[c] = advanced
        sweep += 1


def _scan(p_all, ld, a, g, kkw, ka, rk, lw, lb, *, batch, seq, r_block, k_block, v_block, tb=4 * CHUNK, pairs=12):
    t, rw = ld.shape
    width = pairs * LANES
    tpb = seq // tb
    tok = lambda blk0: pl.BlockSpec((tb, width), lambda b, p, s: (b * tpb + s, blk0 // pairs + p))
    par = pl.BlockSpec((1, width), lambda b, p, s: (0, p))
    return pl.pallas_call(
        _scan_kernel,
        grid=(batch, rw // width, tpb),
        in_specs=[tok(r_block), tok(k_block), tok(v_block), tok(0), tok(0), tok(0),
                  par, par, par, par, par],
        out_specs=tok(0),
        out_shape=jax.ShapeDtypeStruct((t, rw), F32),
        scratch_shapes=[pltpu.VMEM((pairs, LANES, LANES), F32)],
        compiler_params=_cparams(("arbitrary", "arbitrary", "arbitrary")),
        name="scan",
    )(p_all, p_all, p_all, ld, a, g, kkw, ka, rk, lw, lb)


def _outproj_kernel(x_ref, yp_ref, yr_ref, mod_ref, gpost_ref, wp_ref, wr_ref, o_ref):
    o_ref[...] = (jnp.dot(yp_ref[...].astype(BF16), wp_ref[...], preferred_element_type=F32)
                  + jnp.dot(yr_ref[...].astype(BF16), wr_ref[...], preferred_element_type=F32))
    _residual_into(o_ref, x_ref, gpost_ref, mod_ref, 1, 1.0)


def _outproj(x2, y_pool, y_rwkv, mod3, gpost, wp, wr, *, seq, tm=512):
    t, d = x2.shape
    tps = seq // tm
    return pl.pallas_call(
        _outproj_kernel,
        grid=(t // tm,),
        in_specs=[pl.BlockSpec((tm, d), lambda i: (i, 0)),
                  pl.BlockSpec((tm, y_pool.shape[1]), lambda i: (i, 0)),
                  pl.BlockSpec((tm, y_rwkv.shape[1]), lambda i: (i, 0)),
                  pl.BlockSpec((1,) + mod3.shape[1:], lambda i: (i // tps, 0, 0)),
                  pl.BlockSpec((1, d), lambda i: (0, 0)),
                  pl.BlockSpec(wp.shape, lambda i: (0, 0)),
                  pl.BlockSpec(wr.shape, lambda i: (0, 0))],
        out_specs=pl.BlockSpec((tm, d), lambda i: (i, 0)),
        out_shape=jax.ShapeDtypeStruct((t, d), F32),
        compiler_params=_cparams(("arbitrary",)),
        name="outproj",
    )(x2, y_pool, y_rwkv, mod3, gpost, wp, wr)


def _pad_cols(w, n):
    return jnp.pad(w, ((0, 0), (0, n - w.shape[1])))


def _layer(x2, c, batch, seq, w_ada, b_ada, norm_pre, norm_post, f1g, f1u, f1d, w_in, mu_shift, pool_w,
           pool_scale, w0, w2, a0, a2, g2, k_k, k_a, r_k, lnx_w, lnx_b, w_out, f2g, f2u, f2d):
    d = x2.shape[1]
    pool_width = pool_scale.shape[0]
    rw = w0.shape[0]
    n_sub = norm_pre.shape[0]
    col_tile = 512

    mod3 = _ada(c, w_ada, b_ada).reshape(batch, n_sub * N_MOD, d)
    row = lambda vec: vec.reshape(1, -1)

    x2 = _ffn(x2, mod3, row(norm_pre[0]), row(norm_post[0]), f1g.astype(BF16), f1u.astype(BF16),
              f1d.astype(BF16), sub=0, seq=seq)

    n_lora = w_in.shape[1] - pool_width - 3 * rw
    assert (3 * rw) % col_tile == 0 and n_lora <= col_tile and pool_width == col_tile and rw % LANES == 0
    w_in_p = jnp.concatenate([w_in[:, pool_width:pool_width + 3 * rw],
                              _pad_cols(w_in[:, pool_width + 3 * rw:], col_tile),
                              w_in[:, :pool_width]], axis=1).astype(BF16)
    mu_p = jnp.concatenate([mu_shift, jnp.zeros((col_tile - n_lora + pool_width,), F32)])
    p_all = _inproj(x2, mod3, row(norm_pre[1]), w_in_p, row(mu_p), seq=seq, tm=1024,
                    tn=w_in_p.shape[1] // 4)
    lora_block = 3 * rw // col_tile

    y_pool = _pool(p_all, lora_block + 1, pool_w.astype(BF16), row(pool_scale), seq=seq)

    n_w, n_a = w2.shape[0], a2.shape[0]
    w2p = jnp.pad(w2, ((0, col_tile - n_w), (0, 0))).astype(BF16)
    a2p = jnp.pad(a2, ((n_w, col_tile - n_w - n_a), (0, 0))).astype(BF16)
    g2p = jnp.pad(g2, ((n_w + n_a, col_tile - n_lora), (0, 0))).astype(BF16)
    ld, a, g = _lora(p_all, lora_block, w2p, a2p, g2p, row(w0), row(a0))

    y_rwkv = _scan(p_all, ld, a, g, row(k_k), row(k_a), row(r_k), row(lnx_w), row(lnx_b),
                   batch=batch, seq=seq, r_block=0, k_block=rw // LANES, v_block=2 * rw // LANES)

    x2 = _outproj(x2, y_pool, y_rwkv, mod3, row(norm_post[1]), w_out[:pool_width].astype(BF16),
                  w_out[pool_width:].astype(BF16), seq=seq)

    x2 = _ffn(x2, mod3, row(norm_pre[2]), row(norm_post[2]), f2g.astype(BF16), f2u.astype(BF16),
              f2d.astype(BF16), sub=2, seq=seq)
    return x2


def kernel(x, c, w_ada, b_ada, norm_pre, norm_post, ffn1_w_gate, ffn1_w_up, ffn1_w_down, w_in, mu_shift,
           pool_w, pool_scale, w0, w2, a0, a2, g2, k_k, k_a, r_k, lnx_w, lnx_b, w_out, ffn2_w_gate,
           ffn2_w_up, ffn2_w_down):
    batch, seq, d = x.shape
    x2 = x.reshape(batch * seq, d)
    for l in range(w_ada.shape[0]):
        x2 = _layer(x2, c, batch, seq, w_ada[l], b_ada[l], norm_pre[l], norm_post[l], ffn1_w_gate[l],
                    ffn1_w_up[l], ffn1_w_down[l], w_in[l], mu_shift[l], pool_w[l], pool_scale[l], w0[l],
                    w2[l], a0[l], a2[l], g2[l], k_k[l], k_a[l], r_k[l], lnx_w[l], lnx_b[l], w_out[l],
                    ffn2_w_gate[l], ffn2_w_up[l], ffn2_w_down[l])
    return x2.reshape(batch, seq, d)
```

```python
import functools

import jax
import jax.numpy as jnp
from jax import lax
from jax.experimental import pallas as pl
from jax.experimental.pallas import tpu as pltpu

F32 = jnp.float32
BF16 = jnp.bfloat16

NORM_EPS = 1e-6
HEAD_SIZE = 64
LN_X_EPS = 1e-5 * HEAD_SIZE
POOL_WINDOWS = (2, 4, 8, 16)
POOL_GROUP = 128
MACARON_WEIGHT = 0.5
N_MOD = 3

LANES = 128
CHUNK = 64
CHUNK_LAG = 8
ROW_BLOCK = 16
VMEM_LIMIT_MIB = 56


def _cparams(sem, vmem_mib=VMEM_LIMIT_MIB):
    return pltpu.CompilerParams(dimension_semantics=sem, vmem_limit_bytes=vmem_mib * 1024 * 1024)


def _sigmoid(z):
    return 1.0 / (1.0 + jnp.exp(-z))


def _row_blocks(n_rows, fn):
    for b in range(n_rows // ROW_BLOCK):
        fn(pl.ds(b * ROW_BLOCK, ROW_BLOCK))


def _norm_mod_into(h_ref, x_ref, gain_ref, mod_ref, sub):
    shift = mod_ref[0, N_MOD * sub:N_MOD * sub + 1, :]
    scale = mod_ref[0, N_MOD * sub + 1:N_MOD * sub + 2, :]
    mult = gain_ref[...] * (1.0 + scale)

    def block(rows):
        x = x_ref[rows, :]
        inv = lax.rsqrt(jnp.mean(x * x, axis=-1, keepdims=True) + NORM_EPS)
        h_ref[rows, :] = (x * inv * mult + shift).astype(BF16)

    _row_blocks(x_ref.shape[0], block)


def _residual_into(o_ref, x_ref, gain_ref, mod_ref, sub, weight):
    gate = mod_ref[0, N_MOD * sub + 2:N_MOD * sub + 3, :]
    mult = gain_ref[...] * (weight * (1.0 + gate))

    def block(rows):
        y = o_ref[rows, :]
        inv = lax.rsqrt(jnp.mean(y * y, axis=-1, keepdims=True) + NORM_EPS)
        o_ref[rows, :] = x_ref[rows, :] + y * inv * mult

    _row_blocks(o_ref.shape[0], block)


def _bdot(a, b):
    return jnp.dot(a.astype(BF16), b.astype(BF16), preferred_element_type=F32)


def _ada_kernel(c_ref, w_ref, b_ref, o_ref):
    c = c_ref[...]
    s = c * _sigmoid(c)
    o_ref[...] = jnp.dot(s, w_ref[...], preferred_element_type=F32,
                         precision=lax.Precision.HIGHEST) + b_ref[...]


def _ada(c, w, b, tn=1024):
    bsz, d = c.shape
    n = w.shape[1]
    return pl.pallas_call(
        _ada_kernel,
        grid=(n // tn,),
        in_specs=[pl.BlockSpec((bsz, d), lambda j: (0, 0)),
                  pl.BlockSpec((d, tn), lambda j: (0, j)),
                  pl.BlockSpec((1, tn), lambda j: (0, j))],
        out_specs=pl.BlockSpec((bsz, tn), lambda j: (0, j)),
        out_shape=jax.ShapeDtypeStruct((bsz, n), F32),
        compiler_params=_cparams(("arbitrary",)),
        name="ada",
    )(c, w, b.reshape(1, n))


def _ffn_kernel(x_ref, mod_ref, gpre_ref, gpost_ref, wg_ref, wu_ref, wd_ref, o_ref, h_ref, *, sub):
    j = pl.program_id(1)

    @pl.when(j == 0)
    def _():
        _norm_mod_into(h_ref, x_ref, gpre_ref, mod_ref, sub)
        o_ref[...] = jnp.zeros_like(o_ref)

    h = h_ref[...]
    g = jnp.dot(h, wg_ref[...], preferred_element_type=F32)
    u = jnp.dot(h, wu_ref[...], preferred_element_type=F32)
    act = (g * _sigmoid(g) * u).astype(BF16)
    o_ref[...] += jnp.dot(act, wd_ref[...], preferred_element_type=F32)

    @pl.when(j == pl.num_programs(1) - 1)
    def _():
        _residual_into(o_ref, x_ref, gpost_ref, mod_ref, sub, MACARON_WEIGHT)


def _ffn(x2, mod3, gpre, gpost, wg, wu, wd, *, sub, seq, tm=1024, tf=512):
    t, d = x2.shape
    f = wg.shape[1]
    tps = seq // tm
    spill_mib = 5
    vmem_mib = (2 * 2 * tm * d * 4 + tm * d * 2 + 3 * 2 * d * tf * 2 + tm * tf * (4 + 4 + 2)) // 2**20 + spill_mib
    return pl.pallas_call(
        functools.partial(_ffn_kernel, sub=sub),
        grid=(t // tm, f // tf),
        in_specs=[pl.BlockSpec((tm, d), lambda i, j: (i, 0)),
                  pl.BlockSpec((1,) + mod3.shape[1:], lambda i, j: (i // tps, 0, 0)),
                  pl.BlockSpec((1, d), lambda i, j: (0, 0)),
                  pl.BlockSpec((1, d), lambda i, j: (0, 0)),
                  pl.BlockSpec((d, tf), lambda i, j: (0, j)),
                  pl.BlockSpec((d, tf), lambda i, j: (0, j)),
                  pl.BlockSpec((tf, d), lambda i, j: (j, 0))],
        out_specs=pl.BlockSpec((tm, d), lambda i, j: (i, 0)),
        out_shape=jax.ShapeDtypeStruct((t, d), F32),
        scratch_shapes=[pltpu.VMEM((tm, d), BF16)],
        compiler_params=_cparams(("arbitrary", "arbitrary"), vmem_mib),
        name=f"ffn{sub}",
    )(x2, mod3, gpre, gpost, wg, wu, wd)


def _inproj_kernel(x_ref, mod_ref, gpre_ref, w_ref, mu_ref, o_ref, h_ref, carry_ref, *, tps):
    i = pl.program_id(0)
    j = pl.program_id(1)

    @pl.when(j == 0)
    def _():
        _norm_mod_into(h_ref, x_ref, gpre_ref, mod_ref, 1)

    res = jnp.dot(h_ref[...], w_ref[...], preferred_element_type=F32)
    tm, tn = res.shape
    carried = carry_ref[j][0:1, :]
    first = jnp.where(i % tps == 0, jnp.zeros_like(carried), carried)
    row = lax.broadcasted_iota(jnp.int32, (tm, 1), 0)
    prev = jnp.where(row == 0, first, pltpu.roll(res, 1, 0))
    carry_ref[j] = jnp.broadcast_to(res[tm - 1:tm, :], (8, tn))
    o_ref[...] = res + mu_ref[...] * (prev - res)


def _inproj(x2, mod3, gpre, w, mu, *, seq, tm=512, tn=512):
    t, d = x2.shape
    n = w.shape[1]
    tps = seq // tm
    return pl.pallas_call(
        functools.partial(_inproj_kernel, tps=tps),
        grid=(t // tm, n // tn),
        in_specs=[pl.BlockSpec((tm, d), lambda i, j: (i, 0)),
                  pl.BlockSpec((1,) + mod3.shape[1:], lambda i, j: (i // tps, 0, 0)),
                  pl.BlockSpec((1, d), lambda i, j: (0, 0)),
                  pl.BlockSpec((d, tn), lambda i, j: (0, j)),
                  pl.BlockSpec((1, tn), lambda i, j: (0, j))],
        out_specs=pl.BlockSpec((tm, tn), lambda i, j: (i, j)),
        out_shape=jax.ShapeDtypeStruct((t, n), F32),
        scratch_shapes=[pltpu.VMEM((tm, d), BF16), pltpu.VMEM((n // tn, 8, tn), F32)],
        compiler_params=_cparams(("arbitrary", "arbitrary")),
        name="inproj",
    )(x2, mod3, gpre, w, mu)


def _pool_kernel(u_ref, w_ref, s_ref, o_ref, ext_ref, *, tps):
    i = pl.program_id(0)
    tm = u_ref.shape[0]
    pad = max(POOL_WINDOWS)

    @pl.when(i % tps == 0)
    def _():
        ext_ref[0:pad, :] = jnp.zeros((pad, ext_ref.shape[1]), F32)

    ext_ref[pad:pad + tm, :] = u_ref[...]
    t_in_seq = (i % tps) * tm + lax.broadcasted_iota(jnp.int32, (tm, 1), 0)
    tpos = (t_in_seq + 1).astype(F32)
    for gi, win in enumerate(POOL_WINDOWS):
        lo = gi * POOL_GROUP
        acc = ext_ref[pad:pad + tm, lo:lo + POOL_GROUP]
        u_g = acc
        for dlt in range(1, win):
            acc = acc + ext_ref[pad - dlt:pad - dlt + tm, lo:lo + POOL_GROUP]
        pooled = acc / jnp.minimum(tpos, float(win)) - u_g
        mixed = _bdot(pooled, w_ref[gi])
        o_ref[:, lo:lo + POOL_GROUP] = mixed * s_ref[:, lo:lo + POOL_GROUP]
    ext_ref[0:pad, :] = ext_ref[tm:tm + pad, :]


def _pool(p_all, col_block, pool_w, pool_scale, *, seq, tm=512):
    t = p_all.shape[0]
    pw = pool_scale.shape[1]
    tps = seq // tm
    return pl.pallas_call(
        functools.partial(_pool_kernel, tps=tps),
        grid=(t // tm,),
        in_specs=[pl.BlockSpec((tm, pw), lambda i: (i, col_block)),
                  pl.BlockSpec(pool_w.shape, lambda i: (0, 0, 0)),
                  pl.BlockSpec((1, pw), lambda i: (0, 0))],
        out_specs=pl.BlockSpec((tm, pw), lambda i: (i, 0)),
        out_shape=jax.ShapeDtypeStruct((t, pw), F32),
        scratch_shapes=[pltpu.VMEM((tm + max(POOL_WINDOWS), pw), F32)],
        compiler_params=_cparams(("arbitrary",)),
        name="pool",
    )(p_all, pool_w, pool_scale)


def _lora_kernel(p_ref, w2_ref, a2_ref, g2_ref, w0_ref, a0_ref, ld_ref, a_ref, g_ref):
    p = p_ref[...]
    z = w0_ref[...] + _bdot(jnp.tanh(p), w2_ref[...])
    w_log = -(jnp.maximum(-z, 0.0) + jnp.log(1.0 + jnp.exp(-jnp.abs(z)))) - 0.5
    ld_ref[...] = -jnp.exp(w_log)
    a_ref[...] = _sigmoid(a0_ref[...] + _bdot(p, a2_ref[...]))
    g_ref[...] = _bdot(_sigmoid(p), g2_ref[...])


def _lora(p_all, col_block, w2p, a2p, g2p, w0, a0, *, tm=512):
    t = p_all.shape[0]
    kw, r = w2p.shape
    full = lambda shape: pl.BlockSpec(shape, lambda i: (0, 0))
    out = jax.ShapeDtypeStruct((t, r), F32)
    return pl.pallas_call(
        _lora_kernel,
        grid=(t // tm,),
        in_specs=[pl.BlockSpec((tm, kw), lambda i: (i, col_block)),
                  full((kw, r)), full((kw, r)), full((kw, r)), full((1, r)), full((1, r))],
        out_specs=[pl.BlockSpec((tm, r), lambda i: (i, 0))] * 3,
        out_shape=[out, out, out],
        compiler_params=_cparams(("arbitrary",)),
        name="lora",
    )(p_all, w2p, a2p, g2p, w0, a0)


def _scan_kernel(r_ref, k_ref, v_ref, ld_ref, a_ref, g_ref, kkw_ref, ka_ref, rk_ref, lw_ref, lb_ref,
                 o_ref, h_ref):
    C = CHUNK
    P = LANES
    N = HEAD_SIZE

    @pl.when(pl.program_id(2) == 0)
    def _():
        h_ref[...] = jnp.zeros_like(h_ref)

    lane = lax.broadcasted_iota(jnp.int32, (1, P), 1)
    in_a = lane < N
    ri = lax.broadcasted_iota(jnp.int32, (2 * C, 2 * C), 0)
    ci = lax.broadcasted_iota(jnp.int32, (2 * C, 2 * C), 1)

    def same(b):
        sh = b.bit_length() - 1
        return (ri >> sh) == (ci >> sh)

    strict = (ri > ci) & same(C)
    incl = (ri >= ci) & same(C)
    eye = ri == ci
    blk16 = strict & same(16)
    blk32 = strict & same(32) & jnp.logical_not(same(16))
    blk64 = strict & jnp.logical_not(same(32))
    eye_f = eye.astype(F32)
    tri3 = (lax.broadcasted_iota(jnp.int32, (C, 3 * C), 0)
            >= (lax.broadcasted_iota(jnp.int32, (C, 3 * C), 1) & (C - 1))).astype(BF16)

    def stack(x):
        return jnp.concatenate([jnp.where(in_a, x, 0.0), jnp.where(in_a, 0.0, x)], axis=0)

    def headsum(x):
        first = jnp.sum(jnp.where(in_a, x, 0.0), axis=-1, keepdims=True)
        second = jnp.sum(jnp.where(in_a, 0.0, x), axis=-1, keepdims=True)
        return jnp.where(in_a, first, second)

    def pair_chunk(c, q):
        sl = pl.ds(c * C, C)
        ln = pl.ds(q * P, P)
        kkw = kkw_ref[:, ln]
        ka = ka_ref[:, ln]
        rk = rk_ref[:, ln]
        lw = lw_ref[:, ln]
        lb = lb_ref[:, ln]
        r = r_ref[sl, ln]
        k = k_ref[sl, ln]
        v = v_ref[sl, ln]
        ld = ld_ref[sl, ln]
        a = a_ref[sl, ln]

        kkraw = k * kkw
        kmod = k * (1.0 + (a - 1.0) * ka)
        sums = headsum(jnp.concatenate([kkraw * kkraw, r * kmod * rk], axis=0))
        ld_hi = ld.astype(BF16)
        rem = ld - ld_hi.astype(F32)
        ld_mid = rem.astype(BF16)
        ld_lo = (rem - ld_mid.astype(F32)).astype(BF16)
        cum = jnp.dot(tri3, jnp.concatenate([ld_hi, ld_mid, ld_lo], axis=0), preferred_element_type=F32)
        yield
        kk = kkraw / jnp.maximum(jnp.sqrt(sums[:C]), 1e-12)
        bonus = sums[C:]
        cum_last = cum[C - 1:C, :]

        inv_g = jnp.exp(-cum)
        to_end = jnp.exp(cum_last - cum)
        kb = kk * a
        rt = r * jnp.exp(cum)
        at = -kk * jnp.exp(cum - ld)
        bt = kb * inv_g
        kt = kmod * inv_g
        g_end = jnp.exp(cum_last)

        s_v = stack(v).astype(BF16)
        lhs = jnp.concatenate([stack(at), stack(rt)], axis=0).astype(BF16)
        rhs = jnp.concatenate([stack(bt), stack(kt)], axis=0).astype(BF16)
        scores = lax.dot_general(lhs, rhs, (((1,), (1,)), ((), ())), preferred_element_type=F32)
        s_at = lhs[:2 * C]
        s_rt = lhs[2 * C:]
        yield
        s_ab = scores[:2 * C, :2 * C]
        s_ak = jnp.where(strict, scores[:2 * C, 2 * C:], 0.0)
        s_rb = jnp.where(incl, scores[2 * C:, :2 * C], 0.0)
        s_rk = jnp.where(incl, scores[2 * C:, 2 * C:], 0.0)

        x1 = jnp.where(blk16, s_ab, 0.0)
        x2 = _bdot(x1, x1)
        yield
        x4 = _bdot(x2, x2)
        tinv = eye_f + x1
        tinv = tinv + _bdot(tinv, x2)
        yield
        x8 = _bdot(x4, x4)
        tinv = tinv + _bdot(tinv, x4)
        yield
        tinv = tinv + _bdot(tinv, x8)
        yield
        for blk in (blk32, blk64):
            off = jnp.where(blk, s_ab, 0.0)
            part = _bdot(tinv, off)
            yield
            tinv = tinv + _bdot(part, tinv)
            yield

        h = h_ref[q]
        h_b = h.astype(BF16)
        rhs_sa = jnp.dot(jnp.concatenate([s_at, s_ak.astype(BF16)], axis=1),
                         jnp.concatenate([h_b, s_v], axis=0), preferred_element_type=F32)
        yield
        s_sa = _bdot(tinv, rhs_sa)
        yield
        sa_v = jnp.concatenate([s_sa.astype(BF16), s_v], axis=0)
        y_s = jnp.dot(jnp.concatenate([s_rt, s_rb.astype(BF16), s_rk.astype(BF16)], axis=1),
                      jnp.concatenate([h_b, sa_v], axis=0), preferred_element_type=F32)
        upd_l = jnp.concatenate([stack(kb * to_end), stack(kmod * to_end)], axis=0).astype(BF16)
        upd = lax.dot_general(upd_l, sa_v, (((0,), (0,)), ((), ())), preferred_element_type=F32)
        yield
        y = y_s[:C] + y_s[C:]
        g_col = jnp.sum(jnp.where(eye, g_end, 0.0), axis=1, keepdims=True)
        h_ref[q] = h * g_col + upd

        mean = headsum(y) * (1.0 / N)
        yield
        dev = y - mean
        var = headsum(dev * dev) * (1.0 / N)
        yield
        yn = dev * lax.rsqrt(var + LN_X_EPS) * lw + lb
        o_ref[sl, ln] = (yn + bonus * v) * g_ref[sl, ln]

    n_chunks = r_ref.shape[0] // C
    chains = [[pair_chunk(c, q) for q in range(r_ref.shape[1] // P)] for c in range(n_chunks)]
    sweep = 0
    while any(chains):
        for c in range(min(n_chunks, sweep // CHUNK_LAG + 1)):
            advanced = []
            for gen in chains[c]:
                try:
                    next(gen)
                    advanced.append(gen)
                except StopIteration:
                    pass
            chains[c] = advanced
        sweep += 1


def _scan(p_all, ld, a, g, kkw, ka, rk, lw, lb, *, batch, seq, r_block, k_block, v_block, tb=4 * CHUNK, pairs=12):
    t, rw = ld.shape
    width = pairs * LANES
    tpb = seq // tb
    tok = lambda blk0: pl.BlockSpec((tb, width), lambda b, p, s: (b * tpb + s, blk0 // pairs + p))
    par = pl.BlockSpec((1, width), lambda b, p, s: (0, p))
    return pl.pallas_call(
        _scan_kernel,
        grid=(batch, rw // width, tpb),
        in_specs=[tok(r_block), tok(k_block), tok(v_block), tok(0), tok(0), tok(0),
                  par, par, par, par, par],
        out_specs=tok(0),
        out_shape=jax.ShapeDtypeStruct((t, rw), F32),
        scratch_shapes=[pltpu.VMEM((pairs, LANES, LANES), F32)],
        compiler_params=_cparams(("arbitrary", "arbitrary", "arbitrary")),
        name="scan",
    )(p_all, p_all, p_all, ld, a, g, kkw, ka, rk, lw, lb)


def _outproj_kernel(x_ref, yp_ref, yr_ref, mod_ref, gpost_ref, wp_ref, wr_ref, o_ref):
    o_ref[...] = (jnp.dot(yp_ref[...].astype(BF16), wp_ref[...], preferred_element_type=F32)
                  + jnp.dot(yr_ref[...].astype(BF16), wr_ref[...], preferred_element_type=F32))
    _residual_into(o_ref, x_ref, gpost_ref, mod_ref, 1, 1.0)


def _outproj(x2, y_pool, y_rwkv, mod3, gpost, wp, wr, *, seq, tm=512):
    t, d = x2.shape
    tps = seq // tm
    return pl.pallas_call(
        _outproj_kernel,
        grid=(t // tm,),
        in_specs=[pl.BlockSpec((tm, d), lambda i: (i, 0)),
                  pl.BlockSpec((tm, y_pool.shape[1]), lambda i: (i, 0)),
                  pl.BlockSpec((tm, y_rwkv.shape[1]), lambda i: (i, 0)),
                  pl.BlockSpec((1,) + mod3.shape[1:], lambda i: (i // tps, 0, 0)),
                  pl.BlockSpec((1, d), lambda i: (0, 0)),
                  pl.BlockSpec(wp.shape, lambda i: (0, 0)),
                  pl.BlockSpec(wr.shape, lambda i: (0, 0))],
        out_specs=pl.BlockSpec((tm, d), lambda i: (i, 0)),
        out_shape=jax.ShapeDtypeStruct((t, d), F32),
        compiler_params=_cparams(("arbitrary",)),
        name="outproj",
    )(x2, y_pool, y_rwkv, mod3, gpost, wp, wr)


def _pad_cols(w, n):
    return jnp.pad(w, ((0, 0), (0, n - w.shape[1])))


def _layer(x2, c, batch, seq, w_ada, b_ada, norm_pre, norm_post, f1g, f1u, f1d, w_in, mu_shift, pool_w,
           pool_scale, w0, w2, a0, a2, g2, k_k, k_a, r_k, lnx_w, lnx_b, w_out, f2g, f2u, f2d):
    d = x2.shape[1]
    pool_width = pool_scale.shape[0]
    rw = w0.shape[0]
    n_sub = norm_pre.shape[0]
    col_tile = 512

    mod3 = _ada(c, w_ada, b_ada).reshape(batch, n_sub * N_MOD, d)
    row = lambda vec: vec.reshape(1, -1)

    x2 = _ffn(x2, mod3, row(norm_pre[0]), row(norm_post[0]), f1g.astype(BF16), f1u.astype(BF16),
              f1d.astype(BF16), sub=0, seq=seq)

    n_lora = w_in.shape[1] - pool_width - 3 * rw
    assert (3 * rw) % col_tile == 0 and n_lora <= col_tile and pool_width == col_tile and rw % LANES == 0
    w_in_p = jnp.concatenate([w_in[:, pool_width:pool_width + 3 * rw],
                              _pad_cols(w_in[:, pool_width + 3 * rw:], col_tile),
                              w_in[:, :pool_width]], axis=1).astype(BF16)
    mu_p = jnp.concatenate([mu_shift, jnp.zeros((col_tile - n_lora + pool_width,), F32)])
    p_all = _inproj(x2, mod3, row(norm_pre[1]), w_in_p, row(mu_p), seq=seq, tm=1024,
                    tn=w_in_p.shape[1] // 4)
    lora_block = 3 * rw // col_tile

    y_pool = _pool(p_all, lora_block + 1, pool_w.astype(BF16), row(pool_scale), seq=seq)

    n_w, n_a = w2.shape[0], a2.shape[0]
    w2p = jnp.pad(w2, ((0, col_tile - n_w), (0, 0))).astype(BF16)
    a2p = jnp.pad(a2, ((n_w, col_tile - n_w - n_a), (0, 0))).astype(BF16)
    g2p = jnp.pad(g2, ((n_w + n_a, col_tile - n_lora), (0, 0))).astype(BF16)
    ld, a, g = _lora(p_all, lora_block, w2p, a2p, g2p, row(w0), row(a0))

    y_rwkv = _scan(p_all, ld, a, g, row(k_k), row(k_a), row(r_k), row(lnx_w), row(lnx_b),
                   batch=batch, seq=seq, r_block=0, k_block=rw // LANES, v_block=2 * rw // LANES)

    x2 = _outproj(x2, y_pool, y_rwkv, mod3, row(norm_post[1]), w_out[:pool_width].astype(BF16),
                  w_out[pool_width:].astype(BF16), seq=seq)

    x2 = _ffn(x2, mod3, row(norm_pre[2]), row(norm_post[2]), f2g.astype(BF16), f2u.astype(BF16),
              f2d.astype(BF16), sub=2, seq=seq)
    return x2


def kernel(x, c, w_ada, b_ada, norm_pre, norm_post, ffn1_w_gate, ffn1_w_up, ffn1_w_down, w_in, mu_shift,
           pool_w, pool_scale, w0, w2, a0, a2, g2, k_k, k_a, r_k, lnx_w, lnx_b, w_out, ffn2_w_gate,
           ffn2_w_up, ffn2_w_down):
    batch, seq, d = x.shape
    x2 = x.reshape(batch * seq, d)
    for l in range(w_ada.shape[0]):
        x2 = _layer(x2, c, batch, seq, w_ada[l], b_ada[l], norm_pre[l], norm_post[l], ffn1_w_gate[l],
                    ffn1_w_up[l], ffn1_w_down[l], w_in[l], mu_shift[l], pool_w[l], pool_scale[l], w0[l],
                    w2[l], a0[l], a2[l], g2[l], k_k[l], k_a[l], r_k[l], lnx_w[l], lnx_b[l], w_out[l],
                    ffn2_w_gate[l], ffn2_w_up[l], ffn2_w_down[l])
    return x2.reshape(batch, seq, d)
```

```python
import functools

import jax
import jax.numpy as jnp
from jax import lax
from jax.experimental import pallas as pl
from jax.experimental.pallas import tpu as pltpu

F32 = jnp.float32
BF16 = jnp.bfloat16

NORM_EPS = 1e-6
HEAD_SIZE = 64
LN_X_EPS = 1e-5 * HEAD_SIZE
POOL_WINDOWS = (2, 4, 8, 16)
POOL_GROUP = 128
MACARON_WEIGHT = 0.5
N_MOD = 3

LANES = 128
CHUNK = 64
CHUNK_LAG = 8
ROW_BLOCK = 16
VMEM_LIMIT_MIB = 56


def _cparams(sem, vmem_mib=VMEM_LIMIT_MIB):
    return pltpu.CompilerParams(dimension_semantics=sem, vmem_limit_bytes=vmem_mib * 1024 * 1024)


def _sigmoid(z):
    return 1.0 / (1.0 + jnp.exp(-z))


def _row_blocks(n_rows, fn):
    for b in range(n_rows // ROW_BLOCK):
        fn(pl.ds(b * ROW_BLOCK, ROW_BLOCK))


def _norm_mod_into(h_ref, x_ref, gain_ref, mod_ref, sub):
    shift = mod_ref[0, N_MOD * sub:N_MOD * sub + 1, :]
    scale = mod_ref[0, N_MOD * sub + 1:N_MOD * sub + 2, :]
    mult = gain_ref[...] * (1.0 + scale)
    blocks = []

    def block(rows):
        x = x_ref[rows, :]
        inv = lax.rsqrt(jnp.mean(x * x, axis=-1, keepdims=True) + NORM_EPS)
        blocks.append((x * inv * mult + shift).astype(BF16))
        h_ref[rows, :] = blocks[-1]

    _row_blocks(x_ref.shape[0], block)
    return jnp.concatenate(blocks, axis=0)


def _residual_into(o_ref, x_ref, gain_ref, mod_ref, sub, weight, y_all=None):
    gate = mod_ref[0, N_MOD * sub + 2:N_MOD * sub + 3, :]
    mult = gain_ref[...] * (weight * (1.0 + gate))

    def block(rows):
        y = o_ref[rows, :] if y_all is None else y_all[rows.start:rows.start + rows.size, :]
        inv = lax.rsqrt(jnp.mean(y * y, axis=-1, keepdims=True) + NORM_EPS)
        o_ref[rows, :] = x_ref[rows, :] + y * inv * mult

    _row_blocks(o_ref.shape[0], block)


def _bdot(a, b):
    return jnp.dot(a.astype(BF16), b.astype(BF16), preferred_element_type=F32)


def _ada_kernel(c_ref, w_ref, b_ref, o_ref):
    c = c_ref[...]
    s = c * _sigmoid(c)
    o_ref[...] = jnp.dot(s, w_ref[...], preferred_element_type=F32,
                         precision=lax.Precision.HIGHEST) + b_ref[...]


def _ada(c, w, b, tn=1024):
    bsz, d = c.shape
    n = w.shape[1]
    return pl.pallas_call(
        _ada_kernel,
        grid=(n // tn,),
        in_specs=[pl.BlockSpec((bsz, d), lambda j: (0, 0)),
                  pl.BlockSpec((d, tn), lambda j: (0, j)),
                  pl.BlockSpec((1, tn), lambda j: (0, j))],
        out_specs=pl.BlockSpec((bsz, tn), lambda j: (0, j)),
        out_shape=jax.ShapeDtypeStruct((bsz, n), F32),
        compiler_params=_cparams(("arbitrary",)),
        name="ada",
    )(c, w, b.reshape(1, n))


def _ffn_kernel(x_ref, mod_ref, gpre_ref, gpost_ref, wg_ref, wu_ref, wd_ref, o_ref, h_ref, *, sub):
    j = pl.program_id(1)
    last = pl.num_programs(1) - 1

    def chunk(h):
        g = jnp.dot(h, wg_ref[...], preferred_element_type=F32)
        u = jnp.dot(h, wu_ref[...], preferred_element_type=F32)
        act = (g * _sigmoid(g) * u).astype(BF16)
        return jnp.dot(act, wd_ref[...], preferred_element_type=F32)

    @pl.when(j == 0)
    def _():
        o_ref[...] = chunk(_norm_mod_into(h_ref, x_ref, gpre_ref, mod_ref, sub))

    @pl.when((j > 0) & (j < last))
    def _():
        o_ref[...] += chunk(h_ref[...])

    @pl.when(j == last)
    def _():
        _residual_into(o_ref, x_ref, gpost_ref, mod_ref, sub, MACARON_WEIGHT,
                       y_all=o_ref[...] + chunk(h_ref[...]))


def _ffn(x2, mod3, gpre, gpost, wg, wu, wd, *, sub, seq, tm=1024, tf=512):
    t, d = x2.shape
    f = wg.shape[1]
    tps = seq // tm
    spill_mib = 5
    vmem_mib = (2 * 2 * tm * d * 4 + tm * d * 2 + 3 * 2 * d * tf * 2 + tm * tf * (4 + 4 + 2)) // 2**20 + spill_mib
    return pl.pallas_call(
        functools.partial(_ffn_kernel, sub=sub),
        grid=(t // tm, f // tf),
        in_specs=[pl.BlockSpec((tm, d), lambda i, j: (i, 0)),
                  pl.BlockSpec((1,) + mod3.shape[1:], lambda i, j: (i // tps, 0, 0)),
                  pl.BlockSpec((1, d), lambda i, j: (0, 0)),
                  pl.BlockSpec((1, d), lambda i, j: (0, 0)),
                  pl.BlockSpec((d, tf), lambda i, j: (0, j)),
                  pl.BlockSpec((d, tf), lambda i, j: (0, j)),
                  pl.BlockSpec((tf, d), lambda i, j: (j, 0))],
        out_specs=pl.BlockSpec((tm, d), lambda i, j: (i, 0)),
        out_shape=jax.ShapeDtypeStruct((t, d), F32),
        scratch_shapes=[pltpu.VMEM((tm, d), BF16)],
        compiler_params=_cparams(("arbitrary", "arbitrary"), vmem_mib),
        name=f"ffn{sub}",
    )(x2, mod3, gpre, gpost, wg, wu, wd)


def _inproj_kernel(x_ref, mod_ref, gpre_ref, w_ref, mu_ref, o_ref, h_ref, carry_ref, *, tps):
    i = pl.program_id(0)
    j = pl.program_id(1)

    def project(h):
        res = jnp.dot(h, w_ref[...], preferred_element_type=F32)
        tm, tn = res.shape
        carried = carry_ref[j][0:1, :]
        first = jnp.where(i % tps == 0, jnp.zeros_like(carried), carried)
        row = lax.broadcasted_iota(jnp.int32, (tm, 1), 0)
        prev = jnp.where(row == 0, first, pltpu.roll(res, 1, 0))
        carry_ref[j] = jnp.broadcast_to(res[tm - 1:tm, :], (8, tn))
        o_ref[...] = res + mu_ref[...] * (prev - res)

    @pl.when(j == 0)
    def _():
        project(_norm_mod_into(h_ref, x_ref, gpre_ref, mod_ref, 1))

    @pl.when(j > 0)
    def _():
        project(h_ref[...])


def _inproj(x2, mod3, gpre, w, mu, *, seq, tm=512, tn=512):
    t, d = x2.shape
    n = w.shape[1]
    tps = seq // tm
    return pl.pallas_call(
        functools.partial(_inproj_kernel, tps=tps),
        grid=(t // tm, n // tn),
        in_specs=[pl.BlockSpec((tm, d), lambda i, j: (i, 0)),
                  pl.BlockSpec((1,) + mod3.shape[1:], lambda i, j: (i // tps, 0, 0)),
                  pl.BlockSpec((1, d), lambda i, j: (0, 0)),
                  pl.BlockSpec((d, tn), lambda i, j: (0, j)),
                  pl.BlockSpec((1, tn), lambda i, j: (0, j))],
        out_specs=pl.BlockSpec((tm, tn), lambda i, j: (i, j)),
        out_shape=jax.ShapeDtypeStruct((t, n), F32),
        scratch_shapes=[pltpu.VMEM((tm, d), BF16), pltpu.VMEM((n // tn, 8, tn), F32)],
        compiler_params=_cparams(("arbitrary", "arbitrary")),
        name="inproj",
    )(x2, mod3, gpre, w, mu)


def _pool_kernel(u_ref, w_ref, s_ref, o_ref, ext_ref, *, tps):
    i = pl.program_id(0)
    tm = u_ref.shape[0]
    pad = max(POOL_WINDOWS)

    @pl.when(i % tps == 0)
    def _():
        ext_ref[0:pad, :] = jnp.zeros((pad, ext_ref.shape[1]), F32)

    ext_ref[pad:pad + tm, :] = u_ref[...]
    t_in_seq = (i % tps) * tm + lax.broadcasted_iota(jnp.int32, (tm, 1), 0)
    tpos = (t_in_seq + 1).astype(F32)
    for gi, win in enumerate(POOL_WINDOWS):
        lo = gi * POOL_GROUP
        acc = ext_ref[pad:pad + tm, lo:lo + POOL_GROUP]
        u_g = acc
        for dlt in range(1, win):
            acc = acc + ext_ref[pad - dlt:pad - dlt + tm, lo:lo + POOL_GROUP]
        pooled = acc / jnp.minimum(tpos, float(win)) - u_g
        mixed = _bdot(pooled, w_ref[gi])
        o_ref[:, lo:lo + POOL_GROUP] = (mixed * s_ref[:, lo:lo + POOL_GROUP]).astype(o_ref.dtype)
    ext_ref[0:pad, :] = ext_ref[tm:tm + pad, :]


def _pool(p_all, col_block, pool_w, pool_scale, *, seq, tm=512):
    t = p_all.shape[0]
    pw = pool_scale.shape[1]
    tps = seq // tm
    return pl.pallas_call(
        functools.partial(_pool_kernel, tps=tps),
        grid=(t // tm,),
        in_specs=[pl.BlockSpec((tm, pw), lambda i: (i, col_block)),
                  pl.BlockSpec(pool_w.shape, lambda i: (0, 0, 0)),
                  pl.BlockSpec((1, pw), lambda i: (0, 0))],
        out_specs=pl.BlockSpec((tm, pw), lambda i: (i, 0)),
        out_shape=jax.ShapeDtypeStruct((t, pw), BF16),
        scratch_shapes=[pltpu.VMEM((tm + max(POOL_WINDOWS), pw), F32)],
        compiler_params=_cparams(("arbitrary",)),
        name="pool",
    )(p_all, pool_w, pool_scale)


def _lora_kernel(p_ref, w2_ref, a2_ref, g2_ref, w0_ref, a0_ref, ld_ref, a_ref, g_ref):
    p = p_ref[...]
    z = w0_ref[...] + _bdot(jnp.tanh(p), w2_ref[...])
    w_log = -(jnp.maximum(-z, 0.0) + jnp.log(1.0 + jnp.exp(-jnp.abs(z)))) - 0.5
    ld_ref[...] = -jnp.exp(w_log)
    a_ref[...] = _sigmoid(a0_ref[...] + _bdot(p, a2_ref[...]))
    g_ref[...] = _bdot(_sigmoid(p), g2_ref[...])


def _lora(p_all, col_block, w2p, a2p, g2p, w0, a0, *, tm=512):
    t = p_all.shape[0]
    kw, r = w2p.shape
    full = lambda shape: pl.BlockSpec(shape, lambda i: (0, 0))
    out = jax.ShapeDtypeStruct((t, r), F32)
    return pl.pallas_call(
        _lora_kernel,
        grid=(t // tm,),
        in_specs=[pl.BlockSpec((tm, kw), lambda i: (i, col_block)),
                  full((kw, r)), full((kw, r)), full((kw, r)), full((1, r)), full((1, r))],
        out_specs=[pl.BlockSpec((tm, r), lambda i: (i, 0))] * 3,
        out_shape=[out, out, out],
        compiler_params=_cparams(("arbitrary",)),
        name="lora",
    )(p_all, w2p, a2p, g2p, w0, a0)


def _scan_kernel(r_ref, k_ref, v_ref, ld_ref, a_ref, g_ref, kkw_ref, ka_ref, rk_ref, lw_ref, lb_ref,
                 o_ref, h_ref):
    C = CHUNK
    P = LANES
    N = HEAD_SIZE

    @pl.when(pl.program_id(2) == 0)
    def _():
        h_ref[...] = jnp.zeros_like(h_ref)

    lane = lax.broadcasted_iota(jnp.int32, (1, P), 1)
    in_a = lane < N
    ri = lax.broadcasted_iota(jnp.int32, (2 * C, 2 * C), 0)
    ci = lax.broadcasted_iota(jnp.int32, (2 * C, 2 * C), 1)

    def same(b):
        sh = b.bit_length() - 1
        return (ri >> sh) == (ci >> sh)

    strict = (ri > ci) & same(C)
    incl = (ri >= ci) & same(C)
    eye = ri == ci
    blk16 = strict & same(16)
    blk32 = strict & same(32) & jnp.logical_not(same(16))
    blk64 = strict & jnp.logical_not(same(32))
    eye_f = eye.astype(F32)
    tri3 = (lax.broadcasted_iota(jnp.int32, (C, 3 * C), 0)
            >= (lax.broadcasted_iota(jnp.int32, (C, 3 * C), 1) & (C - 1))).astype(BF16)

    def stack(x):
        return jnp.concatenate([jnp.where(in_a, x, 0.0), jnp.where(in_a, 0.0, x)], axis=0)

    def headsum(x):
        first = jnp.sum(jnp.where(in_a, x, 0.0), axis=-1, keepdims=True)
        second = jnp.sum(jnp.where(in_a, 0.0, x), axis=-1, keepdims=True)
        return jnp.where(in_a, first, second)

    def pair_chunk(c, q):
        sl = pl.ds(c * C, C)
        ln = pl.ds(q * P, P)
        kkw = kkw_ref[:, ln]
        ka = ka_ref[:, ln]
        rk = rk_ref[:, ln]
        lw = lw_ref[:, ln]
        lb = lb_ref[:, ln]
        r = r_ref[sl, ln]
        k = k_ref[sl, ln]
        v = v_ref[sl, ln]
        ld = ld_ref[sl, ln]
        a = a_ref[sl, ln]

        kkraw = k * kkw
        kmod = k * (1.0 + (a - 1.0) * ka)
        sums = headsum(jnp.concatenate([kkraw * kkraw, r * kmod * rk], axis=0))
        ld_hi = ld.astype(BF16)
        rem = ld - ld_hi.astype(F32)
        ld_mid = rem.astype(BF16)
        ld_lo = (rem - ld_mid.astype(F32)).astype(BF16)
        cum = jnp.dot(tri3, jnp.concatenate([ld_hi, ld_mid, ld_lo], axis=0), preferred_element_type=F32)
        yield
        kk = kkraw / jnp.maximum(jnp.sqrt(sums[:C]), 1e-12)
        bonus = sums[C:]
        cum_last = cum[C - 1:C, :]

        inv_g = jnp.exp(-cum)
        to_end = jnp.exp(cum_last - cum)
        kb = kk * a
        rt = r * jnp.exp(cum)
        at = -kk * jnp.exp(cum - ld)
        bt = kb * inv_g
        kt = kmod * inv_g
        g_end = jnp.exp(cum_last)

        s_v = stack(v).astype(BF16)
        lhs = jnp.concatenate([stack(at), stack(rt)], axis=0).astype(BF16)
        rhs = jnp.concatenate([stack(bt), stack(kt)], axis=0).astype(BF16)
        scores = lax.dot_general(lhs, rhs, (((1,), (1,)), ((), ())), preferred_element_type=F32)
        s_at = lhs[:2 * C]
        s_rt = lhs[2 * C:]
        yield
        s_ab = scores[:2 * C, :2 * C]
        s_ak = jnp.where(strict, scores[:2 * C, 2 * C:], 0.0)
        s_rb = jnp.where(incl, scores[2 * C:, :2 * C], 0.0)
        s_rk = jnp.where(incl, scores[2 * C:, 2 * C:], 0.0)

        x1 = jnp.where(blk16, s_ab, 0.0)
        x2 = _bdot(x1, x1)
        yield
        x4 = _bdot(x2, x2)
        tinv = eye_f + x1
        tinv = tinv + _bdot(tinv, x2)
        yield
        x8 = _bdot(x4, x4)
        tinv = tinv + _bdot(tinv, x4)
        yield
        tinv = tinv + _bdot(tinv, x8)
        yield
        for blk in (blk32, blk64):
            off = jnp.where(blk, s_ab, 0.0)
            part = _bdot(tinv, off)
            yield
            tinv = tinv + _bdot(part, tinv)
            yield

        h = h_ref[q]
        h_b = h.astype(BF16)
        rhs_sa = jnp.dot(jnp.concatenate([s_at, s_ak.astype(BF16)], axis=1),
                         jnp.concatenate([h_b, s_v], axis=0), preferred_element_type=F32)
        yield
        s_sa = _bdot(tinv, rhs_sa)
        yield
        sa_v = jnp.concatenate([s_sa.astype(BF16), s_v], axis=0)
        y_s = jnp.dot(jnp.concatenate([s_rt, s_rb.astype(BF16), s_rk.astype(BF16)], axis=1),
                      jnp.concatenate([h_b, sa_v], axis=0), preferred_element_type=F32)
        upd_l = jnp.concatenate([stack(kb * to_end), stack(kmod * to_end)], axis=0).astype(BF16)
        upd = lax.dot_general(upd_l, sa_v, (((0,), (0,)), ((), ())), preferred_element_type=F32)
        yield
        y = y_s[:C] + y_s[C:]
        g_col = jnp.sum(jnp.where(eye, g_end, 0.0), axis=1, keepdims=True)
        h_ref[q] = h * g_col + upd

        mean = headsum(y) * (1.0 / N)
        yield
        dev = y - mean
        var = headsum(dev * dev) * (1.0 / N)
        yield
        yn = dev * lax.rsqrt(var + LN_X_EPS) * lw + lb
        o_ref[sl, ln] = ((yn + bonus * v) * g_ref[sl, ln]).astype(o_ref.dtype)

    n_chunks = r_ref.shape[0] // C
    chains = [[pair_chunk(c, q) for q in range(r_ref.shape[1] // P)] for c in range(n_chunks)]
    sweep = 0
    while any(chains):
        for c in range(min(n_chunks, sweep // CHUNK_LAG + 1)):
            advanced = []
            for gen in chains[c]:
                try:
                    next(gen)
                    advanced.append(gen)
                except StopIteration:
                    pass
            chains[c] = advanced
        sweep += 1


def _scan(p_all, ld, a, g, kkw, ka, rk, lw, lb, *, batch, seq, r_block, k_block, v_block, tb=4 * CHUNK, pairs=12):
    t, rw = ld.shape
    width = pairs * LANES
    tpb = seq // tb
    tok = lambda blk0: pl.BlockSpec((tb, width), lambda b, p, s: (b * tpb + s, blk0 // pairs + p))
    par = pl.BlockSpec((1, width), lambda b, p, s: (0, p))
    return pl.pallas_call(
        _scan_kernel,
        grid=(batch, rw // width, tpb),
        in_specs=[tok(r_block), tok(k_block), tok(v_block), tok(0), tok(0), tok(0),
                  par, par, par, par, par],
        out_specs=tok(0),
        out_shape=jax.ShapeDtypeStruct((t, rw), BF16),
        scratch_shapes=[pltpu.VMEM((pairs, LANES, LANES), F32)],
        compiler_params=_cparams(("arbitrary", "arbitrary", "arbitrary")),
        name="scan",
    )(p_all, p_all, p_all, ld, a, g, kkw, ka, rk, lw, lb)


def _outproj_kernel(x_ref, yp_ref, yr_ref, mod_ref, gpost_ref, wp_ref, wr_ref, o_ref):
    y = (jnp.dot(yp_ref[...], wp_ref[...], preferred_element_type=F32)
         + jnp.dot(yr_ref[...], wr_ref[...], preferred_element_type=F32))
    _residual_into(o_ref, x_ref, gpost_ref, mod_ref, 1, 1.0, y_all=y)


def _outproj(x2, y_pool, y_rwkv, mod3, gpost, wp, wr, *, seq, tm=512):
    t, d = x2.shape
    tps = seq // tm
    return pl.pallas_call(
        _outproj_kernel,
        grid=(t // tm,),
        in_specs=[pl.BlockSpec((tm, d), lambda i: (i, 0)),
                  pl.BlockSpec((tm, y_pool.shape[1]), lambda i: (i, 0)),
                  pl.BlockSpec((tm, y_rwkv.shape[1]), lambda i: (i, 0)),
                  pl.BlockSpec((1,) + mod3.shape[1:], lambda i: (i // tps, 0, 0)),
                  pl.BlockSpec((1, d), lambda i: (0, 0)),
                  pl.BlockSpec(wp.shape, lambda i: (0, 0)),
                  pl.BlockSpec(wr.shape, lambda i: (0, 0))],
        out_specs=pl.BlockSpec((tm, d), lambda i: (i, 0)),
        out_shape=jax.ShapeDtypeStruct((t, d), F32),
        compiler_params=_cparams(("arbitrary",)),
        name="outproj",
    )(x2, y_pool, y_rwkv, mod3, gpost, wp, wr)


def _pad_cols(w, n):
    return jnp.pad(w, ((0, 0), (0, n - w.shape[1])))


def _layer(x2, c, batch, seq, w_ada, b_ada, norm_pre, norm_post, f1g, f1u, f1d, w_in, mu_shift, pool_w,
           pool_scale, w0, w2, a0, a2, g2, k_k, k_a, r_k, lnx_w, lnx_b, w_out, f2g, f2u, f2d):
    d = x2.shape[1]
    pool_width = pool_scale.shape[0]
    rw = w0.shape[0]
    n_sub = norm_pre.shape[0]
    col_tile = 512

    mod3 = _ada(c, w_ada, b_ada).reshape(batch, n_sub * N_MOD, d)
    row = lambda vec: vec.reshape(1, -1)

    x2 = _ffn(x2, mod3, row(norm_pre[0]), row(norm_post[0]), f1g.astype(BF16), f1u.astype(BF16),
              f1d.astype(BF16), sub=0, seq=seq)

    n_lora = w_in.shape[1] - pool_width - 3 * rw
    assert (3 * rw) % col_tile == 0 and n_lora <= col_tile and pool_width == col_tile and rw % LANES == 0
    w_in_p = jnp.concatenate([w_in[:, pool_width:pool_width + 3 * rw],
                              _pad_cols(w_in[:, pool_width + 3 * rw:], col_tile),
                              w_in[:, :pool_width]], axis=1).astype(BF16)
    mu_p = jnp.concatenate([mu_shift, jnp.zeros((col_tile - n_lora + pool_width,), F32)])
    p_all = _inproj(x2, mod3, row(norm_pre[1]), w_in_p, row(mu_p), seq=seq, tm=1024,
                    tn=w_in_p.shape[1] // 4)
    lora_block = 3 * rw // col_tile

    y_pool = _pool(p_all, lora_block + 1, pool_w.astype(BF16), row(pool_scale), seq=seq)

    n_w, n_a = w2.shape[0], a2.shape[0]
    w2p = jnp.pad(w2, ((0, col_tile - n_w), (0, 0))).astype(BF16)
    a2p = jnp.pad(a2, ((n_w, col_tile - n_w - n_a), (0, 0))).astype(BF16)
    g2p = jnp.pad(g2, ((n_w + n_a, col_tile - n_lora), (0, 0))).astype(BF16)
    ld, a, g = _lora(p_all, lora_block, w2p, a2p, g2p, row(w0), row(a0))

    y_rwkv = _scan(p_all, ld, a, g, row(k_k), row(k_a), row(r_k), row(lnx_w), row(lnx_b),
                   batch=batch, seq=seq, r_block=0, k_block=rw // LANES, v_block=2 * rw // LANES)

    x2 = _outproj(x2, y_pool, y_rwkv, mod3, row(norm_post[1]), w_out[:pool_width].astype(BF16),
                  w_out[pool_width:].astype(BF16), seq=seq)

    x2 = _ffn(x2, mod3, row(norm_pre[2]), row(norm_post[2]), f2g.astype(BF16), f2u.astype(BF16),
              f2d.astype(BF16), sub=2, seq=seq)
    return x2


def kernel(x, c, w_ada, b_ada, norm_pre, norm_post, ffn1_w_gate, ffn1_w_up, ffn1_w_down, w_in, mu_shift,
           pool_w, pool_scale, w0, w2, a0, a2, g2, k_k, k_a, r_k, lnx_w, lnx_b, w_out, ffn2_w_gate,
           ffn2_w_up, ffn2_w_down):
    batch, seq, d = x.shape
    x2 = x.reshape(batch * seq, d)
    for l in range(w_ada.shape[0]):
        x2 = _layer(x2, c, batch, seq, w_ada[l], b_ada[l], norm_pre[l], norm_post[l], ffn1_w_gate[l],
                    ffn1_w_up[l], ffn1_w_down[l], w_in[l], mu_shift[l], pool_w[l], pool_scale[l], w0[l],
                    w2[l], a0[l], a2[l], g2[l], k_k[l], k_a[l], r_k[l], lnx_w[l], lnx_b[l], w_out[l],
                    ffn2_w_gate[l], ffn2_w_up[l], ffn2_w_down[l])
    return x2.reshape(batch, seq, d)
```

```python
import functools

import jax
import jax.numpy as jnp
from jax import lax
from jax.experimental import pallas as pl
from jax.experimental.pallas import tpu as pltpu

F32 = jnp.float32
BF16 = jnp.bfloat16

NORM_EPS = 1e-6
HEAD_SIZE = 64
LN_X_EPS = 1e-5 * HEAD_SIZE
POOL_WINDOWS = (2, 4, 8, 16)
POOL_GROUP = 128
MACARON_WEIGHT = 0.5
N_MOD = 3

LANES = 128
CHUNK = 64
CHUNK_LAG = 8
ROW_BLOCK = 16
VMEM_LIMIT_MIB = 56


def _cparams(sem, vmem_mib=VMEM_LIMIT_MIB):
    return pltpu.CompilerParams(dimension_semantics=sem, vmem_limit_bytes=vmem_mib * 1024 * 1024)


def _sigmoid(z):
    return 1.0 / (1.0 + jnp.exp(-z))


def _row_blocks(n_rows, fn):
    for b in range(n_rows // ROW_BLOCK):
        fn(pl.ds(b * ROW_BLOCK, ROW_BLOCK))


def _norm_mod_into(h_ref, x_ref, gain_ref, mod_ref, sub):
    shift = mod_ref[0, N_MOD * sub:N_MOD * sub + 1, :]
    scale = mod_ref[0, N_MOD * sub + 1:N_MOD * sub + 2, :]
    mult = gain_ref[...] * (1.0 + scale)
    blocks = []

    def block(rows):
        x = x_ref[rows, :]
        inv = lax.rsqrt(jnp.mean(x * x, axis=-1, keepdims=True) + NORM_EPS)
        blocks.append((x * inv * mult + shift).astype(BF16))
        h_ref[rows, :] = blocks[-1]

    _row_blocks(x_ref.shape[0], block)
    return jnp.concatenate(blocks, axis=0)


def _residual_into(o_ref, x_ref, gain_ref, mod_ref, sub, weight, y_all=None):
    gate = mod_ref[0, N_MOD * sub + 2:N_MOD * sub + 3, :]
    mult = gain_ref[...] * (weight * (1.0 + gate))

    def block(rows):
        y = o_ref[rows, :] if y_all is None else y_all[rows.start:rows.start + rows.size, :]
        inv = lax.rsqrt(jnp.mean(y * y, axis=-1, keepdims=True) + NORM_EPS)
        o_ref[rows, :] = x_ref[rows, :] + y * inv * mult

    _row_blocks(o_ref.shape[0], block)


def _bdot(a, b):
    return jnp.dot(a.astype(BF16), b.astype(BF16), preferred_element_type=F32)


def _ada_kernel(c_ref, w_ref, b_ref, o_ref):
    c = c_ref[...]
    s = c * _sigmoid(c)
    o_ref[...] = jnp.dot(s, w_ref[...], preferred_element_type=F32,
                         precision=lax.Precision.HIGHEST) + b_ref[...]


def _ada(c, w, b, tn=1024):
    bsz, d = c.shape
    n = w.shape[1]
    return pl.pallas_call(
        _ada_kernel,
        grid=(n // tn,),
        in_specs=[pl.BlockSpec((bsz, d), lambda j: (0, 0)),
                  pl.BlockSpec((d, tn), lambda j: (0, j)),
                  pl.BlockSpec((1, tn), lambda j: (0, j))],
        out_specs=pl.BlockSpec((bsz, tn), lambda j: (0, j)),
        out_shape=jax.ShapeDtypeStruct((bsz, n), F32),
        compiler_params=_cparams(("arbitrary",)),
        name="ada",
    )(c, w, b.reshape(1, n))


def _ffn_kernel(x_ref, mod_ref, gpre_ref, gpost_ref, wg_ref, wu_ref, wd_ref, o_ref, h_ref, *, sub):
    j = pl.program_id(1)
    last = pl.num_programs(1) - 1

    def chunk(h):
        g = jnp.dot(h, wg_ref[...], preferred_element_type=F32)
        u = jnp.dot(h, wu_ref[...], preferred_element_type=F32)
        act = (g * _sigmoid(g) * u).astype(BF16)
        return jnp.dot(act, wd_ref[...], preferred_element_type=F32)

    @pl.when(j == 0)
    def _():
        o_ref[...] = chunk(_norm_mod_into(h_ref, x_ref, gpre_ref, mod_ref, sub))

    @pl.when((j > 0) & (j < last))
    def _():
        o_ref[...] += chunk(h_ref[...])

    @pl.when(j == last)
    def _():
        _residual_into(o_ref, x_ref, gpost_ref, mod_ref, sub, MACARON_WEIGHT,
                       y_all=o_ref[...] + chunk(h_ref[...]))


def _ffn(x2, mod3, gpre, gpost, wg, wu, wd, *, sub, seq, tm=1024, tf=512):
    t, d = x2.shape
    f = wg.shape[1]
    tps = seq // tm
    spill_mib = 5
    vmem_mib = (2 * 2 * tm * d * 4 + tm * d * 2 + 3 * 2 * d * tf * 2 + tm * tf * (4 + 4 + 2)) // 2**20 + spill_mib
    return pl.pallas_call(
        functools.partial(_ffn_kernel, sub=sub),
        grid=(t // tm, f // tf),
        in_specs=[pl.BlockSpec((tm, d), lambda i, j: (i, 0)),
                  pl.BlockSpec((1,) + mod3.shape[1:], lambda i, j: (i // tps, 0, 0)),
                  pl.BlockSpec((1, d), lambda i, j: (0, 0)),
                  pl.BlockSpec((1, d), lambda i, j: (0, 0)),
                  pl.BlockSpec((d, tf), lambda i, j: (0, j)),
                  pl.BlockSpec((d, tf), lambda i, j: (0, j)),
                  pl.BlockSpec((tf, d), lambda i, j: (j, 0))],
        out_specs=pl.BlockSpec((tm, d), lambda i, j: (i, 0)),
        out_shape=jax.ShapeDtypeStruct((t, d), F32),
        scratch_shapes=[pltpu.VMEM((tm, d), BF16)],
        compiler_params=_cparams(("arbitrary", "arbitrary"), vmem_mib),
        name=f"ffn{sub}",
    )(x2, mod3, gpre, gpost, wg, wu, wd)


def _inproj_kernel(x_ref, mod_ref, gpre_ref, w_ref, mu_ref, o_ref, h_ref, carry_ref, *, tps):
    i = pl.program_id(0)
    j = pl.program_id(1)

    def project(h):
        res = jnp.dot(h, w_ref[...], preferred_element_type=F32)
        tm, tn = res.shape
        carried = carry_ref[j][0:1, :]
        first = jnp.where(i % tps == 0, jnp.zeros_like(carried), carried)
        row = lax.broadcasted_iota(jnp.int32, (tm, 1), 0)
        prev = jnp.where(row == 0, first, pltpu.roll(res, 1, 0))
        carry_ref[j] = jnp.broadcast_to(res[tm - 1:tm, :], (8, tn))
        o_ref[...] = res + mu_ref[...] * (prev - res)

    @pl.when(j == 0)
    def _():
        project(_norm_mod_into(h_ref, x_ref, gpre_ref, mod_ref, 1))

    @pl.when(j > 0)
    def _():
        project(h_ref[...])


def _inproj(x2, mod3, gpre, w, mu, *, seq, tm=512, tn=512):
    t, d = x2.shape
    n = w.shape[1]
    tps = seq // tm
    return pl.pallas_call(
        functools.partial(_inproj_kernel, tps=tps),
        grid=(t // tm, n // tn),
        in_specs=[pl.BlockSpec((tm, d), lambda i, j: (i, 0)),
                  pl.BlockSpec((1,) + mod3.shape[1:], lambda i, j: (i // tps, 0, 0)),
                  pl.BlockSpec((1, d), lambda i, j: (0, 0)),
                  pl.BlockSpec((d, tn), lambda i, j: (0, j)),
                  pl.BlockSpec((1, tn), lambda i, j: (0, j))],
        out_specs=pl.BlockSpec((tm, tn), lambda i, j: (i, j)),
        out_shape=jax.ShapeDtypeStruct((t, n), F32),
        scratch_shapes=[pltpu.VMEM((tm, d), BF16), pltpu.VMEM((n // tn, 8, tn), F32)],
        compiler_params=_cparams(("arbitrary", "arbitrary")),
        name="inproj",
    )(x2, mod3, gpre, w, mu)


def _pool_kernel(u_ref, w_ref, s_ref, o_ref, ext_ref, *, tps):
    i = pl.program_id(0)
    tm = u_ref.shape[0]
    pad = max(POOL_WINDOWS)

    @pl.when(i % tps == 0)
    def _():
        ext_ref[0:pad, :] = jnp.zeros((pad, ext_ref.shape[1]), F32)

    ext_ref[pad:pad + tm, :] = u_ref[...]
    t_in_seq = (i % tps) * tm + lax.broadcasted_iota(jnp.int32, (tm, 1), 0)
    tpos = (t_in_seq + 1).astype(F32)
    for gi, win in enumerate(POOL_WINDOWS):
        lo = gi * POOL_GROUP
        acc = ext_ref[pad:pad + tm, lo:lo + POOL_GROUP]
        u_g = acc
        for dlt in range(1, win):
            acc = acc + ext_ref[pad - dlt:pad - dlt + tm, lo:lo + POOL_GROUP]
        pooled = acc / jnp.minimum(tpos, float(win)) - u_g
        mixed = _bdot(pooled, w_ref[gi])
        o_ref[:, lo:lo + POOL_GROUP] = (mixed * s_ref[:, lo:lo + POOL_GROUP]).astype(o_ref.dtype)
    ext_ref[0:pad, :] = ext_ref[tm:tm + pad, :]


def _pool(p_all, col_block, pool_w, pool_scale, *, seq, tm=512):
    t = p_all.shape[0]
    pw = pool_scale.shape[1]
    tps = seq // tm
    return pl.pallas_call(
        functools.partial(_pool_kernel, tps=tps),
        grid=(t // tm,),
        in_specs=[pl.BlockSpec((tm, pw), lambda i: (i, col_block)),
                  pl.BlockSpec(pool_w.shape, lambda i: (0, 0, 0)),
                  pl.BlockSpec((1, pw), lambda i: (0, 0))],
        out_specs=pl.BlockSpec((tm, pw), lambda i: (i, 0)),
        out_shape=jax.ShapeDtypeStruct((t, pw), BF16),
        scratch_shapes=[pltpu.VMEM((tm + max(POOL_WINDOWS), pw), F32)],
        compiler_params=_cparams(("arbitrary",)),
        name="pool",
    )(p_all, pool_w, pool_scale)


def _lora_kernel(p_ref, w2_ref, a2_ref, g2_ref, w0_ref, a0_ref, ld_ref, a_ref, g_ref):
    wa = p_ref[:, :LANES]
    gx = p_ref[:, LANES:LANES + g2_ref.shape[0]]
    z = w0_ref[...] + _bdot(jnp.tanh(wa), w2_ref[...])
    w_log = -(jnp.maximum(-z, 0.0) + jnp.log(1.0 + jnp.exp(-jnp.abs(z)))) - 0.5
    ld_ref[...] = -jnp.exp(w_log)
    a_ref[...] = _sigmoid(a0_ref[...] + _bdot(wa, a2_ref[...]))
    g_ref[...] = _bdot(_sigmoid(gx), g2_ref[...])


def _lora(p_all, col_block, col_tile, w2p, a2p, g2p, w0, a0, *, tm=512):
    t = p_all.shape[0]
    r = w2p.shape[1]
    full = lambda arr: pl.BlockSpec(arr.shape, lambda i: (0, 0))
    out = jax.ShapeDtypeStruct((t, r), F32)
    return pl.pallas_call(
        _lora_kernel,
        grid=(t // tm,),
        in_specs=[pl.BlockSpec((tm, col_tile), lambda i: (i, col_block)),
                  full(w2p), full(a2p), full(g2p), full(w0), full(a0)],
        out_specs=[pl.BlockSpec((tm, r), lambda i: (i, 0))] * 3,
        out_shape=[out, out, out],
        compiler_params=_cparams(("arbitrary",)),
        name="lora",
    )(p_all, w2p, a2p, g2p, w0, a0)


def _scan_kernel(r_ref, k_ref, v_ref, ld_ref, a_ref, g_ref, kkw_ref, ka_ref, rk_ref, lw_ref, lb_ref,
                 o_ref, h_ref):
    C = CHUNK
    P = LANES
    N = HEAD_SIZE

    @pl.when(pl.program_id(2) == 0)
    def _():
        h_ref[...] = jnp.zeros_like(h_ref)

    lane = lax.broadcasted_iota(jnp.int32, (1, P), 1)
    in_a = lane < N
    ri = lax.broadcasted_iota(jnp.int32, (2 * C, 2 * C), 0)
    ci = lax.broadcasted_iota(jnp.int32, (2 * C, 2 * C), 1)

    def same(b):
        sh = b.bit_length() - 1
        return (ri >> sh) == (ci >> sh)

    strict = (ri > ci) & same(C)
    incl = (ri >= ci) & same(C)
    eye = ri == ci
    blk16 = strict & same(16)
    blk32 = strict & same(32) & jnp.logical_not(same(16))
    blk64 = strict & jnp.logical_not(same(32))
    eye_f = eye.astype(F32)
    tri3 = (lax.broadcasted_iota(jnp.int32, (C, 3 * C), 0)
            >= (lax.broadcasted_iota(jnp.int32, (C, 3 * C), 1) & (C - 1))).astype(BF16)

    def stack(x):
        return jnp.concatenate([jnp.where(in_a, x, 0.0), jnp.where(in_a, 0.0, x)], axis=0)

    def headsum(x):
        first = jnp.sum(jnp.where(in_a, x, 0.0), axis=-1, keepdims=True)
        second = jnp.sum(jnp.where(in_a, 0.0, x), axis=-1, keepdims=True)
        return jnp.where(in_a, first, second)

    def pair_chunk(c, q):
        sl = pl.ds(c * C, C)
        ln = pl.ds(q * P, P)
        kkw = kkw_ref[:, ln]
        ka = ka_ref[:, ln]
        rk = rk_ref[:, ln]
        lw = lw_ref[:, ln]
        lb = lb_ref[:, ln]
        r = r_ref[sl, ln]
        k = k_ref[sl, ln]
        v = v_ref[sl, ln]
        ld = ld_ref[sl, ln]
        a = a_ref[sl, ln]

        kkraw = k * kkw
        kmod = k * (1.0 + (a - 1.0) * ka)
        sums = headsum(jnp.concatenate([kkraw * kkraw, r * kmod * rk], axis=0))
        ld_hi = ld.astype(BF16)
        rem = ld - ld_hi.astype(F32)
        ld_mid = rem.astype(BF16)
        ld_lo = (rem - ld_mid.astype(F32)).astype(BF16)
        cum = jnp.dot(tri3, jnp.concatenate([ld_hi, ld_mid, ld_lo], axis=0), preferred_element_type=F32)
        yield
        kk = kkraw / jnp.maximum(jnp.sqrt(sums[:C]), 1e-12)
        bonus = sums[C:]
        cum_last = cum[C - 1:C, :]

        inv_g = jnp.exp(-cum)
        to_end = jnp.exp(cum_last - cum)
        kb = kk * a
        rt = r * jnp.exp(cum)
        at = -kk * jnp.exp(cum - ld)
        bt = kb * inv_g
        kt = kmod * inv_g
        g_end = jnp.exp(cum_last)

        s_v = stack(v).astype(BF16)
        lhs = jnp.concatenate([stack(at), stack(rt)], axis=0).astype(BF16)
        rhs = jnp.concatenate([stack(bt), stack(kt)], axis=0).astype(BF16)
        scores = lax.dot_general(lhs, rhs, (((1,), (1,)), ((), ())), preferred_element_type=F32)
        s_at = lhs[:2 * C]
        s_rt = lhs[2 * C:]
        yield
        s_ab = scores[:2 * C, :2 * C]
        s_ak = jnp.where(strict, scores[:2 * C, 2 * C:], 0.0)
        s_rb = jnp.where(incl, scores[2 * C:, :2 * C], 0.0)
        s_rk = jnp.where(incl, scores[2 * C:, 2 * C:], 0.0)

        x1 = jnp.where(blk16, s_ab, 0.0)
        x2 = _bdot(x1, x1)
        yield
        x4 = _bdot(x2, x2)
        tinv = eye_f + x1
        tinv = tinv + _bdot(tinv, x2)
        yield
        x8 = _bdot(x4, x4)
        tinv = tinv + _bdot(tinv, x4)
        yield
        tinv = tinv + _bdot(tinv, x8)
        yield
        for blk in (blk32, blk64):
            off = jnp.where(blk, s_ab, 0.0)
            part = _bdot(tinv, off)
            yield
            tinv = tinv + _bdot(part, tinv)
            yield

        h = h_ref[q]
        h_b = h.astype(BF16)
        rhs_sa = jnp.dot(jnp.concatenate([s_at, s_ak.astype(BF16)], axis=1),
                         jnp.concatenate([h_b, s_v], axis=0), preferred_element_type=F32)
        yield
        s_sa = _bdot(tinv, rhs_sa)
        yield
        sa_v = jnp.concatenate([s_sa.astype(BF16), s_v], axis=0)
        y_s = jnp.dot(jnp.concatenate([s_rt, s_rb.astype(BF16), s_rk.astype(BF16)], axis=1),
                      jnp.concatenate([h_b, sa_v], axis=0), preferred_element_type=F32)
        upd_l = jnp.concatenate([stack(kb * to_end), stack(kmod * to_end)], axis=0).astype(BF16)
        upd = lax.dot_general(upd_l, sa_v, (((0,), (0,)), ((), ())), preferred_element_type=F32)
        yield
        y = y_s[:C] + y_s[C:]
        g_col = jnp.sum(jnp.where(eye, g_end, 0.0), axis=1, keepdims=True)
        h_ref[q] = h * g_col + upd

        mean = headsum(y) * (1.0 / N)
        yield
        dev = y - mean
        var = headsum(dev * dev) * (1.0 / N)
        yield
        yn = dev * lax.rsqrt(var + LN_X_EPS) * lw + lb
        o_ref[sl, ln] = ((yn + bonus * v) * g_ref[sl, ln]).astype(o_ref.dtype)

    n_chunks = r_ref.shape[0] // C
    chains = [[pair_chunk(c, q) for q in range(r_ref.shape[1] // P)] for c in range(n_chunks)]
    sweep = 0
    while any(chains):
        for c in range(min(n_chunks, sweep // CHUNK_LAG + 1)):
            advanced = []
            for gen in chains[c]:
                try:
                    next(gen)
                    advanced.append(gen)
                except StopIteration:
                    pass
            chains[c] = advanced
        sweep += 1


def _scan(p_all, ld, a, g, kkw, ka, rk, lw, lb, *, batch, seq, r_block, k_block, v_block, tb=4 * CHUNK, pairs=12):
    t, rw = ld.shape
    width = pairs * LANES
    tpb = seq // tb
    tok = lambda blk0: pl.BlockSpec((tb, width), lambda b, p, s: (b * tpb + s, blk0 // pairs + p))
    par = pl.BlockSpec((1, width), lambda b, p, s: (0, p))
    return pl.pallas_call(
        _scan_kernel,
        grid=(batch, rw // width, tpb),
        in_specs=[tok(r_block), tok(k_block), tok(v_block), tok(0), tok(0), tok(0),
                  par, par, par, par, par],
        out_specs=tok(0),
        out_shape=jax.ShapeDtypeStruct((t, rw), BF16),
        scratch_shapes=[pltpu.VMEM((pairs, LANES, LANES), F32)],
        compiler_params=_cparams(("arbitrary", "arbitrary", "arbitrary")),
        name="scan",
    )(p_all, p_all, p_all, ld, a, g, kkw, ka, rk, lw, lb)


def _outproj_kernel(x_ref, yp_ref, yr_ref, mod_ref, gpost_ref, wp_ref, wr_ref, o_ref):
    y = (jnp.dot(yp_ref[...], wp_ref[...], preferred_element_type=F32)
         + jnp.dot(yr_ref[...], wr_ref[...], preferred_element_type=F32))
    _residual_into(o_ref, x_ref, gpost_ref, mod_ref, 1, 1.0, y_all=y)


def _outproj(x2, y_pool, y_rwkv, mod3, gpost, wp, wr, *, seq, tm=512):
    t, d = x2.shape
    tps = seq // tm
    return pl.pallas_call(
        _outproj_kernel,
        grid=(t // tm,),
        in_specs=[pl.BlockSpec((tm, d), lambda i: (i, 0)),
                  pl.BlockSpec((tm, y_pool.shape[1]), lambda i: (i, 0)),
                  pl.BlockSpec((tm, y_rwkv.shape[1]), lambda i: (i, 0)),
                  pl.BlockSpec((1,) + mod3.shape[1:], lambda i: (i // tps, 0, 0)),
                  pl.BlockSpec((1, d), lambda i: (0, 0)),
                  pl.BlockSpec(wp.shape, lambda i: (0, 0)),
                  pl.BlockSpec(wr.shape, lambda i: (0, 0))],
        out_specs=pl.BlockSpec((tm, d), lambda i: (i, 0)),
        out_shape=jax.ShapeDtypeStruct((t, d), F32),
        compiler_params=_cparams(("arbitrary",)),
        name="outproj",
    )(x2, y_pool, y_rwkv, mod3, gpost, wp, wr)


def _pad_cols(w, n):
    return jnp.pad(w, ((0, 0), (0, n - w.shape[1])))


def _layer(x2, c, batch, seq, w_ada, b_ada, norm_pre, norm_post, f1g, f1u, f1d, w_in, mu_shift, pool_w,
           pool_scale, w0, w2, a0, a2, g2, k_k, k_a, r_k, lnx_w, lnx_b, w_out, f2g, f2u, f2d):
    d = x2.shape[1]
    pool_width = pool_scale.shape[0]
    rw = w0.shape[0]
    n_sub = norm_pre.shape[0]
    col_tile = 512

    mod3 = _ada(c, w_ada, b_ada).reshape(batch, n_sub * N_MOD, d)
    row = lambda vec: vec.reshape(1, -1)

    x2 = _ffn(x2, mod3, row(norm_pre[0]), row(norm_post[0]), f1g.astype(BF16), f1u.astype(BF16),
              f1d.astype(BF16), sub=0, seq=seq)

    n_lora = w_in.shape[1] - pool_width - 3 * rw
    assert (3 * rw) % col_tile == 0 and n_lora <= col_tile and pool_width == col_tile and rw % LANES == 0
    w_in_p = jnp.concatenate([w_in[:, pool_width:pool_width + 3 * rw],
                              _pad_cols(w_in[:, pool_width + 3 * rw:], col_tile),
                              w_in[:, :pool_width]], axis=1).astype(BF16)
    mu_p = jnp.concatenate([mu_shift, jnp.zeros((col_tile - n_lora + pool_width,), F32)])
    p_all = _inproj(x2, mod3, row(norm_pre[1]), w_in_p, row(mu_p), seq=seq, tm=1024,
                    tn=w_in_p.shape[1] // 4)
    lora_block = 3 * rw // col_tile

    y_pool = _pool(p_all, lora_block + 1, pool_w.astype(BF16), row(pool_scale), seq=seq)

    n_w, n_a, n_g = w2.shape[0], a2.shape[0], g2.shape[0]
    assert n_w + n_a == LANES
    w2p = jnp.pad(w2, ((0, n_a), (0, 0))).astype(BF16)
    a2p = jnp.pad(a2, ((n_w, 0), (0, 0))).astype(BF16)
    g2p = jnp.pad(g2, ((0, -n_g % LANES), (0, 0))).astype(BF16)
    ld, a, g = _lora(p_all, lora_block, col_tile, w2p, a2p, g2p, row(w0), row(a0))

    y_rwkv = _scan(p_all, ld, a, g, row(k_k), row(k_a), row(r_k), row(lnx_w), row(lnx_b),
                   batch=batch, seq=seq, r_block=0, k_block=rw // LANES, v_block=2 * rw // LANES)

    x2 = _outproj(x2, y_pool, y_rwkv, mod3, row(norm_post[1]), w_out[:pool_width].astype(BF16),
                  w_out[pool_width:].astype(BF16), seq=seq)

    x2 = _ffn(x2, mod3, row(norm_pre[2]), row(norm_post[2]), f2g.astype(BF16), f2u.astype(BF16),
              f2d.astype(BF16), sub=2, seq=seq)
    return x2


def kernel(x, c, w_ada, b_ada, norm_pre, norm_post, ffn1_w_gate, ffn1_w_up, ffn1_w_down, w_in, mu_shift,
           pool_w, pool_scale, w0, w2, a0, a2, g2, k_k, k_a, r_k, lnx_w, lnx_b, w_out, ffn2_w_gate,
           ffn2_w_up, ffn2_w_down):
    batch, seq, d = x.shape
    x2 = x.reshape(batch * seq, d)
    for l in range(w_ada.shape[0]):
        x2 = _layer(x2, c, batch, seq, w_ada[l], b_ada[l], norm_pre[l], norm_post[l], ffn1_w_gate[l],
                    ffn1_w_up[l], ffn1_w_down[l], w_in[l], mu_shift[l], pool_w[l], pool_scale[l], w0[l],
                    w2[l], a0[l], a2[l], g2[l], k_k[l], k_a[l], r_k[l], lnx_w[l], lnx_b[l], w_out[l],
                    ffn2_w_gate[l], ffn2_w_up[l], ffn2_w_down[l])
    return x2.reshape(batch, seq, d)
```

```python
import functools

import jax
import jax.numpy as jnp
from jax import lax
from jax.experimental import pallas as pl
from jax.experimental.pallas import tpu as pltpu

F32 = jnp.float32
BF16 = jnp.bfloat16

NORM_EPS = 1e-6
HEAD_SIZE = 64
LN_X_EPS = 1e-5 * HEAD_SIZE
POOL_WINDOWS = (2, 4, 8, 16)
POOL_GROUP = 128
MACARON_WEIGHT = 0.5
N_MOD = 3

LANES = 128
CHUNK = 64
CHUNK_LAG = 8
ROW_BLOCK = 16
VMEM_LIMIT_MIB = 56


def _cparams(sem, vmem_mib=VMEM_LIMIT_MIB):
    return pltpu.CompilerParams(dimension_semantics=sem, vmem_limit_bytes=vmem_mib * 1024 * 1024)


def _sigmoid(z):
    return 1.0 / (1.0 + jnp.exp(-z))


def _row_blocks(n_rows, fn):
    for b in range(n_rows // ROW_BLOCK):
        fn(pl.ds(b * ROW_BLOCK, ROW_BLOCK))


def _norm_mod_into(h_ref, x_ref, gain_ref, mod_ref, sub):
    shift = mod_ref[0, N_MOD * sub:N_MOD * sub + 1, :]
    scale = mod_ref[0, N_MOD * sub + 1:N_MOD * sub + 2, :]
    mult = gain_ref[...] * (1.0 + scale)
    blocks = []

    def block(rows):
        x = x_ref[rows, :]
        inv = lax.rsqrt(jnp.mean(x * x, axis=-1, keepdims=True) + NORM_EPS)
        blocks.append((x * inv * mult + shift).astype(BF16))
        h_ref[rows, :] = blocks[-1]

    _row_blocks(x_ref.shape[0], block)
    return jnp.concatenate(blocks, axis=0)


def _residual_into(o_ref, x_ref, gain_ref, mod_ref, sub, weight, y_all=None):
    gate = mod_ref[0, N_MOD * sub + 2:N_MOD * sub + 3, :]
    mult = gain_ref[...] * (weight * (1.0 + gate))

    def block(rows):
        y = o_ref[rows, :] if y_all is None else y_all[rows.start:rows.start + rows.size, :]
        inv = lax.rsqrt(jnp.mean(y * y, axis=-1, keepdims=True) + NORM_EPS)
        o_ref[rows, :] = x_ref[rows, :] + y * inv * mult

    _row_blocks(o_ref.shape[0], block)


def _bdot(a, b):
    return jnp.dot(a.astype(BF16), b.astype(BF16), preferred_element_type=F32)


def _ada_kernel(c_ref, w_ref, b_ref, o_ref):
    c = c_ref[...]
    s = c * _sigmoid(c)
    o_ref[...] = jnp.dot(s, w_ref[...], preferred_element_type=F32,
                         precision=lax.Precision.HIGHEST) + b_ref[...]


def _ada(c, w, b, tn=1024):
    bsz, d = c.shape
    n = w.shape[1]
    return pl.pallas_call(
        _ada_kernel,
        grid=(n // tn,),
        in_specs=[pl.BlockSpec((bsz, d), lambda j: (0, 0)),
                  pl.BlockSpec((d, tn), lambda j: (0, j)),
                  pl.BlockSpec((1, tn), lambda j: (0, j))],
        out_specs=pl.BlockSpec((bsz, tn), lambda j: (0, j)),
        out_shape=jax.ShapeDtypeStruct((bsz, n), F32),
        compiler_params=_cparams(("arbitrary",)),
        name="ada",
    )(c, w, b.reshape(1, n))


def _ffn_kernel(x_ref, mod_ref, gpre_ref, gpost_ref, wg_ref, wu_ref, wd_ref, o_ref, h_ref, *, sub):
    j = pl.program_id(1)
    last = pl.num_programs(1) - 1

    def chunk(h):
        g = jnp.dot(h, wg_ref[...], preferred_element_type=F32)
        u = jnp.dot(h, wu_ref[...], preferred_element_type=F32)
        act = (g * _sigmoid(g) * u).astype(BF16)
        return jnp.dot(act, wd_ref[...], preferred_element_type=F32)

    @pl.when(j == 0)
    def _():
        o_ref[...] = chunk(_norm_mod_into(h_ref, x_ref, gpre_ref, mod_ref, sub))

    @pl.when((j > 0) & (j < last))
    def _():
        o_ref[...] += chunk(h_ref[...])

    @pl.when(j == last)
    def _():
        _residual_into(o_ref, x_ref, gpost_ref, mod_ref, sub, MACARON_WEIGHT,
                       y_all=o_ref[...] + chunk(h_ref[...]))


def _ffn(x2, mod3, gpre, gpost, wg, wu, wd, *, sub, seq, tm=1024, tf=512):
    t, d = x2.shape
    f = wg.shape[1]
    tps = seq // tm
    spill_mib = 5
    vmem_mib = (2 * 2 * tm * d * 4 + tm * d * 2 + 3 * 2 * d * tf * 2 + tm * tf * (4 + 4 + 2)) // 2**20 + spill_mib
    return pl.pallas_call(
        functools.partial(_ffn_kernel, sub=sub),
        grid=(t // tm, f // tf),
        in_specs=[pl.BlockSpec((tm, d), lambda i, j: (i, 0)),
                  pl.BlockSpec((1,) + mod3.shape[1:], lambda i, j: (i // tps, 0, 0)),
                  pl.BlockSpec((1, d), lambda i, j: (0, 0)),
                  pl.BlockSpec((1, d), lambda i, j: (0, 0)),
                  pl.BlockSpec((d, tf), lambda i, j: (0, j)),
                  pl.BlockSpec((d, tf), lambda i, j: (0, j)),
                  pl.BlockSpec((tf, d), lambda i, j: (j, 0))],
        out_specs=pl.BlockSpec((tm, d), lambda i, j: (i, 0)),
        out_shape=jax.ShapeDtypeStruct((t, d), F32),
        scratch_shapes=[pltpu.VMEM((tm, d), BF16)],
        compiler_params=_cparams(("arbitrary", "arbitrary"), vmem_mib),
        name=f"ffn{sub}",
    )(x2, mod3, gpre, gpost, wg, wu, wd)


def _inproj_kernel(x_ref, mod_ref, gpre_ref, w_ref, mu_ref, o_ref, h_ref, carry_ref, *, tps):
    i = pl.program_id(0)
    j = pl.program_id(1)

    def project(h):
        res = jnp.dot(h, w_ref[...], preferred_element_type=F32)
        tm, tn = res.shape
        carried = carry_ref[j][0:1, :]
        first = jnp.where(i % tps == 0, jnp.zeros_like(carried), carried)
        row = lax.broadcasted_iota(jnp.int32, (tm, 1), 0)
        prev = jnp.where(row == 0, first, pltpu.roll(res, 1, 0))
        carry_ref[j] = jnp.broadcast_to(res[tm - 1:tm, :], (8, tn))
        o_ref[...] = res + mu_ref[...] * (prev - res)

    @pl.when(j == 0)
    def _():
        project(_norm_mod_into(h_ref, x_ref, gpre_ref, mod_ref, 1))

    @pl.when(j > 0)
    def _():
        project(h_ref[...])


def _inproj(x2, mod3, gpre, w, mu, *, seq, tm=512, tn=512):
    t, d = x2.shape
    n = w.shape[1]
    tps = seq // tm
    return pl.pallas_call(
        functools.partial(_inproj_kernel, tps=tps),
        grid=(t // tm, n // tn),
        in_specs=[pl.BlockSpec((tm, d), lambda i, j: (i, 0)),
                  pl.BlockSpec((1,) + mod3.shape[1:], lambda i, j: (i // tps, 0, 0)),
                  pl.BlockSpec((1, d), lambda i, j: (0, 0)),
                  pl.BlockSpec((d, tn), lambda i, j: (0, j)),
                  pl.BlockSpec((1, tn), lambda i, j: (0, j))],
        out_specs=pl.BlockSpec((tm, tn), lambda i, j: (i, j)),
        out_shape=jax.ShapeDtypeStruct((t, n), F32),
        scratch_shapes=[pltpu.VMEM((tm, d), BF16), pltpu.VMEM((n // tn, 8, tn), F32)],
        compiler_params=_cparams(("arbitrary", "arbitrary")),
        name="inproj",
    )(x2, mod3, gpre, w, mu)


def _lora_kernel(p_ref, w2_ref, a2_ref, g2_ref, w0_ref, a0_ref, ld_ref, a_ref, g_ref):
    wa = p_ref[:, :LANES]
    gx = p_ref[:, LANES:LANES + g2_ref.shape[0]]
    z = w0_ref[...] + _bdot(jnp.tanh(wa), w2_ref[...])
    w_log = -(jnp.maximum(-z, 0.0) + jnp.log(1.0 + jnp.exp(-jnp.abs(z)))) - 0.5
    ld_ref[...] = -jnp.exp(w_log)
    a_ref[...] = _sigmoid(a0_ref[...] + _bdot(wa, a2_ref[...]))
    g_ref[...] = _bdot(_sigmoid(gx), g2_ref[...])


def _lora(p_all, col_block, col_tile, w2p, a2p, g2p, w0, a0, *, tm=512):
    t = p_all.shape[0]
    r = w2p.shape[1]
    full = lambda arr: pl.BlockSpec(arr.shape, lambda i: (0, 0))
    out = jax.ShapeDtypeStruct((t, r), F32)
    return pl.pallas_call(
        _lora_kernel,
        grid=(t // tm,),
        in_specs=[pl.BlockSpec((tm, col_tile), lambda i: (i, col_block)),
                  full(w2p), full(a2p), full(g2p), full(w0), full(a0)],
        out_specs=[pl.BlockSpec((tm, r), lambda i: (i, 0))] * 3,
        out_shape=[out, out, out],
        compiler_params=_cparams(("arbitrary",)),
        name="lora",
    )(p_all, w2p, a2p, g2p, w0, a0)


def _scan_kernel(r_ref, k_ref, v_ref, ld_ref, a_ref, g_ref, kkw_ref, ka_ref, rk_ref, lw_ref, lb_ref,
                 o_ref, h_ref):
    C = CHUNK
    P = LANES
    N = HEAD_SIZE

    @pl.when(pl.program_id(2) == 0)
    def _():
        h_ref[...] = jnp.zeros_like(h_ref)

    lane = lax.broadcasted_iota(jnp.int32, (1, P), 1)
    in_a = lane < N
    ri = lax.broadcasted_iota(jnp.int32, (2 * C, 2 * C), 0)
    ci = lax.broadcasted_iota(jnp.int32, (2 * C, 2 * C), 1)

    def same(b):
        sh = b.bit_length() - 1
        return (ri >> sh) == (ci >> sh)

    strict = (ri > ci) & same(C)
    incl = (ri >= ci) & same(C)
    eye = ri == ci
    blk16 = strict & same(16)
    blk32 = strict & same(32) & jnp.logical_not(same(16))
    blk64 = strict & jnp.logical_not(same(32))
    eye_f = eye.astype(F32)
    tri3 = (lax.broadcasted_iota(jnp.int32, (C, 3 * C), 0)
            >= (lax.broadcasted_iota(jnp.int32, (C, 3 * C), 1) & (C - 1))).astype(BF16)

    def stack(x):
        return jnp.concatenate([jnp.where(in_a, x, 0.0), jnp.where(in_a, 0.0, x)], axis=0)

    def headsum(x):
        first = jnp.sum(jnp.where(in_a, x, 0.0), axis=-1, keepdims=True)
        second = jnp.sum(jnp.where(in_a, 0.0, x), axis=-1, keepdims=True)
        return jnp.where(in_a, first, second)

    def pair_chunk(c, q):
        sl = pl.ds(c * C, C)
        ln = pl.ds(q * P, P)
        kkw = kkw_ref[:, ln]
        ka = ka_ref[:, ln]
        rk = rk_ref[:, ln]
        lw = lw_ref[:, ln]
        lb = lb_ref[:, ln]
        r = r_ref[sl, ln]
        k = k_ref[sl, ln]
        v = v_ref[sl, ln]
        ld = ld_ref[sl, ln]
        a = a_ref[sl, ln]

        kkraw = k * kkw
        kmod = k * (1.0 + (a - 1.0) * ka)
        sums = headsum(jnp.concatenate([kkraw * kkraw, r * kmod * rk], axis=0))
        ld_hi = ld.astype(BF16)
        rem = ld - ld_hi.astype(F32)
        ld_mid = rem.astype(BF16)
        ld_lo = (rem - ld_mid.astype(F32)).astype(BF16)
        cum = jnp.dot(tri3, jnp.concatenate([ld_hi, ld_mid, ld_lo], axis=0), preferred_element_type=F32)
        yield
        kk = kkraw / jnp.maximum(jnp.sqrt(sums[:C]), 1e-12)
        bonus = sums[C:]
        cum_last = cum[C - 1:C, :]

        inv_g = jnp.exp(-cum)
        to_end = jnp.exp(cum_last - cum)
        kb = kk * a
        rt = r * jnp.exp(cum)
        at = -kk * jnp.exp(cum - ld)
        bt = kb * inv_g
        kt = kmod * inv_g
        g_end = jnp.exp(cum_last)

        s_v = stack(v).astype(BF16)
        lhs = jnp.concatenate([stack(at), stack(rt)], axis=0).astype(BF16)
        rhs = jnp.concatenate([stack(bt), stack(kt)], axis=0).astype(BF16)
        scores = lax.dot_general(lhs, rhs, (((1,), (1,)), ((), ())), preferred_element_type=F32)
        s_at = lhs[:2 * C]
        s_rt = lhs[2 * C:]
        yield
        s_ab = scores[:2 * C, :2 * C]
        s_ak = jnp.where(strict, scores[:2 * C, 2 * C:], 0.0)
        s_rb = jnp.where(incl, scores[2 * C:, :2 * C], 0.0)
        s_rk = jnp.where(incl, scores[2 * C:, 2 * C:], 0.0)

        x1 = jnp.where(blk16, s_ab, 0.0)
        x2 = _bdot(x1, x1)
        yield
        x4 = _bdot(x2, x2)
        tinv = eye_f + x1
        tinv = tinv + _bdot(tinv, x2)
        yield
        x8 = _bdot(x4, x4)
        tinv = tinv + _bdot(tinv, x4)
        yield
        tinv = tinv + _bdot(tinv, x8)
        yield
        for blk in (blk32, blk64):
            off = jnp.where(blk, s_ab, 0.0)
            part = _bdot(tinv, off)
            yield
            tinv = tinv + _bdot(part, tinv)
            yield

        h = h_ref[q]
        h_b = h.astype(BF16)
        rhs_sa = jnp.dot(jnp.concatenate([s_at, s_ak.astype(BF16)], axis=1),
                         jnp.concatenate([h_b, s_v], axis=0), preferred_element_type=F32)
        yield
        s_sa = _bdot(tinv, rhs_sa)
        yield
        sa_v = jnp.concatenate([s_sa.astype(BF16), s_v], axis=0)
        y_s = jnp.dot(jnp.concatenate([s_rt, s_rb.astype(BF16), s_rk.astype(BF16)], axis=1),
                      jnp.concatenate([h_b, sa_v], axis=0), preferred_element_type=F32)
        upd_l = jnp.concatenate([stack(kb * to_end), stack(kmod * to_end)], axis=0).astype(BF16)
        upd = lax.dot_general(upd_l, sa_v, (((0,), (0,)), ((), ())), preferred_element_type=F32)
        yield
        y = y_s[:C] + y_s[C:]
        g_col = jnp.sum(jnp.where(eye, g_end, 0.0), axis=1, keepdims=True)
        h_ref[q] = h * g_col + upd

        mean = headsum(y) * (1.0 / N)
        yield
        dev = y - mean
        var = headsum(dev * dev) * (1.0 / N)
        yield
        yn = dev * lax.rsqrt(var + LN_X_EPS) * lw + lb
        o_ref[sl, ln] = ((yn + bonus * v) * g_ref[sl, ln]).astype(o_ref.dtype)

    n_chunks = r_ref.shape[0] // C
    chains = [[pair_chunk(c, q) for q in range(r_ref.shape[1] // P)] for c in range(n_chunks)]
    sweep = 0
    while any(chains):
        for c in range(min(n_chunks, sweep // CHUNK_LAG + 1)):
            advanced = []
            for gen in chains[c]:
                try:
                    next(gen)
                    advanced.append(gen)
                except StopIteration:
                    pass
            chains[c] = advanced
        sweep += 1


def _scan(p_all, ld, a, g, kkw, ka, rk, lw, lb, *, batch, seq, r_block, k_block, v_block, tb=4 * CHUNK, pairs=12):
    t, rw = ld.shape
    width = pairs * LANES
    tpb = seq // tb
    tok = lambda blk0: pl.BlockSpec((tb, width), lambda b, p, s: (b * tpb + s, blk0 // pairs + p))
    par = pl.BlockSpec((1, width), lambda b, p, s: (0, p))
    return pl.pallas_call(
        _scan_kernel,
        grid=(batch, rw // width, tpb),
        in_specs=[tok(r_block), tok(k_block), tok(v_block), tok(0), tok(0), tok(0),
                  par, par, par, par, par],
        out_specs=tok(0),
        out_shape=jax.ShapeDtypeStruct((t, rw), BF16),
        scratch_shapes=[pltpu.VMEM((pairs, LANES, LANES), F32)],
        compiler_params=_cparams(("arbitrary", "arbitrary", "arbitrary")),
        name="scan",
    )(p_all, p_all, p_all, ld, a, g, kkw, ka, rk, lw, lb)


def _pooled(u_ref, pw_ref, ps_ref, ext_ref, t0):
    tm = u_ref.shape[0]
    pad = max(POOL_WINDOWS)
    ext_ref[pad:pad + tm, :] = u_ref[...]
    tpos = (t0 + 1 + lax.broadcasted_iota(jnp.int32, (tm, 1), 0)).astype(F32)
    groups = []
    for gi, win in enumerate(POOL_WINDOWS):
        lo = gi * POOL_GROUP
        acc = ext_ref[pad:pad + tm, lo:lo + POOL_GROUP]
        u_g = acc
        for dlt in range(1, win):
            acc = acc + ext_ref[pad - dlt:pad - dlt + tm, lo:lo + POOL_GROUP]
        pooled = acc / jnp.minimum(tpos, float(win)) - u_g
        groups.append((_bdot(pooled, pw_ref[gi]) * ps_ref[:, lo:lo + POOL_GROUP]).astype(BF16))
    ext_ref[0:pad, :] = ext_ref[tm:tm + pad, :]
    return jnp.concatenate(groups, axis=1)


def _outproj_kernel(x_ref, u_ref, yr_ref, mod_ref, gpost_ref, pw_ref, ps_ref, wo_ref, o_ref, ext_ref, *, tps):
    i = pl.program_id(0)
    tm = x_ref.shape[0]
    n_pool = u_ref.shape[1]

    @pl.when(i % tps == 0)
    def _():
        ext_ref[0:max(POOL_WINDOWS), :] = jnp.zeros((max(POOL_WINDOWS), n_pool), F32)

    y = jnp.dot(yr_ref[...], wo_ref[n_pool:, :], preferred_element_type=F32)
    y_pool = _pooled(u_ref, pw_ref, ps_ref, ext_ref, (i % tps) * tm)
    y = y + jnp.dot(y_pool, wo_ref[:n_pool, :], preferred_element_type=F32)
    _residual_into(o_ref, x_ref, gpost_ref, mod_ref, 1, 1.0, y_all=y)


def _outproj(x2, p_all, pool_block, y_rwkv, mod3, gpost, pool_w, pool_scale, w_out, *, seq, tm=512):
    t, d = x2.shape
    n_pool = pool_scale.shape[1]
    tps = seq // tm
    return pl.pallas_call(
        functools.partial(_outproj_kernel, tps=tps),
        grid=(t // tm,),
        in_specs=[pl.BlockSpec((tm, d), lambda i: (i, 0)),
                  pl.BlockSpec((tm, n_pool), lambda i: (i, pool_block)),
                  pl.BlockSpec((tm, y_rwkv.shape[1]), lambda i: (i, 0)),
                  pl.BlockSpec((1,) + mod3.shape[1:], lambda i: (i // tps, 0, 0)),
                  pl.BlockSpec((1, d), lambda i: (0, 0)),
                  pl.BlockSpec(pool_w.shape, lambda i: (0, 0, 0)),
                  pl.BlockSpec((1, n_pool), lambda i: (0, 0)),
                  pl.BlockSpec(w_out.shape, lambda i: (0, 0))],
        out_specs=pl.BlockSpec((tm, d), lambda i: (i, 0)),
        out_shape=jax.ShapeDtypeStruct((t, d), F32),
        scratch_shapes=[pltpu.VMEM((tm + max(POOL_WINDOWS), n_pool), F32)],
        compiler_params=_cparams(("arbitrary",)),
        name="outproj",
    )(x2, p_all, y_rwkv, mod3, gpost, pool_w, pool_scale, w_out)


def _pad_cols(w, n):
    return jnp.pad(w, ((0, 0), (0, n - w.shape[1])))


def _layer(x2, c, batch, seq, w_ada, b_ada, norm_pre, norm_post, f1g, f1u, f1d, w_in, mu_shift, pool_w,
           pool_scale, w0, w2, a0, a2, g2, k_k, k_a, r_k, lnx_w, lnx_b, w_out, f2g, f2u, f2d):
    d = x2.shape[1]
    pool_width = pool_scale.shape[0]
    rw = w0.shape[0]
    n_sub = norm_pre.shape[0]
    col_tile = 512

    mod3 = _ada(c, w_ada, b_ada).reshape(batch, n_sub * N_MOD, d)
    row = lambda vec: vec.reshape(1, -1)

    x2 = _ffn(x2, mod3, row(norm_pre[0]), row(norm_post[0]), f1g.astype(BF16), f1u.astype(BF16),
              f1d.astype(BF16), sub=0, seq=seq)

    n_lora = w_in.shape[1] - pool_width - 3 * rw
    assert (3 * rw) % col_tile == 0 and n_lora <= col_tile and pool_width == col_tile and rw % LANES == 0
    w_in_p = jnp.concatenate([w_in[:, pool_width:pool_width + 3 * rw],
                              _pad_cols(w_in[:, pool_width + 3 * rw:], col_tile),
                              w_in[:, :pool_width]], axis=1).astype(BF16)
    mu_p = jnp.concatenate([mu_shift, jnp.zeros((col_tile - n_lora + pool_width,), F32)])
    p_all = _inproj(x2, mod3, row(norm_pre[1]), w_in_p, row(mu_p), seq=seq, tm=1024,
                    tn=w_in_p.shape[1] // 4)
    lora_block = 3 * rw // col_tile

    n_w, n_a, n_g = w2.shape[0], a2.shape[0], g2.shape[0]
    assert n_w + n_a == LANES
    w2p = jnp.pad(w2, ((0, n_a), (0, 0))).astype(BF16)
    a2p = jnp.pad(a2, ((n_w, 0), (0, 0))).astype(BF16)
    g2p = jnp.pad(g2, ((0, -n_g % LANES), (0, 0))).astype(BF16)
    ld, a, g = _lora(p_all, lora_block, col_tile, w2p, a2p, g2p, row(w0), row(a0))

    y_rwkv = _scan(p_all, ld, a, g, row(k_k), row(k_a), row(r_k), row(lnx_w), row(lnx_b),
                   batch=batch, seq=seq, r_block=0, k_block=rw // LANES, v_block=2 * rw // LANES)

    x2 = _outproj(x2, p_all, lora_block + 1, y_rwkv, mod3, row(norm_post[1]), pool_w.astype(BF16),
                  row(pool_scale), w_out.astype(BF16), seq=seq)

    x2 = _ffn(x2, mod3, row(norm_pre[2]), row(norm_post[2]), f2g.astype(BF16), f2u.astype(BF16),
              f2d.astype(BF16), sub=2, seq=seq)
    return x2


def kernel(x, c, w_ada, b_ada, norm_pre, norm_post, ffn1_w_gate, ffn1_w_up, ffn1_w_down, w_in, mu_shift,
           pool_w, pool_scale, w0, w2, a0, a2, g2, k_k, k_a, r_k, lnx_w, lnx_b, w_out, ffn2_w_gate,
           ffn2_w_up, ffn2_w_down):
    batch, seq, d = x.shape
    x2 = x.reshape(batch * seq, d)
    for l in range(w_ada.shape[0]):
        x2 = _layer(x2, c, batch, seq, w_ada[l], b_ada[l], norm_pre[l], norm_post[l], ffn1_w_gate[l],
                    ffn1_w_up[l], ffn1_w_down[l], w_in[l], mu_shift[l], pool_w[l], pool_scale[l], w0[l],
                    w2[l], a0[l], a2[l], g2[l], k_k[l], k_a[l], r_k[l], lnx_w[l], lnx_b[l], w_out[l],
                    ffn2_w_gate[l], ffn2_w_up[l], ffn2_w_down[l])
    return x2.reshape(batch, seq, d)
```

```python
import functools

import jax
import jax.numpy as jnp
from jax import lax
from jax.experimental import pallas as pl
from jax.experimental.pallas import tpu as pltpu

F32 = jnp.float32
BF16 = jnp.bfloat16

NORM_EPS = 1e-6
HEAD_SIZE = 64
LN_X_EPS = 1e-5 * HEAD_SIZE
POOL_WINDOWS = (2, 4, 8, 16)
POOL_GROUP = 128
MACARON_WEIGHT = 0.5
N_MOD = 3

LANES = 128
CHUNK = 64
CHUNK_LAG = 8
ROW_BLOCK = 16
VMEM_LIMIT_MIB = 56


def _cparams(sem, vmem_mib=VMEM_LIMIT_MIB):
    return pltpu.CompilerParams(dimension_semantics=sem, vmem_limit_bytes=vmem_mib * 1024 * 1024)


def _sigmoid(z):
    return 1.0 / (1.0 + jnp.exp(-z))


def _row_blocks(n_rows, fn):
    for b in range(n_rows // ROW_BLOCK):
        fn(pl.ds(b * ROW_BLOCK, ROW_BLOCK))


def _norm_mod_into(h_ref, x_ref, gain_ref, mod_ref, sub):
    shift = mod_ref[0, N_MOD * sub:N_MOD * sub + 1, :]
    scale = mod_ref[0, N_MOD * sub + 1:N_MOD * sub + 2, :]
    mult = gain_ref[...] * (1.0 + scale)
    blocks = []

    def block(rows):
        x = x_ref[rows, :]
        inv = lax.rsqrt(jnp.mean(x * x, axis=-1, keepdims=True) + NORM_EPS)
        blocks.append((x * inv * mult + shift).astype(BF16))
        h_ref[rows, :] = blocks[-1]

    _row_blocks(x_ref.shape[0], block)
    return jnp.concatenate(blocks, axis=0)


def _residual_into(o_ref, x_ref, gain_ref, mod_ref, sub, weight, y_all):
    gate = mod_ref[0, N_MOD * sub + 2:N_MOD * sub + 3, :]
    mult = gain_ref[...] * (weight * (1.0 + gate))

    def block(rows):
        y = y_all[rows.start:rows.start + rows.size, :]
        inv = lax.rsqrt(jnp.mean(y * y, axis=-1, keepdims=True) + NORM_EPS)
        o_ref[rows, :] = x_ref[rows, :] + y * inv * mult

    _row_blocks(o_ref.shape[0], block)


def _bdot(a, b):
    return jnp.dot(a.astype(BF16), b.astype(BF16), preferred_element_type=F32)


def _ada_kernel(c_ref, w_ref, b_ref, o_ref):
    c = c_ref[...]
    s = c * _sigmoid(c)
    o_ref[...] = jnp.dot(s, w_ref[...], preferred_element_type=F32,
                         precision=lax.Precision.HIGHEST) + b_ref[...]


def _ada(c, w, b, tn=1024):
    bsz, d = c.shape
    n = w.shape[1]
    return pl.pallas_call(
        _ada_kernel,
        grid=(n // tn,),
        in_specs=[pl.BlockSpec((bsz, d), lambda j: (0, 0)),
                  pl.BlockSpec((d, tn), lambda j: (0, j)),
                  pl.BlockSpec((1, tn), lambda j: (0, j))],
        out_specs=pl.BlockSpec((bsz, tn), lambda j: (0, j)),
        out_shape=jax.ShapeDtypeStruct((bsz, n), F32),
        compiler_params=_cparams(("arbitrary",)),
        name="ada",
    )(c, w, b.reshape(1, n))


def _ffn_kernel(x_ref, mod_ref, gpre_ref, gpost_ref, wg_ref, wu_ref, wd_ref, o_ref, h_ref, *, sub):
    j = pl.program_id(1)
    last = pl.num_programs(1) - 1

    def chunk(h):
        g = jnp.dot(h, wg_ref[...], preferred_element_type=F32)
        u = jnp.dot(h, wu_ref[...], preferred_element_type=F32)
        act = (g * _sigmoid(g) * u).astype(BF16)
        return jnp.dot(act, wd_ref[...], preferred_element_type=F32)

    @pl.when(j == 0)
    def _():
        o_ref[...] = chunk(_norm_mod_into(h_ref, x_ref, gpre_ref, mod_ref, sub))

    @pl.when((j > 0) & (j < last))
    def _():
        o_ref[...] += chunk(h_ref[...])

    @pl.when(j == last)
    def _():
        _residual_into(o_ref, x_ref, gpost_ref, mod_ref, sub, MACARON_WEIGHT,
                       y_all=o_ref[...] + chunk(h_ref[...]))


def _ffn(x2, mod3, gpre, gpost, wg, wu, wd, *, sub, seq, tm=1024, tf=512):
    t, d = x2.shape
    f = wg.shape[1]
    tps = seq // tm
    assert seq % tm == 0 and f % tf == 0 and f // tf >= 2
    spill_mib = 5
    vmem_mib = (2 * 2 * tm * d * 4 + tm * d * 2 + 3 * 2 * d * tf * 2 + tm * tf * (4 + 4 + 2)) // 2**20 + spill_mib
    return pl.pallas_call(
        functools.partial(_ffn_kernel, sub=sub),
        grid=(t // tm, f // tf),
        in_specs=[pl.BlockSpec((tm, d), lambda i, j: (i, 0)),
                  pl.BlockSpec((1,) + mod3.shape[1:], lambda i, j: (i // tps, 0, 0)),
                  pl.BlockSpec((1, d), lambda i, j: (0, 0)),
                  pl.BlockSpec((1, d), lambda i, j: (0, 0)),
                  pl.BlockSpec((d, tf), lambda i, j: (0, j)),
                  pl.BlockSpec((d, tf), lambda i, j: (0, j)),
                  pl.BlockSpec((tf, d), lambda i, j: (j, 0))],
        out_specs=pl.BlockSpec((tm, d), lambda i, j: (i, 0)),
        out_shape=jax.ShapeDtypeStruct((t, d), F32),
        scratch_shapes=[pltpu.VMEM((tm, d), BF16)],
        compiler_params=_cparams(("arbitrary", "arbitrary"), vmem_mib),
        name=f"ffn{sub}",
    )(x2, mod3, gpre, gpost, wg, wu, wd)


def _inproj_kernel(x_ref, mod_ref, gpre_ref, w_ref, mu_ref, o_ref, h_ref, carry_ref, *, tps):
    i = pl.program_id(0)
    j = pl.program_id(1)

    def project(h):
        res = jnp.dot(h, w_ref[...], preferred_element_type=F32)
        tm, tn = res.shape
        carried = carry_ref[j][0:1, :]
        first = jnp.where(i % tps == 0, jnp.zeros_like(carried), carried)
        row = lax.broadcasted_iota(jnp.int32, (tm, 1), 0)
        prev = jnp.where(row == 0, first, pltpu.roll(res, 1, 0))
        carry_ref[j] = jnp.broadcast_to(res[tm - 1:tm, :], (8, tn))
        o_ref[...] = res + mu_ref[...] * (prev - res)

    @pl.when(j == 0)
    def _():
        project(_norm_mod_into(h_ref, x_ref, gpre_ref, mod_ref, 1))

    @pl.when(j > 0)
    def _():
        project(h_ref[...])


def _inproj(x2, mod3, gpre, w, mu, *, seq, tm=512, tn=512):
    t, d = x2.shape
    n = w.shape[1]
    tps = seq // tm
    assert seq % tm == 0 and n % tn == 0
    return pl.pallas_call(
        functools.partial(_inproj_kernel, tps=tps),
        grid=(t // tm, n // tn),
        in_specs=[pl.BlockSpec((tm, d), lambda i, j: (i, 0)),
                  pl.BlockSpec((1,) + mod3.shape[1:], lambda i, j: (i // tps, 0, 0)),
                  pl.BlockSpec((1, d), lambda i, j: (0, 0)),
                  pl.BlockSpec((d, tn), lambda i, j: (0, j)),
                  pl.BlockSpec((1, tn), lambda i, j: (0, j))],
        out_specs=pl.BlockSpec((tm, tn), lambda i, j: (i, j)),
        out_shape=jax.ShapeDtypeStruct((t, n), F32),
        scratch_shapes=[pltpu.VMEM((tm, d), BF16), pltpu.VMEM((n // tn, 8, tn), F32)],
        compiler_params=_cparams(("arbitrary", "arbitrary")),
        name="inproj",
    )(x2, mod3, gpre, w, mu)


def _lora_kernel(p_ref, w2_ref, a2_ref, g2_ref, w0_ref, a0_ref, ld_ref, a_ref, g_ref):
    wa = p_ref[:, :LANES]
    gx = p_ref[:, LANES:LANES + g2_ref.shape[0]]
    z = w0_ref[...] + _bdot(jnp.tanh(wa), w2_ref[...])
    w_log = -(jnp.maximum(-z, 0.0) + jnp.log(1.0 + jnp.exp(-jnp.abs(z)))) - 0.5
    ld_ref[...] = -jnp.exp(w_log)
    a_ref[...] = _sigmoid(a0_ref[...] + _bdot(wa, a2_ref[...]))
    g_ref[...] = _bdot(_sigmoid(gx), g2_ref[...])


def _lora(p_all, col_block, col_tile, w2p, a2p, g2p, w0, a0, *, tm=512):
    t = p_all.shape[0]
    r = w2p.shape[1]
    assert t % tm == 0
    full = lambda arr: pl.BlockSpec(arr.shape, lambda i: (0, 0))
    out = jax.ShapeDtypeStruct((t, r), F32)
    return pl.pallas_call(
        _lora_kernel,
        grid=(t // tm,),
        in_specs=[pl.BlockSpec((tm, col_tile), lambda i: (i, col_block)),
                  full(w2p), full(a2p), full(g2p), full(w0), full(a0)],
        out_specs=[pl.BlockSpec((tm, r), lambda i: (i, 0))] * 3,
        out_shape=[out, out, out],
        compiler_params=_cparams(("arbitrary",)),
        name="lora",
    )(p_all, w2p, a2p, g2p, w0, a0)


def _scan_kernel(r_ref, k_ref, v_ref, ld_ref, a_ref, g_ref, kkw_ref, ka_ref, rk_ref, lw_ref, lb_ref,
                 o_ref, h_ref):
    C = CHUNK
    P = LANES
    N = HEAD_SIZE

    @pl.when(pl.program_id(2) == 0)
    def _():
        h_ref[...] = jnp.zeros_like(h_ref)

    lane = lax.broadcasted_iota(jnp.int32, (1, P), 1)
    in_a = lane < N
    ri = lax.broadcasted_iota(jnp.int32, (2 * C, 2 * C), 0)
    ci = lax.broadcasted_iota(jnp.int32, (2 * C, 2 * C), 1)

    def same(b):
        sh = b.bit_length() - 1
        return (ri >> sh) == (ci >> sh)

    strict = (ri > ci) & same(C)
    incl = (ri >= ci) & same(C)
    eye = ri == ci
    blk16 = strict & same(16)
    blk32 = strict & same(32) & jnp.logical_not(same(16))
    blk64 = strict & jnp.logical_not(same(32))
    eye_f = eye.astype(F32)
    tri3 = (lax.broadcasted_iota(jnp.int32, (C, 3 * C), 0)
            >= (lax.broadcasted_iota(jnp.int32, (C, 3 * C), 1) & (C - 1))).astype(BF16)

    def stack(x):
        return jnp.concatenate([jnp.where(in_a, x, 0.0), jnp.where(in_a, 0.0, x)], axis=0)

    def headsum(x):
        first = jnp.sum(jnp.where(in_a, x, 0.0), axis=-1, keepdims=True)
        second = jnp.sum(jnp.where(in_a, 0.0, x), axis=-1, keepdims=True)
        return jnp.where(in_a, first, second)

    def pair_chunk(c, q):
        sl = pl.ds(c * C, C)
        ln = pl.ds(q * P, P)
        kkw = kkw_ref[:, ln]
        ka = ka_ref[:, ln]
        rk = rk_ref[:, ln]
        lw = lw_ref[:, ln]
        lb = lb_ref[:, ln]
        r = r_ref[sl, ln]
        k = k_ref[sl, ln]
        v = v_ref[sl, ln]
        ld = ld_ref[sl, ln]
        a = a_ref[sl, ln]

        kkraw = k * kkw
        kmod = k * (1.0 + (a - 1.0) * ka)
        sums = headsum(jnp.concatenate([kkraw * kkraw, r * kmod * rk], axis=0))
        ld_hi = ld.astype(BF16)
        rem = ld - ld_hi.astype(F32)
        ld_mid = rem.astype(BF16)
        ld_lo = (rem - ld_mid.astype(F32)).astype(BF16)
        cum = jnp.dot(tri3, jnp.concatenate([ld_hi, ld_mid, ld_lo], axis=0), preferred_element_type=F32)
        yield
        kk = kkraw / jnp.maximum(jnp.sqrt(sums[:C]), 1e-12)
        bonus = sums[C:]
        cum_last = cum[C - 1:C, :]

        inv_g = jnp.exp(-cum)
        to_end = jnp.exp(cum_last - cum)
        kb = kk * a
        rt = r * jnp.exp(cum)
        at = -kk * jnp.exp(cum - ld)
        bt = kb * inv_g
        kt = kmod * inv_g
        g_end = jnp.exp(cum_last)

        s_v = stack(v).astype(BF16)
        lhs = jnp.concatenate([stack(at), stack(rt)], axis=0).astype(BF16)
        rhs = jnp.concatenate([stack(bt), stack(kt)], axis=0).astype(BF16)
        scores = lax.dot_general(lhs, rhs, (((1,), (1,)), ((), ())), preferred_element_type=F32)
        s_at = lhs[:2 * C]
        s_rt = lhs[2 * C:]
        yield
        s_ab = scores[:2 * C, :2 * C]
        s_ak = jnp.where(strict, scores[:2 * C, 2 * C:], 0.0)
        s_rb = jnp.where(incl, scores[2 * C:, :2 * C], 0.0)
        s_rk = jnp.where(incl, scores[2 * C:, 2 * C:], 0.0)

        x1 = jnp.where(blk16, s_ab, 0.0)
        x2 = _bdot(x1, x1)
        yield
        x4 = _bdot(x2, x2)
        tinv = eye_f + x1
        tinv = tinv + _bdot(tinv, x2)
        yield
        x8 = _bdot(x4, x4)
        tinv = tinv + _bdot(tinv, x4)
        yield
        tinv = tinv + _bdot(tinv, x8)
        yield
        for blk in (blk32, blk64):
            off = jnp.where(blk, s_ab, 0.0)
            part = _bdot(tinv, off)
            yield
            tinv = tinv + _bdot(part, tinv)
            yield

        h = h_ref[q]
        h_b = h.astype(BF16)
        rhs_sa = jnp.dot(jnp.concatenate([s_at, s_ak.astype(BF16)], axis=1),
                         jnp.concatenate([h_b, s_v], axis=0), preferred_element_type=F32)
        yield
        s_sa = _bdot(tinv, rhs_sa)
        yield
        sa_v = jnp.concatenate([s_sa.astype(BF16), s_v], axis=0)
        y_s = jnp.dot(jnp.concatenate([s_rt, s_rb.astype(BF16), s_rk.astype(BF16)], axis=1),
                      jnp.concatenate([h_b, sa_v], axis=0), preferred_element_type=F32)
        upd_l = jnp.concatenate([stack(kb * to_end), stack(kmod * to_end)], axis=0).astype(BF16)
        upd = lax.dot_general(upd_l, sa_v, (((0,), (0,)), ((), ())), preferred_element_type=F32)
        yield
        y = y_s[:C] + y_s[C:]
        g_col = jnp.sum(jnp.where(eye, g_end, 0.0), axis=1, keepdims=True)
        h_ref[q] = h * g_col + upd

        mean = headsum(y) * (1.0 / N)
        yield
        dev = y - mean
        var = headsum(dev * dev) * (1.0 / N)
        yield
        yn = dev * lax.rsqrt(var + LN_X_EPS) * lw + lb
        o_ref[sl, ln] = ((yn + bonus * v) * g_ref[sl, ln]).astype(o_ref.dtype)

    n_chunks = r_ref.shape[0] // C
    chains = [[pair_chunk(c, q) for q in range(r_ref.shape[1] // P)] for c in range(n_chunks)]
    sweep = 0
    while any(chains):
        for c in range(min(n_chunks, sweep // CHUNK_LAG + 1)):
            advanced = []
            for gen in chains[c]:
                try:
                    next(gen)
                    advanced.append(gen)
                except StopIteration:
                    pass
            chains[c] = advanced
        sweep += 1


def _scan(p_all, ld, a, g, kkw, ka, rk, lw, lb, *, batch, seq, r_block, k_block, v_block, tb=4 * CHUNK, pairs=12):
    t, rw = ld.shape
    width = pairs * LANES
    tpb = seq // tb
    assert seq % tb == 0 and tb % CHUNK == 0 and rw % width == 0 and CHUNK_LAG >= 3
    tok = lambda blk0: pl.BlockSpec((tb, width), lambda b, p, s: (b * tpb + s, blk0 // pairs + p))
    par = pl.BlockSpec((1, width), lambda b, p, s: (0, p))
    return pl.pallas_call(
        _scan_kernel,
        grid=(batch, rw // width, tpb),
        in_specs=[tok(r_block), tok(k_block), tok(v_block), tok(0), tok(0), tok(0),
                  par, par, par, par, par],
        out_specs=tok(0),
        out_shape=jax.ShapeDtypeStruct((t, rw), BF16),
        scratch_shapes=[pltpu.VMEM((pairs, LANES, LANES), F32)],
        compiler_params=_cparams(("arbitrary", "arbitrary", "arbitrary")),
        name="scan",
    )(p_all, p_all, p_all, ld, a, g, kkw, ka, rk, lw, lb)


def _pooled(u_ref, pw_ref, ps_ref, ext_ref, t0):
    tm = u_ref.shape[0]
    pad = max(POOL_WINDOWS)
    ext_ref[pad:pad + tm, :] = u_ref[...]
    tpos = (t0 + 1 + lax.broadcasted_iota(jnp.int32, (tm, 1), 0)).astype(F32)
    groups = []
    for gi, win in enumerate(POOL_WINDOWS):
        lo = gi * POOL_GROUP
        acc = ext_ref[pad:pad + tm, lo:lo + POOL_GROUP]
        u_g = acc
        for dlt in range(1, win):
            acc = acc + ext_ref[pad - dlt:pad - dlt + tm, lo:lo + POOL_GROUP]
        pooled = acc / jnp.minimum(tpos, float(win)) - u_g
        groups.append((_bdot(pooled, pw_ref[gi]) * ps_ref[:, lo:lo + POOL_GROUP]).astype(BF16))
    ext_ref[0:pad, :] = ext_ref[tm:tm + pad, :]
    return jnp.concatenate(groups, axis=1)


def _outproj_kernel(x_ref, u_ref, yr_ref, mod_ref, gpost_ref, pw_ref, ps_ref, wo_ref, o_ref, ext_ref, *, tps):
    i = pl.program_id(0)
    tm = x_ref.shape[0]
    n_pool = u_ref.shape[1]

    @pl.when(i % tps == 0)
    def _():
        ext_ref[0:max(POOL_WINDOWS), :] = jnp.zeros((max(POOL_WINDOWS), n_pool), F32)

    y = jnp.dot(yr_ref[...], wo_ref[n_pool:, :], preferred_element_type=F32)
    y_pool = _pooled(u_ref, pw_ref, ps_ref, ext_ref, (i % tps) * tm)
    y = y + jnp.dot(y_pool, wo_ref[:n_pool, :], preferred_element_type=F32)
    _residual_into(o_ref, x_ref, gpost_ref, mod_ref, 1, 1.0, y_all=y)


def _outproj(x2, p_all, pool_block, y_rwkv, mod3, gpost, pool_w, pool_scale, w_out, *, seq, tm=512):
    t, d = x2.shape
    n_pool = pool_scale.shape[1]
    tps = seq // tm
    assert seq % tm == 0 and n_pool == len(POOL_WINDOWS) * POOL_GROUP
    return pl.pallas_call(
        functools.partial(_outproj_kernel, tps=tps),
        grid=(t // tm,),
        in_specs=[pl.BlockSpec((tm, d), lambda i: (i, 0)),
                  pl.BlockSpec((tm, n_pool), lambda i: (i, pool_block)),
                  pl.BlockSpec((tm, y_rwkv.shape[1]), lambda i: (i, 0)),
                  pl.BlockSpec((1,) + mod3.shape[1:], lambda i: (i // tps, 0, 0)),
                  pl.BlockSpec((1, d), lambda i: (0, 0)),
                  pl.BlockSpec(pool_w.shape, lambda i: (0, 0, 0)),
                  pl.BlockSpec((1, n_pool), lambda i: (0, 0)),
                  pl.BlockSpec(w_out.shape, lambda i: (0, 0))],
        out_specs=pl.BlockSpec((tm, d), lambda i: (i, 0)),
        out_shape=jax.ShapeDtypeStruct((t, d), F32),
        scratch_shapes=[pltpu.VMEM((tm + max(POOL_WINDOWS), n_pool), F32)],
        compiler_params=_cparams(("arbitrary",)),
        name="outproj",
    )(x2, p_all, y_rwkv, mod3, gpost, pool_w, pool_scale, w_out)


def _pad_cols(w, n):
    return jnp.pad(w, ((0, 0), (0, n - w.shape[1])))


def _layer(x2, c, batch, seq, w_ada, b_ada, norm_pre, norm_post, f1g, f1u, f1d, w_in, mu_shift, pool_w,
           pool_scale, w0, w2, a0, a2, g2, k_k, k_a, r_k, lnx_w, lnx_b, w_out, f2g, f2u, f2d):
    d = x2.shape[1]
    pool_width = pool_scale.shape[0]
    rw = w0.shape[0]
    n_sub = norm_pre.shape[0]
    col_tile = 512

    mod3 = _ada(c, w_ada, b_ada).reshape(batch, n_sub * N_MOD, d)
    row = lambda vec: vec.reshape(1, -1)

    x2 = _ffn(x2, mod3, row(norm_pre[0]), row(norm_post[0]), f1g.astype(BF16), f1u.astype(BF16),
              f1d.astype(BF16), sub=0, seq=seq)

    n_lora = w_in.shape[1] - pool_width - 3 * rw
    assert (3 * rw) % col_tile == 0 and n_lora <= col_tile and pool_width == col_tile and rw % LANES == 0
    w_in_b = w_in.astype(BF16)
    w_in_p = jnp.concatenate([w_in_b[:, pool_width:pool_width + 3 * rw],
                              _pad_cols(w_in_b[:, pool_width + 3 * rw:], col_tile),
                              w_in_b[:, :pool_width]], axis=1)
    mu_p = jnp.concatenate([mu_shift, jnp.zeros((col_tile - n_lora + pool_width,), F32)])
    p_all = _inproj(x2, mod3, row(norm_pre[1]), w_in_p, row(mu_p), seq=seq, tm=1024,
                    tn=w_in_p.shape[1] // 4)
    lora_block = 3 * rw // col_tile

    n_w, n_a, n_g = w2.shape[0], a2.shape[0], g2.shape[0]
    assert n_w + n_a == LANES
    w2p = jnp.pad(w2, ((0, n_a), (0, 0))).astype(BF16)
    a2p = jnp.pad(a2, ((n_w, 0), (0, 0))).astype(BF16)
    g2p = jnp.pad(g2, ((0, -n_g % LANES), (0, 0))).astype(BF16)
    ld, a, g = _lora(p_all, lora_block, col_tile, w2p, a2p, g2p, row(w0), row(a0))

    y_rwkv = _scan(p_all, ld, a, g, row(k_k), row(k_a), row(r_k), row(lnx_w), row(lnx_b),
                   batch=batch, seq=seq, r_block=0, k_block=rw // LANES, v_block=2 * rw // LANES)

    x2 = _outproj(x2, p_all, lora_block + 1, y_rwkv, mod3, row(norm_post[1]), pool_w.astype(BF16),
                  row(pool_scale), w_out.astype(BF16), seq=seq)

    x2 = _ffn(x2, mod3, row(norm_pre[2]), row(norm_post[2]), f2g.astype(BF16), f2u.astype(BF16),
              f2d.astype(BF16), sub=2, seq=seq)
    return x2


def kernel(x, c, w_ada, b_ada, norm_pre, norm_post, ffn1_w_gate, ffn1_w_up, ffn1_w_down, w_in, mu_shift,
           pool_w, pool_scale, w0, w2, a0, a2, g2, k_k, k_a, r_k, lnx_w, lnx_b, w_out, ffn2_w_gate,
           ffn2_w_up, ffn2_w_down):
    batch, seq, d = x.shape
    x2 = x.reshape(batch * seq, d)
    for l in range(w_ada.shape[0]):
        x2 = _layer(x2, c, batch, seq, w_ada[l], b_ada[l], norm_pre[l], norm_post[l], ffn1_w_gate[l],
                    ffn1_w_up[l], ffn1_w_down[l], w_in[l], mu_shift[l], pool_w[l], pool_scale[l], w0[l],
                    w2[l], a0[l], a2[l], g2[l], k_k[l], k_a[l], r_k[l], lnx_w[l], lnx_b[l], w_out[l],
                    ffn2_w_gate[l], ffn2_w_up[l], ffn2_w_down[l])
    return x2.reshape(batch, seq, d)
```

```python
import functools

import jax
import jax.numpy as jnp
from jax import lax
from jax.experimental import pallas as pl
from jax.experimental.pallas import tpu as pltpu

F32 = jnp.float32
BF16 = jnp.bfloat16

NORM_EPS = 1e-6
HEAD_SIZE = 64
LN_X_EPS = 1e-5 * HEAD_SIZE
POOL_WINDOWS = (2, 4, 8, 16)
POOL_GROUP = 128
MACARON_WEIGHT = 0.5
N_MOD = 3
DECAY_SCALE = 0.6065306597126334

LANES = 128
CHUNK = 64
CHUNK_LAG = 8
ROW_BLOCK = 16
VMEM_LIMIT_MIB = 56


def _cparams(sem, vmem_mib=VMEM_LIMIT_MIB):
    return pltpu.CompilerParams(dimension_semantics=sem, vmem_limit_bytes=vmem_mib * 1024 * 1024)


def _sigmoid(z):
    return 0.5 * (1.0 + jnp.tanh(0.5 * z))


def _row_blocks(n_rows, fn):
    for b in range(n_rows // ROW_BLOCK):
        fn(pl.ds(b * ROW_BLOCK, ROW_BLOCK))


def _norm_mod_into(h_ref, x_ref, gain_ref, mod_ref, sub):
    shift = mod_ref[0, N_MOD * sub:N_MOD * sub + 1, :]
    scale = mod_ref[0, N_MOD * sub + 1:N_MOD * sub + 2, :]
    mult = gain_ref[...] * (1.0 + scale)
    blocks = []

    def block(rows):
        x = x_ref[rows, :]
        inv = lax.rsqrt(jnp.mean(x * x, axis=-1, keepdims=True) + NORM_EPS)
        blocks.append((x * inv * mult + shift).astype(BF16))
        h_ref[rows, :] = blocks[-1]

    _row_blocks(x_ref.shape[0], block)
    return jnp.concatenate(blocks, axis=0)


def _residual_into(o_ref, x_ref, gain_ref, mod_ref, sub, weight, y_all):
    gate = mod_ref[0, N_MOD * sub + 2:N_MOD * sub + 3, :]
    mult = gain_ref[...] * (weight * (1.0 + gate))

    def block(rows):
        y = y_all[rows.start:rows.start + rows.size, :]
        inv = lax.rsqrt(jnp.mean(y * y, axis=-1, keepdims=True) + NORM_EPS)
        o_ref[rows, :] = x_ref[rows, :] + y * inv * mult

    _row_blocks(o_ref.shape[0], block)


def _bdot(a, b):
    return jnp.dot(a.astype(BF16), b.astype(BF16), preferred_element_type=F32)


def _ada_kernel(c_ref, w_ref, b_ref, o_ref):
    c = c_ref[...]
    s = c * _sigmoid(c)
    o_ref[...] = jnp.dot(s, w_ref[...], preferred_element_type=F32,
                         precision=lax.Precision.HIGHEST) + b_ref[...]


def _ada(c, w, b, tn=1024):
    bsz, d = c.shape
    n = w.shape[1]
    return pl.pallas_call(
        _ada_kernel,
        grid=(n // tn,),
        in_specs=[pl.BlockSpec((bsz, d), lambda j: (0, 0)),
                  pl.BlockSpec((d, tn), lambda j: (0, j)),
                  pl.BlockSpec((1, tn), lambda j: (0, j))],
        out_specs=pl.BlockSpec((bsz, tn), lambda j: (0, j)),
        out_shape=jax.ShapeDtypeStruct((bsz, n), F32),
        compiler_params=_cparams(("arbitrary",)),
        name="ada",
    )(c, w, b.reshape(1, n))


def _ffn_kernel(x_ref, mod_ref, gpre_ref, gpost_ref, wg_ref, wu_ref, wd_ref, o_ref, h_ref, *, sub):
    j = pl.program_id(1)
    last = pl.num_programs(1) - 1

    def chunk(h):
        g = jnp.dot(h, wg_ref[...], preferred_element_type=F32)
        u = jnp.dot(h, wu_ref[...], preferred_element_type=F32)
        act = (g * _sigmoid(g) * u).astype(BF16)
        return jnp.dot(act, wd_ref[...], preferred_element_type=F32)

    @pl.when(j == 0)
    def _():
        o_ref[...] = chunk(_norm_mod_into(h_ref, x_ref, gpre_ref, mod_ref, sub))

    @pl.when((j > 0) & (j < last))
    def _():
        o_ref[...] += chunk(h_ref[...])

    @pl.when(j == last)
    def _():
        _residual_into(o_ref, x_ref, gpost_ref, mod_ref, sub, MACARON_WEIGHT,
                       y_all=o_ref[...] + chunk(h_ref[...]))


def _ffn(x2, mod3, gpre, gpost, wg, wu, wd, *, sub, seq, tm=1024, tf=512):
    t, d = x2.shape
    f = wg.shape[1]
    tps = seq // tm
    assert seq % tm == 0 and f % tf == 0 and f // tf >= 2
    spill_mib = 5
    vmem_mib = (2 * 2 * tm * d * 4 + tm * d * 2 + 3 * 2 * d * tf * 2 + tm * tf * (4 + 4 + 2)) // 2**20 + spill_mib
    return pl.pallas_call(
        functools.partial(_ffn_kernel, sub=sub),
        grid=(t // tm, f // tf),
        in_specs=[pl.BlockSpec((tm, d), lambda i, j: (i, 0)),
                  pl.BlockSpec((1,) + mod3.shape[1:], lambda i, j: (i // tps, 0, 0)),
                  pl.BlockSpec((1, d), lambda i, j: (0, 0)),
                  pl.BlockSpec((1, d), lambda i, j: (0, 0)),
                  pl.BlockSpec((d, tf), lambda i, j: (0, j)),
                  pl.BlockSpec((d, tf), lambda i, j: (0, j)),
                  pl.BlockSpec((tf, d), lambda i, j: (j, 0))],
        out_specs=pl.BlockSpec((tm, d), lambda i, j: (i, 0)),
        out_shape=jax.ShapeDtypeStruct((t, d), F32),
        scratch_shapes=[pltpu.VMEM((tm, d), BF16)],
        compiler_params=_cparams(("arbitrary", "arbitrary"), vmem_mib),
        name=f"ffn{sub}",
    )(x2, mod3, gpre, gpost, wg, wu, wd)


def _inproj_kernel(x_ref, mod_ref, gpre_ref, w_ref, mu_ref, o_ref, h_ref, carry_ref, *, tps):
    i = pl.program_id(0)
    j = pl.program_id(1)

    def project(h):
        res = jnp.dot(h, w_ref[...], preferred_element_type=F32)
        tm, tn = res.shape
        carried = carry_ref[j][0:1, :]
        first = jnp.where(i % tps == 0, jnp.zeros_like(carried), carried)
        row = lax.broadcasted_iota(jnp.int32, (tm, 1), 0)
        prev = jnp.where(row == 0, first, pltpu.roll(res, 1, 0))
        carry_ref[j] = jnp.broadcast_to(res[tm - 1:tm, :], (8, tn))
        o_ref[...] = res + mu_ref[...] * (prev - res)

    @pl.when(j == 0)
    def _():
        project(_norm_mod_into(h_ref, x_ref, gpre_ref, mod_ref, 1))

    @pl.when(j > 0)
    def _():
        project(h_ref[...])


def _inproj(x2, mod3, gpre, w, mu, *, seq, tm=512, tn=512):
    t, d = x2.shape
    n = w.shape[1]
    tps = seq // tm
    assert seq % tm == 0 and n % tn == 0
    return pl.pallas_call(
        functools.partial(_inproj_kernel, tps=tps),
        grid=(t // tm, n // tn),
        in_specs=[pl.BlockSpec((tm, d), lambda i, j: (i, 0)),
                  pl.BlockSpec((1,) + mod3.shape[1:], lambda i, j: (i // tps, 0, 0)),
                  pl.BlockSpec((1, d), lambda i, j: (0, 0)),
                  pl.BlockSpec((d, tn), lambda i, j: (0, j)),
                  pl.BlockSpec((1, tn), lambda i, j: (0, j))],
        out_specs=pl.BlockSpec((tm, tn), lambda i, j: (i, j)),
        out_shape=jax.ShapeDtypeStruct((t, n), F32),
        scratch_shapes=[pltpu.VMEM((tm, d), BF16), pltpu.VMEM((n // tn, 8, tn), F32)],
        compiler_params=_cparams(("arbitrary", "arbitrary")),
        name="inproj",
    )(x2, mod3, gpre, w, mu)


def _lora_kernel(p_ref, w2_ref, a2_ref, g2_ref, w0_ref, a0_ref, ld_ref, a_ref, g_ref):
    wa = p_ref[:, :LANES]
    gx = p_ref[:, LANES:LANES + g2_ref.shape[0]]
    z = w0_ref[...] + _bdot(jnp.tanh(wa), w2_ref[...])
    ld_ref[...] = -DECAY_SCALE * _sigmoid(z)
    a_ref[...] = _sigmoid(a0_ref[...] + _bdot(wa, a2_ref[...]))
    g_ref[...] = _bdot(_sigmoid(gx), g2_ref[...])


def _lora(p_all, col_block, col_tile, w2p, a2p, g2p, w0, a0, *, tm=512):
    t = p_all.shape[0]
    r = w2p.shape[1]
    assert t % tm == 0
    full = lambda arr: pl.BlockSpec(arr.shape, lambda i: (0, 0))
    out = jax.ShapeDtypeStruct((t, r), F32)
    return pl.pallas_call(
        _lora_kernel,
        grid=(t // tm,),
        in_specs=[pl.BlockSpec((tm, col_tile), lambda i: (i, col_block)),
                  full(w2p), full(a2p), full(g2p), full(w0), full(a0)],
        out_specs=[pl.BlockSpec((tm, r), lambda i: (i, 0))] * 3,
        out_shape=[out, out, out],
        compiler_params=_cparams(("arbitrary",)),
        name="lora",
    )(p_all, w2p, a2p, g2p, w0, a0)


def _scan_kernel(r_ref, k_ref, v_ref, ld_ref, a_ref, g_ref, kkw_ref, ka_ref, rk_ref, lw_ref, lb_ref,
                 o_ref, h_ref):
    C = CHUNK
    P = LANES
    N = HEAD_SIZE

    @pl.when(pl.program_id(2) == 0)
    def _():
        h_ref[...] = jnp.zeros_like(h_ref)

    lane = lax.broadcasted_iota(jnp.int32, (1, P), 1)
    in_a = lane < N
    ri = lax.broadcasted_iota(jnp.int32, (2 * C, 2 * C), 0)
    ci = lax.broadcasted_iota(jnp.int32, (2 * C, 2 * C), 1)

    def same(b):
        sh = b.bit_length() - 1
        return (ri >> sh) == (ci >> sh)

    strict = (ri > ci) & same(C)
    incl = (ri >= ci) & same(C)
    eye = ri == ci
    blk16 = strict & same(16)
    blk32 = strict & same(32) & jnp.logical_not(same(16))
    blk64 = strict & jnp.logical_not(same(32))
    eye_f = eye.astype(F32)
    tri3 = (lax.broadcasted_iota(jnp.int32, (C, 3 * C), 0)
            >= (lax.broadcasted_iota(jnp.int32, (C, 3 * C), 1) & (C - 1))).astype(BF16)

    def stack(x):
        return jnp.concatenate([jnp.where(in_a, x, 0.0), jnp.where(in_a, 0.0, x)], axis=0)

    def headsum(x):
        first = jnp.sum(jnp.where(in_a, x, 0.0), axis=-1, keepdims=True)
        second = jnp.sum(jnp.where(in_a, 0.0, x), axis=-1, keepdims=True)
        return jnp.where(in_a, first, second)

    def pair_chunk(c, q):
        sl = pl.ds(c * C, C)
        ln = pl.ds(q * P, P)
        kkw = kkw_ref[:, ln]
        ka = ka_ref[:, ln]
        rk = rk_ref[:, ln]
        lw = lw_ref[:, ln]
        lb = lb_ref[:, ln]
        r = r_ref[sl, ln]
        k = k_ref[sl, ln]
        v = v_ref[sl, ln]
        ld = ld_ref[sl, ln]
        a = a_ref[sl, ln]

        kkraw = k * kkw
        kmod = k * (1.0 + (a - 1.0) * ka)
        sums = headsum(jnp.concatenate([kkraw * kkraw, r * kmod * rk], axis=0))
        ld_hi = ld.astype(BF16)
        rem = ld - ld_hi.astype(F32)
        ld_mid = rem.astype(BF16)
        ld_lo = (rem - ld_mid.astype(F32)).astype(BF16)
        cum = jnp.dot(tri3, jnp.concatenate([ld_hi, ld_mid, ld_lo], axis=0), preferred_element_type=F32)
        yield
        kk = kkraw / jnp.maximum(jnp.sqrt(sums[:C]), 1e-12)
        bonus = sums[C:]
        cum_last = cum[C - 1:C, :]

        inv_g = jnp.exp(-cum)
        to_end = jnp.exp(cum_last - cum)
        kb = kk * a
        rt = r * jnp.exp(cum)
        at = -kk * jnp.exp(cum - ld)
        bt = kb * inv_g
        kt = kmod * inv_g
        g_end = jnp.exp(cum_last)

        s_v = stack(v).astype(BF16)
        lhs = jnp.concatenate([stack(at), stack(rt)], axis=0).astype(BF16)
        rhs = jnp.concatenate([stack(bt), stack(kt)], axis=0).astype(BF16)
        scores = lax.dot_general(lhs, rhs, (((1,), (1,)), ((), ())), preferred_element_type=F32)
        s_at = lhs[:2 * C]
        s_rt = lhs[2 * C:]
        yield
        s_ab = scores[:2 * C, :2 * C]
        s_ak = jnp.where(strict, scores[:2 * C, 2 * C:], 0.0)
        s_rb = jnp.where(incl, scores[2 * C:, :2 * C], 0.0)
        s_rk = jnp.where(incl, scores[2 * C:, 2 * C:], 0.0)

        x1 = jnp.where(blk16, s_ab, 0.0)
        x2 = _bdot(x1, x1)
        yield
        x4 = _bdot(x2, x2)
        tinv = eye_f + x1
        tinv = tinv + _bdot(tinv, x2)
        yield
        x8 = _bdot(x4, x4)
        tinv = tinv + _bdot(tinv, x4)
        yield
        tinv = tinv + _bdot(tinv, x8)
        yield
        for blk in (blk32, blk64):
            off = jnp.where(blk, s_ab, 0.0)
            part = _bdot(tinv, off)
            yield
            tinv = tinv + _bdot(part, tinv)
            yield

        h = h_ref[q]
        h_b = h.astype(BF16)
        rhs_sa = jnp.dot(jnp.concatenate([s_at, s_ak.astype(BF16)], axis=1),
                         jnp.concatenate([h_b, s_v], axis=0), preferred_element_type=F32)
        yield
        s_sa = _bdot(tinv, rhs_sa)
        yield
        sa_v = jnp.concatenate([s_sa.astype(BF16), s_v], axis=0)
        y_s = jnp.dot(jnp.concatenate([s_rt, s_rb.astype(BF16), s_rk.astype(BF16)], axis=1),
                      jnp.concatenate([h_b, sa_v], axis=0), preferred_element_type=F32)
        upd_l = jnp.concatenate([stack(kb * to_end), stack(kmod * to_end)], axis=0).astype(BF16)
        upd = lax.dot_general(upd_l, sa_v, (((0,), (0,)), ((), ())), preferred_element_type=F32)
        yield
        y = y_s[:C] + y_s[C:]
        g_col = jnp.sum(jnp.where(eye, g_end, 0.0), axis=1, keepdims=True)
        h_ref[q] = h * g_col + upd

        mean = headsum(y) * (1.0 / N)
        yield
        dev = y - mean
        var = headsum(dev * dev) * (1.0 / N)
        yield
        yn = dev * lax.rsqrt(var + LN_X_EPS) * lw + lb
        o_ref[sl, ln] = ((yn + bonus * v) * g_ref[sl, ln]).astype(o_ref.dtype)

    n_chunks = r_ref.shape[0] // C
    chains = [[pair_chunk(c, q) for q in range(r_ref.shape[1] // P)] for c in range(n_chunks)]
    sweep = 0
    while any(chains):
        for c in range(min(n_chunks, sweep // CHUNK_LAG + 1)):
            advanced = []
            for gen in chains[c]:
                try:
                    next(gen)
                    advanced.append(gen)
                except StopIteration:
                    pass
            chains[c] = advanced
        sweep += 1


def _scan(p_all, ld, a, g, kkw, ka, rk, lw, lb, *, batch, seq, r_block, k_block, v_block, tb=4 * CHUNK, pairs=12):
    t, rw = ld.shape
    width = pairs * LANES
    tpb = seq // tb
    assert seq % tb == 0 and tb % CHUNK == 0 and rw % width == 0 and CHUNK_LAG >= 3
    tok = lambda blk0: pl.BlockSpec((tb, width), lambda b, p, s: (b * tpb + s, blk0 // pairs + p))
    par = pl.BlockSpec((1, width), lambda b, p, s: (0, p))
    return pl.pallas_call(
        _scan_kernel,
        grid=(batch, rw // width, tpb),
        in_specs=[tok(r_block), tok(k_block), tok(v_block), tok(0), tok(0), tok(0),
                  par, par, par, par, par],
        out_specs=tok(0),
        out_shape=jax.ShapeDtypeStruct((t, rw), BF16),
        scratch_shapes=[pltpu.VMEM((pairs, LANES, LANES), F32)],
        compiler_params=_cparams(("arbitrary", "arbitrary", "arbitrary")),
        name="scan",
    )(p_all, p_all, p_all, ld, a, g, kkw, ka, rk, lw, lb)


def _pooled(u_ref, pw_ref, ps_ref, ext_ref, t0):
    tm = u_ref.shape[0]
    pad = max(POOL_WINDOWS)
    ext_ref[pad:pad + tm, :] = u_ref[...]
    tpos = (t0 + 1 + lax.broadcasted_iota(jnp.int32, (tm, 1), 0)).astype(F32)
    groups = []
    for gi, win in enumerate(POOL_WINDOWS):
        lo = gi * POOL_GROUP
        acc = ext_ref[pad:pad + tm, lo:lo + POOL_GROUP]
        u_g = acc
        for dlt in range(1, win):
            acc = acc + ext_ref[pad - dlt:pad - dlt + tm, lo:lo + POOL_GROUP]
        pooled = acc / jnp.minimum(tpos, float(win)) - u_g
        groups.append((_bdot(pooled, pw_ref[gi]) * ps_ref[:, lo:lo + POOL_GROUP]).astype(BF16))
    ext_ref[0:pad, :] = ext_ref[tm:tm + pad, :]
    return jnp.concatenate(groups, axis=1)


def _outproj_kernel(x_ref, u_ref, yr_ref, mod_ref, gpost_ref, pw_ref, ps_ref, wo_ref, o_ref, ext_ref, *, tps):
    i = pl.program_id(0)
    tm = x_ref.shape[0]
    n_pool = u_ref.shape[1]

    @pl.when(i % tps == 0)
    def _():
        ext_ref[0:max(POOL_WINDOWS), :] = jnp.zeros((max(POOL_WINDOWS), n_pool), F32)

    y = jnp.dot(yr_ref[...], wo_ref[n_pool:, :], preferred_element_type=F32)
    y_pool = _pooled(u_ref, pw_ref, ps_ref, ext_ref, (i % tps) * tm)
    y = y + jnp.dot(y_pool, wo_ref[:n_pool, :], preferred_element_type=F32)
    _residual_into(o_ref, x_ref, gpost_ref, mod_ref, 1, 1.0, y_all=y)


def _outproj(x2, p_all, pool_block, y_rwkv, mod3, gpost, pool_w, pool_scale, w_out, *, seq, tm=512):
    t, d = x2.shape
    n_pool = pool_scale.shape[1]
    tps = seq // tm
    assert seq % tm == 0 and n_pool == len(POOL_WINDOWS) * POOL_GROUP
    return pl.pallas_call(
        functools.partial(_outproj_kernel, tps=tps),
        grid=(t // tm,),
        in_specs=[pl.BlockSpec((tm, d), lambda i: (i, 0)),
                  pl.BlockSpec((tm, n_pool), lambda i: (i, pool_block)),
                  pl.BlockSpec((tm, y_rwkv.shape[1]), lambda i: (i, 0)),
                  pl.BlockSpec((1,) + mod3.shape[1:], lambda i: (i // tps, 0, 0)),
                  pl.BlockSpec((1, d), lambda i: (0, 0)),
                  pl.BlockSpec(pool_w.shape, lambda i: (0, 0, 0)),
                  pl.BlockSpec((1, n_pool), lambda i: (0, 0)),
                  pl.BlockSpec(w_out.shape, lambda i: (0, 0))],
        out_specs=pl.BlockSpec((tm, d), lambda i: (i, 0)),
        out_shape=jax.ShapeDtypeStruct((t, d), F32),
        scratch_shapes=[pltpu.VMEM((tm + max(POOL_WINDOWS), n_pool), F32)],
        compiler_params=_cparams(("arbitrary",)),
        name="outproj",
    )(x2, p_all, y_rwkv, mod3, gpost, pool_w, pool_scale, w_out)


def _pad_cols(w, n):
    return jnp.pad(w, ((0, 0), (0, n - w.shape[1])))


def _layer(x2, c, batch, seq, w_ada, b_ada, norm_pre, norm_post, f1g, f1u, f1d, w_in, mu_shift, pool_w,
           pool_scale, w0, w2, a0, a2, g2, k_k, k_a, r_k, lnx_w, lnx_b, w_out, f2g, f2u, f2d):
    d = x2.shape[1]
    pool_width = pool_scale.shape[0]
    rw = w0.shape[0]
    n_sub = norm_pre.shape[0]
    col_tile = 512

    mod3 = _ada(c, w_ada, b_ada).reshape(batch, n_sub * N_MOD, d)
    row = lambda vec: vec.reshape(1, -1)

    x2 = _ffn(x2, mod3, row(norm_pre[0]), row(norm_post[0]), f1g.astype(BF16), f1u.astype(BF16),
              f1d.astype(BF16), sub=0, seq=seq)

    n_lora = w_in.shape[1] - pool_width - 3 * rw
    assert (3 * rw) % col_tile == 0 and n_lora <= col_tile and pool_width == col_tile and rw % LANES == 0
    w_in_b = w_in.astype(BF16)
    w_in_p = jnp.concatenate([w_in_b[:, pool_width:pool_width + 3 * rw],
                              _pad_cols(w_in_b[:, pool_width + 3 * rw:], col_tile),
                              w_in_b[:, :pool_width]], axis=1)
    mu_p = jnp.concatenate([mu_shift, jnp.zeros((col_tile - n_lora + pool_width,), F32)])
    p_all = _inproj(x2, mod3, row(norm_pre[1]), w_in_p, row(mu_p), seq=seq, tm=1024,
                    tn=w_in_p.shape[1] // 4)
    lora_block = 3 * rw // col_tile

    n_w, n_a, n_g = w2.shape[0], a2.shape[0], g2.shape[0]
    assert n_w + n_a == LANES
    w2p = jnp.pad(w2, ((0, n_a), (0, 0))).astype(BF16)
    a2p = jnp.pad(a2, ((n_w, 0), (0, 0))).astype(BF16)
    g2p = jnp.pad(g2, ((0, -n_g % LANES), (0, 0))).astype(BF16)
    ld, a, g = _lora(p_all, lora_block, col_tile, w2p, a2p, g2p, row(w0), row(a0))

    y_rwkv = _scan(p_all, ld, a, g, row(k_k), row(k_a), row(r_k), row(lnx_w), row(lnx_b),
                   batch=batch, seq=seq, r_block=0, k_block=rw // LANES, v_block=2 * rw // LANES)

    x2 = _outproj(x2, p_all, lora_block + 1, y_rwkv, mod3, row(norm_post[1]), pool_w.astype(BF16),
                  row(pool_scale), w_out.astype(BF16), seq=seq)

    x2 = _ffn(x2, mod3, row(norm_pre[2]), row(norm_post[2]), f2g.astype(BF16), f2u.astype(BF16),
              f2d.astype(BF16), sub=2, seq=seq)
    return x2


def kernel(x, c, w_ada, b_ada, norm_pre, norm_post, ffn1_w_gate, ffn1_w_up, ffn1_w_down, w_in, mu_shift,
           pool_w, pool_scale, w0, w2, a0, a2, g2, k_k, k_a, r_k, lnx_w, lnx_b, w_out, ffn2_w_gate,
           ffn2_w_up, ffn2_w_down):
    batch, seq, d = x.shape
    x2 = x.reshape(batch * seq, d)
    for l in range(w_ada.shape[0]):
        x2 = _layer(x2, c, batch, seq, w_ada[l], b_ada[l], norm_pre[l], norm_post[l], ffn1_w_gate[l],
                    ffn1_w_up[l], ffn1_w_down[l], w_in[l], mu_shift[l], pool_w[l], pool_scale[l], w0[l],
                    w2[l], a0[l], a2[l], g2[l], k_k[l], k_a[l], r_k[l], lnx_w[l], lnx_b[l], w_out[l],
                    ffn2_w_gate[l], ffn2_w_up[l], ffn2_w_down[l])
    return x2.reshape(batch, seq, d)
```

```python
import functools

import jax
import jax.numpy as jnp
from jax import lax
from jax.experimental import pallas as pl
from jax.experimental.pallas import tpu as pltpu

F32 = jnp.float32
BF16 = jnp.bfloat16

NORM_EPS = 1e-6
HEAD_SIZE = 64
LN_X_EPS = 1e-5 * HEAD_SIZE
POOL_WINDOWS = (2, 4, 8, 16)
POOL_GROUP = 128
MACARON_WEIGHT = 0.5
N_MOD = 3
DECAY_SCALE = 0.6065306597126334

LANES = 128
CHUNK = 64
CHUNK_LAG = 8
MAPS_SWEEPS = (3, 19)
ROW_BLOCK = 16
VMEM_LIMIT_MIB = 56


def _cparams(sem, vmem_mib=VMEM_LIMIT_MIB):
    return pltpu.CompilerParams(dimension_semantics=sem, vmem_limit_bytes=vmem_mib * 1024 * 1024)


def _sigmoid(z):
    return 0.5 * (1.0 + jnp.tanh(0.5 * z))


def _row_blocks(n_rows, fn):
    for b in range(n_rows // ROW_BLOCK):
        fn(pl.ds(b * ROW_BLOCK, ROW_BLOCK))


def _norm_mod_into(h_ref, x_ref, gain_ref, mod_ref, sub):
    shift = mod_ref[0, N_MOD * sub:N_MOD * sub + 1, :]
    scale = mod_ref[0, N_MOD * sub + 1:N_MOD * sub + 2, :]
    mult = gain_ref[...] * (1.0 + scale)
    blocks = []

    def block(rows):
        x = x_ref[rows, :]
        inv = lax.rsqrt(jnp.mean(x * x, axis=-1, keepdims=True) + NORM_EPS)
        blocks.append((x * inv * mult + shift).astype(BF16))
        h_ref[rows, :] = blocks[-1]

    _row_blocks(x_ref.shape[0], block)
    return jnp.concatenate(blocks, axis=0)


def _residual_into(o_ref, x_ref, gain_ref, mod_ref, sub, weight, y_all):
    gate = mod_ref[0, N_MOD * sub + 2:N_MOD * sub + 3, :]
    mult = gain_ref[...] * (weight * (1.0 + gate))

    def block(rows):
        y = y_all[rows.start:rows.start + rows.size, :]
        inv = lax.rsqrt(jnp.mean(y * y, axis=-1, keepdims=True) + NORM_EPS)
        o_ref[rows, :] = x_ref[rows, :] + y * inv * mult

    _row_blocks(o_ref.shape[0], block)


def _bdot(a, b):
    return jnp.dot(a.astype(BF16), b.astype(BF16), preferred_element_type=F32)


def _ada_kernel(c_ref, w_ref, b_ref, o_ref):
    c = c_ref[...]
    s = c * _sigmoid(c)
    o_ref[...] = jnp.dot(s, w_ref[...], preferred_element_type=F32,
                         precision=lax.Precision.HIGHEST) + b_ref[...]


def _ada(c, w, b, tn=1024):
    bsz, d = c.shape
    n = w.shape[1]
    return pl.pallas_call(
        _ada_kernel,
        grid=(n // tn,),
        in_specs=[pl.BlockSpec((bsz, d), lambda j: (0, 0)),
                  pl.BlockSpec((d, tn), lambda j: (0, j)),
                  pl.BlockSpec((1, tn), lambda j: (0, j))],
        out_specs=pl.BlockSpec((bsz, tn), lambda j: (0, j)),
        out_shape=jax.ShapeDtypeStruct((bsz, n), F32),
        compiler_params=_cparams(("arbitrary",)),
        name="ada",
    )(c, w, b.reshape(1, n))


def _ffn_kernel(x_ref, mod_ref, gpre_ref, gpost_ref, wg_ref, wu_ref, wd_ref, o_ref, h_ref, *, sub):
    j = pl.program_id(1)
    last = pl.num_programs(1) - 1

    def chunk(h):
        g = jnp.dot(h, wg_ref[...], preferred_element_type=F32)
        u = jnp.dot(h, wu_ref[...], preferred_element_type=F32)
        act = (g * _sigmoid(g) * u).astype(BF16)
        return jnp.dot(act, wd_ref[...], preferred_element_type=F32)

    @pl.when(j == 0)
    def _():
        o_ref[...] = chunk(_norm_mod_into(h_ref, x_ref, gpre_ref, mod_ref, sub))

    @pl.when((j > 0) & (j < last))
    def _():
        o_ref[...] += chunk(h_ref[...])

    @pl.when(j == last)
    def _():
        _residual_into(o_ref, x_ref, gpost_ref, mod_ref, sub, MACARON_WEIGHT,
                       y_all=o_ref[...] + chunk(h_ref[...]))


def _ffn(x2, mod3, gpre, gpost, wg, wu, wd, *, sub, seq, tm=1024, tf=512):
    t, d = x2.shape
    f = wg.shape[1]
    tps = seq // tm
    assert seq % tm == 0 and f % tf == 0 and f // tf >= 2
    spill_mib = 5
    vmem_mib = (2 * 2 * tm * d * 4 + tm * d * 2 + 3 * 2 * d * tf * 2 + tm * tf * (4 + 4 + 2)) // 2**20 + spill_mib
    return pl.pallas_call(
        functools.partial(_ffn_kernel, sub=sub),
        grid=(t // tm, f // tf),
        in_specs=[pl.BlockSpec((tm, d), lambda i, j: (i, 0)),
                  pl.BlockSpec((1,) + mod3.shape[1:], lambda i, j: (i // tps, 0, 0)),
                  pl.BlockSpec((1, d), lambda i, j: (0, 0)),
                  pl.BlockSpec((1, d), lambda i, j: (0, 0)),
                  pl.BlockSpec((d, tf), lambda i, j: (0, j)),
                  pl.BlockSpec((d, tf), lambda i, j: (0, j)),
                  pl.BlockSpec((tf, d), lambda i, j: (j, 0))],
        out_specs=pl.BlockSpec((tm, d), lambda i, j: (i, 0)),
        out_shape=jax.ShapeDtypeStruct((t, d), F32),
        scratch_shapes=[pltpu.VMEM((tm, d), BF16)],
        compiler_params=_cparams(("arbitrary", "arbitrary"), vmem_mib),
        name=f"ffn{sub}",
    )(x2, mod3, gpre, gpost, wg, wu, wd)


def _inproj_kernel(x_ref, mod_ref, gpre_ref, w_ref, mu_ref, o_ref, h_ref, carry_ref, *, tps):
    i = pl.program_id(0)
    j = pl.program_id(1)

    def project(h):
        res = jnp.dot(h, w_ref[...], preferred_element_type=F32)
        tm, tn = res.shape
        carried = carry_ref[j][0:1, :]
        first = jnp.where(i % tps == 0, jnp.zeros_like(carried), carried)
        row = lax.broadcasted_iota(jnp.int32, (tm, 1), 0)
        prev = jnp.where(row == 0, first, pltpu.roll(res, 1, 0))
        carry_ref[j] = jnp.broadcast_to(res[tm - 1:tm, :], (8, tn))
        o_ref[...] = res + mu_ref[...] * (prev - res)

    @pl.when(j == 0)
    def _():
        project(_norm_mod_into(h_ref, x_ref, gpre_ref, mod_ref, 1))

    @pl.when(j > 0)
    def _():
        project(h_ref[...])


def _inproj(x2, mod3, gpre, w, mu, *, seq, tm=512, tn=512):
    t, d = x2.shape
    n = w.shape[1]
    tps = seq // tm
    assert seq % tm == 0 and n % tn == 0
    return pl.pallas_call(
        functools.partial(_inproj_kernel, tps=tps),
        grid=(t // tm, n // tn),
        in_specs=[pl.BlockSpec((tm, d), lambda i, j: (i, 0)),
                  pl.BlockSpec((1,) + mod3.shape[1:], lambda i, j: (i // tps, 0, 0)),
                  pl.BlockSpec((1, d), lambda i, j: (0, 0)),
                  pl.BlockSpec((d, tn), lambda i, j: (0, j)),
                  pl.BlockSpec((1, tn), lambda i, j: (0, j))],
        out_specs=pl.BlockSpec((tm, tn), lambda i, j: (i, j)),
        out_shape=jax.ShapeDtypeStruct((t, n), F32),
        scratch_shapes=[pltpu.VMEM((tm, d), BF16), pltpu.VMEM((n // tn, 8, tn), F32)],
        compiler_params=_cparams(("arbitrary", "arbitrary")),
        name="inproj",
    )(x2, mod3, gpre, w, mu)


def _low_rank_stage1(p_ref, w2_ref, a2_ref, g2_ref):
    wa = p_ref[:, :LANES]
    gx = p_ref[:, LANES:LANES + g2_ref.shape[0]]
    return (_bdot(jnp.tanh(wa), w2_ref[...]), _bdot(wa, a2_ref[...]), _bdot(_sigmoid(gx), g2_ref[...]))


def _low_rank_stage2(dots, w0_ref, a0_ref):
    zw, za, g = dots
    return -DECAY_SCALE * _sigmoid(w0_ref[...] + zw), _sigmoid(a0_ref[...] + za), g


def _scan_kernel(r_ref, k_ref, v_ref, pc_ref, pn_ref, w2_ref, a2_ref, g2_ref, w0_ref, a0_ref,
                 kkw_ref, ka_ref, rk_ref, lw_ref, lb_ref, o_ref, h_ref, ld_ref, a_ref, g_ref):
    C = CHUNK
    P = LANES
    N = HEAD_SIZE

    @pl.when(pl.program_id(2) == 0)
    def _():
        h_ref[...] = jnp.zeros_like(h_ref)
        ld0, a0v, g0 = _low_rank_stage2(_low_rank_stage1(pc_ref, w2_ref, a2_ref, g2_ref), w0_ref, a0_ref)
        ld_ref[...] = ld0
        a_ref[...] = a0v
        g_ref[...] = g0

    lane = lax.broadcasted_iota(jnp.int32, (1, P), 1)
    in_a = lane < N
    ri = lax.broadcasted_iota(jnp.int32, (2 * C, 2 * C), 0)
    ci = lax.broadcasted_iota(jnp.int32, (2 * C, 2 * C), 1)

    def same(b):
        sh = b.bit_length() - 1
        return (ri >> sh) == (ci >> sh)

    strict = (ri > ci) & same(C)
    incl = (ri >= ci) & same(C)
    eye = ri == ci
    blk16 = strict & same(16)
    blk32 = strict & same(32) & jnp.logical_not(same(16))
    blk64 = strict & jnp.logical_not(same(32))
    eye_f = eye.astype(F32)
    tri3 = (lax.broadcasted_iota(jnp.int32, (C, 3 * C), 0)
            >= (lax.broadcasted_iota(jnp.int32, (C, 3 * C), 1) & (C - 1))).astype(BF16)

    def stack(x):
        return jnp.concatenate([jnp.where(in_a, x, 0.0), jnp.where(in_a, 0.0, x)], axis=0)

    def headsum(x):
        first = jnp.sum(jnp.where(in_a, x, 0.0), axis=-1, keepdims=True)
        second = jnp.sum(jnp.where(in_a, 0.0, x), axis=-1, keepdims=True)
        return jnp.where(in_a, first, second)

    def pair_chunk(c, q):
        sl = pl.ds(c * C, C)
        ln = pl.ds(q * P, P)
        kkw = kkw_ref[:, ln]
        ka = ka_ref[:, ln]
        rk = rk_ref[:, ln]
        lw = lw_ref[:, ln]
        lb = lb_ref[:, ln]
        r = r_ref[sl, ln]
        k = k_ref[sl, ln]
        v = v_ref[sl, ln]
        ld = ld_ref[sl, ln]
        a = a_ref[sl, ln]

        kkraw = k * kkw
        kmod = k * (1.0 + (a - 1.0) * ka)
        sums = headsum(jnp.concatenate([kkraw * kkraw, r * kmod * rk], axis=0))
        ld_hi = ld.astype(BF16)
        rem = ld - ld_hi.astype(F32)
        ld_mid = rem.astype(BF16)
        ld_lo = (rem - ld_mid.astype(F32)).astype(BF16)
        cum = jnp.dot(tri3, jnp.concatenate([ld_hi, ld_mid, ld_lo], axis=0), preferred_element_type=F32)
        yield
        kk = kkraw / jnp.maximum(jnp.sqrt(sums[:C]), 1e-12)
        bonus = sums[C:]
        cum_last = cum[C - 1:C, :]

        inv_g = jnp.exp(-cum)
        to_end = jnp.exp(cum_last - cum)
        kb = kk * a
        rt = r * jnp.exp(cum)
        at = -kk * jnp.exp(cum - ld)
        bt = kb * inv_g
        kt = kmod * inv_g
        g_end = jnp.exp(cum_last)

        s_v = stack(v).astype(BF16)
        lhs = jnp.concatenate([stack(at), stack(rt)], axis=0).astype(BF16)
        rhs = jnp.concatenate([stack(bt), stack(kt)], axis=0).astype(BF16)
        scores = lax.dot_general(lhs, rhs, (((1,), (1,)), ((), ())), preferred_element_type=F32)
        s_at = lhs[:2 * C]
        s_rt = lhs[2 * C:]
        yield
        s_ab = scores[:2 * C, :2 * C]
        s_ak = jnp.where(strict, scores[:2 * C, 2 * C:], 0.0)
        s_rb = jnp.where(incl, scores[2 * C:, :2 * C], 0.0)
        s_rk = jnp.where(incl, scores[2 * C:, 2 * C:], 0.0)

        x1 = jnp.where(blk16, s_ab, 0.0)
        x2 = _bdot(x1, x1)
        yield
        x4 = _bdot(x2, x2)
        tinv = eye_f + x1
        tinv = tinv + _bdot(tinv, x2)
        yield
        x8 = _bdot(x4, x4)
        tinv = tinv + _bdot(tinv, x4)
        yield
        tinv = tinv + _bdot(tinv, x8)
        yield
        for blk in (blk32, blk64):
            off = jnp.where(blk, s_ab, 0.0)
            part = _bdot(tinv, off)
            yield
            tinv = tinv + _bdot(part, tinv)
            yield

        h = h_ref[q]
        h_b = h.astype(BF16)
        rhs_sa = jnp.dot(jnp.concatenate([s_at, s_ak.astype(BF16)], axis=1),
                         jnp.concatenate([h_b, s_v], axis=0), preferred_element_type=F32)
        yield
        s_sa = _bdot(tinv, rhs_sa)
        yield
        sa_v = jnp.concatenate([s_sa.astype(BF16), s_v], axis=0)
        y_s = jnp.dot(jnp.concatenate([s_rt, s_rb.astype(BF16), s_rk.astype(BF16)], axis=1),
                      jnp.concatenate([h_b, sa_v], axis=0), preferred_element_type=F32)
        upd_l = jnp.concatenate([stack(kb * to_end), stack(kmod * to_end)], axis=0).astype(BF16)
        upd = lax.dot_general(upd_l, sa_v, (((0,), (0,)), ((), ())), preferred_element_type=F32)
        yield
        y = y_s[:C] + y_s[C:]
        g_col = jnp.sum(jnp.where(eye, g_end, 0.0), axis=1, keepdims=True)
        h_ref[q] = h * g_col + upd

        mean = headsum(y) * (1.0 / N)
        yield
        dev = y - mean
        var = headsum(dev * dev) * (1.0 / N)
        yield
        yn = dev * lax.rsqrt(var + LN_X_EPS) * lw + lb
        o_ref[sl, ln] = ((yn + bonus * v) * g_ref[sl, ln]).astype(o_ref.dtype)

    n_chunks = r_ref.shape[0] // C
    chains = [[pair_chunk(c, q) for q in range(r_ref.shape[1] // P)] for c in range(n_chunks)]
    next_maps = {}
    sweep = 0
    while any(chains):
        if sweep == MAPS_SWEEPS[0]:
            next_maps["dots"] = _low_rank_stage1(pn_ref, w2_ref, a2_ref, g2_ref)
        if sweep == MAPS_SWEEPS[1]:
            next_maps["out"] = _low_rank_stage2(next_maps["dots"], w0_ref, a0_ref)
        for c in range(min(n_chunks, sweep // CHUNK_LAG + 1)):
            advanced = []
            for gen in chains[c]:
                try:
                    next(gen)
                    advanced.append(gen)
                except StopIteration:
                    pass
            chains[c] = advanced
        sweep += 1
    assert sweep > MAPS_SWEEPS[1]
    ld_ref[...], a_ref[...], g_ref[...] = next_maps["out"]


def _scan(p_all, lora_block, col_tile, w2p, a2p, g2p, w0, a0, kkw, ka, rk, lw, lb, *, batch, seq, r_block,
          k_block, v_block, tb=4 * CHUNK, pairs=12):
    t = p_all.shape[0]
    rw = kkw.shape[1]
    width = pairs * LANES
    tpb = seq // tb
    assert seq % tb == 0 and tb % CHUNK == 0 and rw == width and CHUNK_LAG >= 3
    tok = lambda blk0: pl.BlockSpec((tb, width), lambda b, p, s: (b * tpb + s, blk0 // pairs + p))
    par = pl.BlockSpec((1, width), lambda b, p, s: (0, p))
    full = lambda arr: pl.BlockSpec(arr.shape, lambda b, p, s: (0, 0))
    maps_now = pl.BlockSpec((tb, col_tile), lambda b, p, s: (b * tpb + s, lora_block))
    maps_next = pl.BlockSpec((tb, col_tile), lambda b, p, s: (b * tpb + jnp.minimum(s + 1, tpb - 1), lora_block))
    return pl.pallas_call(
        _scan_kernel,
        grid=(batch, rw // width, tpb),
        in_specs=[tok(r_block), tok(k_block), tok(v_block), maps_now, maps_next,
                  full(w2p), full(a2p), full(g2p), full(w0), full(a0), par, par, par, par, par],
        out_specs=tok(0),
        out_shape=jax.ShapeDtypeStruct((t, rw), BF16),
        scratch_shapes=[pltpu.VMEM((pairs, LANES, LANES), F32)] + [pltpu.VMEM((tb, rw), F32)] * 3,
        compiler_params=_cparams(("arbitrary", "arbitrary", "arbitrary")),
        name="scan",
    )(p_all, p_all, p_all, p_all, p_all, w2p, a2p, g2p, w0, a0, kkw, ka, rk, lw, lb)


def _pooled(u_ref, pw_ref, ps_ref, ext_ref, t0):
    tm = u_ref.shape[0]
    pad = max(POOL_WINDOWS)
    ext_ref[pad:pad + tm, :] = u_ref[...]
    tpos = (t0 + 1 + lax.broadcasted_iota(jnp.int32, (tm, 1), 0)).astype(F32)
    groups = []
    for gi, win in enumerate(POOL_WINDOWS):
        lo = gi * POOL_GROUP
        acc = ext_ref[pad:pad + tm, lo:lo + POOL_GROUP]
        u_g = acc
        for dlt in range(1, win):
            acc = acc + ext_ref[pad - dlt:pad - dlt + tm, lo:lo + POOL_GROUP]
        pooled = acc / jnp.minimum(tpos, float(win)) - u_g
        groups.append((_bdot(pooled, pw_ref[gi]) * ps_ref[:, lo:lo + POOL_GROUP]).astype(BF16))
    ext_ref[0:pad, :] = ext_ref[tm:tm + pad, :]
    return jnp.concatenate(groups, axis=1)


def _outproj_kernel(x_ref, u_ref, yr_ref, mod_ref, gpost_ref, pw_ref, ps_ref, wo_ref, o_ref, ext_ref, *, tps):
    i = pl.program_id(0)
    tm = x_ref.shape[0]
    n_pool = u_ref.shape[1]

    @pl.when(i % tps == 0)
    def _():
        ext_ref[0:max(POOL_WINDOWS), :] = jnp.zeros((max(POOL_WINDOWS), n_pool), F32)

    y = jnp.dot(yr_ref[...], wo_ref[n_pool:, :], preferred_element_type=F32)
    y_pool = _pooled(u_ref, pw_ref, ps_ref, ext_ref, (i % tps) * tm)
    y = y + jnp.dot(y_pool, wo_ref[:n_pool, :], preferred_element_type=F32)
    _residual_into(o_ref, x_ref, gpost_ref, mod_ref, 1, 1.0, y_all=y)


def _outproj(x2, p_all, pool_block, y_rwkv, mod3, gpost, pool_w, pool_scale, w_out, *, seq, tm=512):
    t, d = x2.shape
    n_pool = pool_scale.shape[1]
    tps = seq // tm
    assert seq % tm == 0 and n_pool == len(POOL_WINDOWS) * POOL_GROUP
    return pl.pallas_call(
        functools.partial(_outproj_kernel, tps=tps),
        grid=(t // tm,),
        in_specs=[pl.BlockSpec((tm, d), lambda i: (i, 0)),
                  pl.BlockSpec((tm, n_pool), lambda i: (i, pool_block)),
                  pl.BlockSpec((tm, y_rwkv.shape[1]), lambda i: (i, 0)),
                  pl.BlockSpec((1,) + mod3.shape[1:], lambda i: (i // tps, 0, 0)),
                  pl.BlockSpec((1, d), lambda i: (0, 0)),
                  pl.BlockSpec(pool_w.shape, lambda i: (0, 0, 0)),
                  pl.BlockSpec((1, n_pool), lambda i: (0, 0)),
                  pl.BlockSpec(w_out.shape, lambda i: (0, 0))],
        out_specs=pl.BlockSpec((tm, d), lambda i: (i, 0)),
        out_shape=jax.ShapeDtypeStruct((t, d), F32),
        scratch_shapes=[pltpu.VMEM((tm + max(POOL_WINDOWS), n_pool), F32)],
        compiler_params=_cparams(("arbitrary",)),
        name="outproj",
    )(x2, p_all, y_rwkv, mod3, gpost, pool_w, pool_scale, w_out)


def _pad_cols(w, n):
    return jnp.pad(w, ((0, 0), (0, n - w.shape[1])))


def _layer(x2, c, batch, seq, w_ada, b_ada, norm_pre, norm_post, f1g, f1u, f1d, w_in, mu_shift, pool_w,
           pool_scale, w0, w2, a0, a2, g2, k_k, k_a, r_k, lnx_w, lnx_b, w_out, f2g, f2u, f2d):
    d = x2.shape[1]
    pool_width = pool_scale.shape[0]
    rw = w0.shape[0]
    n_sub = norm_pre.shape[0]
    col_tile = 512

    mod3 = _ada(c, w_ada, b_ada).reshape(batch, n_sub * N_MOD, d)
    row = lambda vec: vec.reshape(1, -1)

    x2 = _ffn(x2, mod3, row(norm_pre[0]), row(norm_post[0]), f1g.astype(BF16), f1u.astype(BF16),
              f1d.astype(BF16), sub=0, seq=seq)

    n_lora = w_in.shape[1] - pool_width - 3 * rw
    assert (3 * rw) % col_tile == 0 and n_lora <= col_tile and pool_width == col_tile and rw % LANES == 0
    w_in_b = w_in.astype(BF16)
    w_in_p = jnp.concatenate([w_in_b[:, pool_width:pool_width + 3 * rw],
                              _pad_cols(w_in_b[:, pool_width + 3 * rw:], col_tile),
                              w_in_b[:, :pool_width]], axis=1)
    mu_p = jnp.concatenate([mu_shift, jnp.zeros((col_tile - n_lora + pool_width,), F32)])
    p_all = _inproj(x2, mod3, row(norm_pre[1]), w_in_p, row(mu_p), seq=seq, tm=1024,
                    tn=w_in_p.shape[1] // 4)
    lora_block = 3 * rw // col_tile

    n_w, n_a, n_g = w2.shape[0], a2.shape[0], g2.shape[0]
    assert n_w + n_a == LANES
    w2p = jnp.pad(w2, ((0, n_a), (0, 0))).astype(BF16)
    a2p = jnp.pad(a2, ((n_w, 0), (0, 0))).astype(BF16)
    g2p = jnp.pad(g2, ((0, -n_g % LANES), (0, 0))).astype(BF16)
    y_rwkv = _scan(p_all, lora_block, col_tile, w2p, a2p, g2p, row(w0), row(a0), row(k_k), row(k_a), row(r_k),
                   row(lnx_w), row(lnx_b), batch=batch, seq=seq, r_block=0, k_block=rw // LANES,
                   v_block=2 * rw // LANES)


    x2 = _outproj(x2, p_all, lora_block + 1, y_rwkv, mod3, row(norm_post[1]), pool_w.astype(BF16),
                  row(pool_scale), w_out.astype(BF16), seq=seq)

    x2 = _ffn(x2, mod3, row(norm_pre[2]), row(norm_post[2]), f2g.astype(BF16), f2u.astype(BF16),
              f2d.astype(BF16), sub=2, seq=seq)
    return x2


def kernel(x, c, w_ada, b_ada, norm_pre, norm_post, ffn1_w_gate, ffn1_w_up, ffn1_w_down, w_in, mu_shift,
           pool_w, pool_scale, w0, w2, a0, a2, g2, k_k, k_a, r_k, lnx_w, lnx_b, w_out, ffn2_w_gate,
           ffn2_w_up, ffn2_w_down):
    batch, seq, d = x.shape
    x2 = x.reshape(batch * seq, d)
    for l in range(w_ada.shape[0]):
        x2 = _layer(x2, c, batch, seq, w_ada[l], b_ada[l], norm_pre[l], norm_post[l], ffn1_w_gate[l],
                    ffn1_w_up[l], ffn1_w_down[l], w_in[l], mu_shift[l], pool_w[l], pool_scale[l], w0[l],
                    w2[l], a0[l], a2[l], g2[l], k_k[l], k_a[l], r_k[l], lnx_w[l], lnx_b[l], w_out[l],
                    ffn2_w_gate[l], ffn2_w_up[l], ffn2_w_down[l])
    return x2.reshape(batch, seq, d)
```

```python
import functools

import jax
import jax.numpy as jnp
from jax import lax
from jax.experimental import pallas as pl
from jax.experimental.pallas import tpu as pltpu

F32 = jnp.float32
BF16 = jnp.bfloat16

NORM_EPS = 1e-6
HEAD_SIZE = 64
LN_X_EPS = 1e-5 * HEAD_SIZE
POOL_WINDOWS = (2, 4, 8, 16)
POOL_GROUP = 128
MACARON_WEIGHT = 0.5
N_MOD = 3
DECAY_SCALE = 0.6065306597126334

LANES = 128
CHUNK = 64
CHUNK_LAG = 8
MAPS_SWEEPS = (3, 19)
ROW_BLOCK = 16
VMEM_LIMIT_MIB = 56


def _cparams(sem, vmem_mib=VMEM_LIMIT_MIB):
    return pltpu.CompilerParams(dimension_semantics=sem, vmem_limit_bytes=vmem_mib * 1024 * 1024)


def _sigmoid(z):
    return 0.5 * (1.0 + jnp.tanh(0.5 * z))


def _row_blocks(n_rows, fn):
    for b in range(n_rows // ROW_BLOCK):
        fn(pl.ds(b * ROW_BLOCK, ROW_BLOCK))


def _norm_mod_into(h_ref, x_ref, gain_ref, mod_ref, sub):
    shift = mod_ref[0, N_MOD * sub:N_MOD * sub + 1, :]
    scale = mod_ref[0, N_MOD * sub + 1:N_MOD * sub + 2, :]
    mult = gain_ref[...] * (1.0 + scale)
    blocks = []

    def block(rows):
        x = x_ref[rows, :]
        inv = lax.rsqrt(jnp.mean(x * x, axis=-1, keepdims=True) + NORM_EPS)
        blocks.append((x * inv * mult + shift).astype(BF16))
        h_ref[rows, :] = blocks[-1]

    _row_blocks(x_ref.shape[0], block)
    return jnp.concatenate(blocks, axis=0)


def _residual_into(o_ref, x_ref, gain_ref, mod_ref, sub, weight, y_all):
    gate = mod_ref[0, N_MOD * sub + 2:N_MOD * sub + 3, :]
    mult = gain_ref[...] * (weight * (1.0 + gate))

    def block(rows):
        y = y_all[rows.start:rows.start + rows.size, :]
        inv = lax.rsqrt(jnp.mean(y * y, axis=-1, keepdims=True) + NORM_EPS)
        o_ref[rows, :] = x_ref[rows, :] + y * inv * mult

    _row_blocks(o_ref.shape[0], block)


def _bdot(a, b):
    return jnp.dot(a.astype(BF16), b.astype(BF16), preferred_element_type=F32)


def _ada_kernel(c_ref, w_ref, b_ref, o_ref):
    c = c_ref[...]
    s = c * _sigmoid(c)
    o_ref[...] = jnp.dot(s, w_ref[...], preferred_element_type=F32,
                         precision=lax.Precision.HIGHEST) + b_ref[...]


def _ada(c, w, b, tn=1024):
    bsz, d = c.shape
    n = w.shape[1]
    return pl.pallas_call(
        _ada_kernel,
        grid=(n // tn,),
        in_specs=[pl.BlockSpec((bsz, d), lambda j: (0, 0)),
                  pl.BlockSpec((d, tn), lambda j: (0, j)),
                  pl.BlockSpec((1, tn), lambda j: (0, j))],
        out_specs=pl.BlockSpec((bsz, tn), lambda j: (0, j)),
        out_shape=jax.ShapeDtypeStruct((bsz, n), F32),
        compiler_params=_cparams(("arbitrary",)),
        name="ada",
    )(c, w, b.reshape(1, n))


def _ffn_kernel(x_ref, mod_ref, gpre_ref, gpost_ref, wg_ref, wu_ref, wd_ref, o_ref, h_ref, *, sub):
    j = pl.program_id(1)
    last = pl.num_programs(1) - 1

    def chunk(h):
        g = jnp.dot(h, wg_ref[...], preferred_element_type=F32)
        u = jnp.dot(h, wu_ref[...], preferred_element_type=F32)
        act = (g * _sigmoid(g) * u).astype(BF16)
        return jnp.dot(act, wd_ref[...], preferred_element_type=F32)

    @pl.when(j == 0)
    def _():
        o_ref[...] = chunk(_norm_mod_into(h_ref, x_ref, gpre_ref, mod_ref, sub))

    @pl.when((j > 0) & (j < last))
    def _():
        o_ref[...] += chunk(h_ref[...])

    @pl.when(j == last)
    def _():
        _residual_into(o_ref, x_ref, gpost_ref, mod_ref, sub, MACARON_WEIGHT,
                       y_all=o_ref[...] + chunk(h_ref[...]))


def _ffn(x2, mod3, gpre, gpost, wg, wu, wd, *, sub, seq, tm=1024, tf=512):
    t, d = x2.shape
    f = wg.shape[1]
    tps = seq // tm
    assert seq % tm == 0 and f % tf == 0 and f // tf >= 2
    spill_mib = 5
    vmem_mib = (2 * 2 * tm * d * 4 + tm * d * 2 + 3 * 2 * d * tf * 2 + tm * tf * (4 + 4 + 2)) // 2**20 + spill_mib
    return pl.pallas_call(
        functools.partial(_ffn_kernel, sub=sub),
        grid=(t // tm, f // tf),
        in_specs=[pl.BlockSpec((tm, d), lambda i, j: (i, 0)),
                  pl.BlockSpec((1,) + mod3.shape[1:], lambda i, j: (i // tps, 0, 0)),
                  pl.BlockSpec((1, d), lambda i, j: (0, 0)),
                  pl.BlockSpec((1, d), lambda i, j: (0, 0)),
                  pl.BlockSpec((d, tf), lambda i, j: (0, j)),
                  pl.BlockSpec((d, tf), lambda i, j: (0, j)),
                  pl.BlockSpec((tf, d), lambda i, j: (j, 0))],
        out_specs=pl.BlockSpec((tm, d), lambda i, j: (i, 0)),
        out_shape=jax.ShapeDtypeStruct((t, d), F32),
        scratch_shapes=[pltpu.VMEM((tm, d), BF16)],
        compiler_params=_cparams(("arbitrary", "arbitrary"), vmem_mib),
        name=f"ffn{sub}",
    )(x2, mod3, gpre, gpost, wg, wu, wd)


def _inproj_kernel(x_ref, mod_ref, gpre_ref, w_ref, mu_ref, o_ref, h_ref, carry_ref, *, tps):
    i = pl.program_id(0)
    j = pl.program_id(1)

    def project(h):
        res = jnp.dot(h, w_ref[...], preferred_element_type=F32)
        tm, tn = res.shape
        carried = carry_ref[j][0:1, :]
        first = jnp.where(i % tps == 0, jnp.zeros_like(carried), carried)
        row = lax.broadcasted_iota(jnp.int32, (tm, 1), 0)
        prev = jnp.where(row == 0, first, pltpu.roll(res, 1, 0))
        carry_ref[j] = jnp.broadcast_to(res[tm - 1:tm, :], (8, tn))
        o_ref[...] = res + mu_ref[...] * (prev - res)

    @pl.when(j == 0)
    def _():
        project(_norm_mod_into(h_ref, x_ref, gpre_ref, mod_ref, 1))

    @pl.when(j > 0)
    def _():
        project(h_ref[...])


def _inproj(x2, mod3, gpre, w, mu, *, seq, tm=512, tn=512):
    t, d = x2.shape
    n = w.shape[1]
    tps = seq // tm
    assert seq % tm == 0 and n % tn == 0
    return pl.pallas_call(
        functools.partial(_inproj_kernel, tps=tps),
        grid=(t // tm, n // tn),
        in_specs=[pl.BlockSpec((tm, d), lambda i, j: (i, 0)),
                  pl.BlockSpec((1,) + mod3.shape[1:], lambda i, j: (i // tps, 0, 0)),
                  pl.BlockSpec((1, d), lambda i, j: (0, 0)),
                  pl.BlockSpec((d, tn), lambda i, j: (0, j)),
                  pl.BlockSpec((1, tn), lambda i, j: (0, j))],
        out_specs=pl.BlockSpec((tm, tn), lambda i, j: (i, j)),
        out_shape=jax.ShapeDtypeStruct((t, n), F32),
        scratch_shapes=[pltpu.VMEM((tm, d), BF16), pltpu.VMEM((n // tn, 8, tn), F32)],
        compiler_params=_cparams(("arbitrary", "arbitrary")),
        name="inproj",
    )(x2, mod3, gpre, w, mu)


def _low_rank_stage1(p_ref, w2_ref, a2_ref, g2_ref):
    wa = p_ref[:, :LANES]
    gx = p_ref[:, LANES:LANES + g2_ref.shape[0]]
    return (_bdot(jnp.tanh(wa), w2_ref[...]), _bdot(wa, a2_ref[...]), _bdot(_sigmoid(gx), g2_ref[...]))


def _low_rank_stage2(dots, w0_ref, a0_ref):
    zw, za, g = dots
    return -DECAY_SCALE * _sigmoid(w0_ref[...] + zw), _sigmoid(a0_ref[...] + za), g


def _scan_kernel(r_ref, k_ref, v_ref, pc_ref, pn_ref, w2_ref, a2_ref, g2_ref, w0_ref, a0_ref,
                 kkw_ref, ka_ref, rk_ref, lw_ref, lb_ref, o_ref, h_ref, ld_ref, a_ref, g_ref):
    C = CHUNK
    P = LANES
    N = HEAD_SIZE

    @pl.when(pl.program_id(2) == 0)
    def _():
        h_ref[...] = jnp.zeros_like(h_ref)
        ld0, a0v, g0 = _low_rank_stage2(_low_rank_stage1(pc_ref, w2_ref, a2_ref, g2_ref), w0_ref, a0_ref)
        ld_ref[...] = ld0
        a_ref[...] = a0v
        g_ref[...] = g0

    lane = lax.broadcasted_iota(jnp.int32, (1, P), 1)
    in_a = lane < N
    ri = lax.broadcasted_iota(jnp.int32, (2 * C, 2 * C), 0)
    ci = lax.broadcasted_iota(jnp.int32, (2 * C, 2 * C), 1)

    def same(b):
        sh = b.bit_length() - 1
        return (ri >> sh) == (ci >> sh)

    strict = (ri > ci) & same(C)
    incl = (ri >= ci) & same(C)
    eye = ri == ci
    blk16 = strict & same(16)
    blk32 = strict & same(32) & jnp.logical_not(same(16))
    blk64 = strict & jnp.logical_not(same(32))
    eye_f = eye.astype(F32)
    tri3 = (lax.broadcasted_iota(jnp.int32, (C, 3 * C), 0)
            >= (lax.broadcasted_iota(jnp.int32, (C, 3 * C), 1) & (C - 1))).astype(BF16)

    def stack(x):
        return jnp.concatenate([jnp.where(in_a, x, 0.0), jnp.where(in_a, 0.0, x)], axis=0)

    def headsum(x):
        first = jnp.sum(jnp.where(in_a, x, 0.0), axis=-1, keepdims=True)
        second = jnp.sum(jnp.where(in_a, 0.0, x), axis=-1, keepdims=True)
        return jnp.where(in_a, first, second)

    def pair_chunk(c, q):
        sl = pl.ds(c * C, C)
        ln = pl.ds(q * P, P)
        kkw = kkw_ref[:, ln]
        ka = ka_ref[:, ln]
        rk = rk_ref[:, ln]
        lw = lw_ref[:, ln]
        lb = lb_ref[:, ln]
        r = r_ref[sl, ln]
        k = k_ref[sl, ln]
        v = v_ref[sl, ln]
        ld = ld_ref[sl, ln]
        a = a_ref[sl, ln]

        kkraw = k * kkw
        kmod = k * (1.0 + (a - 1.0) * ka)
        sums = headsum(jnp.concatenate([kkraw * kkraw, r * kmod * rk], axis=0))
        ld_hi = ld.astype(BF16)
        rem = ld - ld_hi.astype(F32)
        ld_mid = rem.astype(BF16)
        ld_lo = (rem - ld_mid.astype(F32)).astype(BF16)
        cum = jnp.dot(tri3, jnp.concatenate([ld_hi, ld_mid, ld_lo], axis=0), preferred_element_type=F32)
        yield
        kk = kkraw / jnp.maximum(jnp.sqrt(sums[:C]), 1e-12)
        bonus = sums[C:]
        cum_last = cum[C - 1:C, :]

        inv_g = jnp.exp(-cum)
        kb = kk * a
        rt = r * jnp.exp(cum)
        at = -kk * jnp.exp(cum - ld)
        bt = kb * inv_g
        kt = kmod * inv_g
        g_end = jnp.exp(cum_last)
        to_end = g_end * inv_g

        s_v = stack(v).astype(BF16)
        lhs = jnp.concatenate([stack(at), stack(rt)], axis=0).astype(BF16)
        rhs = jnp.concatenate([stack(bt), stack(kt)], axis=0).astype(BF16)
        scores = lax.dot_general(lhs, rhs, (((1,), (1,)), ((), ())), preferred_element_type=F32)
        s_at = lhs[:2 * C]
        s_rt = lhs[2 * C:]
        yield
        s_ab = scores[:2 * C, :2 * C]
        s_ak = jnp.where(strict, scores[:2 * C, 2 * C:], 0.0)
        s_rb = jnp.where(incl, scores[2 * C:, :2 * C], 0.0)
        s_rk = jnp.where(incl, scores[2 * C:, 2 * C:], 0.0)

        x1 = jnp.where(blk16, s_ab, 0.0)
        x2 = _bdot(x1, x1)
        yield
        x4 = _bdot(x2, x2)
        tinv = eye_f + x1
        tinv = tinv + _bdot(tinv, x2)
        yield
        x8 = _bdot(x4, x4)
        tinv = tinv + _bdot(tinv, x4)
        yield
        tinv = tinv + _bdot(tinv, x8)
        yield
        for blk in (blk32, blk64):
            off = jnp.where(blk, s_ab, 0.0)
            part = _bdot(tinv, off)
            yield
            tinv = tinv + _bdot(part, tinv)
            yield

        h = h_ref[q]
        h_b = h.astype(BF16)
        rhs_sa = jnp.dot(jnp.concatenate([s_at, s_ak.astype(BF16)], axis=1),
                         jnp.concatenate([h_b, s_v], axis=0), preferred_element_type=F32)
        yield
        s_sa = _bdot(tinv, rhs_sa)
        yield
        sa_v = jnp.concatenate([s_sa.astype(BF16), s_v], axis=0)
        y_s = jnp.dot(jnp.concatenate([s_rt, s_rb.astype(BF16), s_rk.astype(BF16)], axis=1),
                      jnp.concatenate([h_b, sa_v], axis=0), preferred_element_type=F32)
        upd_l = jnp.concatenate([stack(kb * to_end), stack(kmod * to_end)], axis=0).astype(BF16)
        upd = lax.dot_general(upd_l, sa_v, (((0,), (0,)), ((), ())), preferred_element_type=F32)
        yield
        y = y_s[:C] + y_s[C:]
        g_col = jnp.sum(jnp.where(eye, g_end, 0.0), axis=1, keepdims=True)
        h_ref[q] = h * g_col + upd

        mean = headsum(y) * (1.0 / N)
        yield
        dev = y - mean
        var = headsum(dev * dev) * (1.0 / N)
        yield
        yn = dev * lax.rsqrt(var + LN_X_EPS) * lw + lb
        o_ref[sl, ln] = ((yn + bonus * v) * g_ref[sl, ln]).astype(o_ref.dtype)

    n_chunks = r_ref.shape[0] // C
    chains = [[pair_chunk(c, q) for q in range(r_ref.shape[1] // P)] for c in range(n_chunks)]
    next_maps = {}
    sweep = 0
    while any(chains):
        if sweep == MAPS_SWEEPS[0]:
            next_maps["dots"] = _low_rank_stage1(pn_ref, w2_ref, a2_ref, g2_ref)
        if sweep == MAPS_SWEEPS[1]:
            next_maps["out"] = _low_rank_stage2(next_maps["dots"], w0_ref, a0_ref)
        for c in range(min(n_chunks, sweep // CHUNK_LAG + 1)):
            advanced = []
            for gen in chains[c]:
                try:
                    next(gen)
                    advanced.append(gen)
                except StopIteration:
                    pass
            chains[c] = advanced
        sweep += 1
    assert sweep > MAPS_SWEEPS[1]
    ld_ref[...], a_ref[...], g_ref[...] = next_maps["out"]


def _scan(p_all, lora_block, col_tile, w2p, a2p, g2p, w0, a0, kkw, ka, rk, lw, lb, *, batch, seq, r_block,
          k_block, v_block, tb=4 * CHUNK, pairs=12):
    t = p_all.shape[0]
    rw = kkw.shape[1]
    width = pairs * LANES
    tpb = seq // tb
    assert seq % tb == 0 and tb % CHUNK == 0 and rw == width and CHUNK_LAG >= 3
    tok = lambda blk0: pl.BlockSpec((tb, width), lambda b, p, s: (b * tpb + s, blk0 // pairs + p))
    par = pl.BlockSpec((1, width), lambda b, p, s: (0, p))
    full = lambda arr: pl.BlockSpec(arr.shape, lambda b, p, s: (0, 0))
    maps_now = pl.BlockSpec((tb, col_tile), lambda b, p, s: (b * tpb + s, lora_block))
    maps_next = pl.BlockSpec((tb, col_tile), lambda b, p, s: (b * tpb + jnp.minimum(s + 1, tpb - 1), lora_block))
    return pl.pallas_call(
        _scan_kernel,
        grid=(batch, rw // width, tpb),
        in_specs=[tok(r_block), tok(k_block), tok(v_block), maps_now, maps_next,
                  full(w2p), full(a2p), full(g2p), full(w0), full(a0), par, par, par, par, par],
        out_specs=tok(0),
        out_shape=jax.ShapeDtypeStruct((t, rw), BF16),
        scratch_shapes=[pltpu.VMEM((pairs, LANES, LANES), F32)] + [pltpu.VMEM((tb, rw), F32)] * 3,
        compiler_params=_cparams(("arbitrary", "arbitrary", "arbitrary")),
        name="scan",
    )(p_all, p_all, p_all, p_all, p_all, w2p, a2p, g2p, w0, a0, kkw, ka, rk, lw, lb)


def _pooled(u_ref, pw_ref, ps_ref, ext_ref, t0):
    tm = u_ref.shape[0]
    pad = max(POOL_WINDOWS)
    ext_ref[pad:pad + tm, :] = u_ref[...]
    tpos = (t0 + 1 + lax.broadcasted_iota(jnp.int32, (tm, 1), 0)).astype(F32)
    groups = []
    for gi, win in enumerate(POOL_WINDOWS):
        lo = gi * POOL_GROUP
        acc = ext_ref[pad:pad + tm, lo:lo + POOL_GROUP]
        u_g = acc
        for dlt in range(1, win):
            acc = acc + ext_ref[pad - dlt:pad - dlt + tm, lo:lo + POOL_GROUP]
        pooled = acc / jnp.minimum(tpos, float(win)) - u_g
        groups.append((_bdot(pooled, pw_ref[gi]) * ps_ref[:, lo:lo + POOL_GROUP]).astype(BF16))
    ext_ref[0:pad, :] = ext_ref[tm:tm + pad, :]
    return jnp.concatenate(groups, axis=1)


def _outproj_kernel(x_ref, u_ref, yr_ref, mod_ref, gpost_ref, pw_ref, ps_ref, wo_ref, o_ref, ext_ref, *, tps):
    i = pl.program_id(0)
    tm = x_ref.shape[0]
    n_pool = u_ref.shape[1]

    @pl.when(i % tps == 0)
    def _():
        ext_ref[0:max(POOL_WINDOWS), :] = jnp.zeros((max(POOL_WINDOWS), n_pool), F32)

    y = jnp.dot(yr_ref[...], wo_ref[n_pool:, :], preferred_element_type=F32)
    y_pool = _pooled(u_ref, pw_ref, ps_ref, ext_ref, (i % tps) * tm)
    y = y + jnp.dot(y_pool, wo_ref[:n_pool, :], preferred_element_type=F32)
    _residual_into(o_ref, x_ref, gpost_ref, mod_ref, 1, 1.0, y_all=y)


def _outproj(x2, p_all, pool_block, y_rwkv, mod3, gpost, pool_w, pool_scale, w_out, *, seq, tm=512):
    t, d = x2.shape
    n_pool = pool_scale.shape[1]
    tps = seq // tm
    assert seq % tm == 0 and n_pool == len(POOL_WINDOWS) * POOL_GROUP
    return pl.pallas_call(
        functools.partial(_outproj_kernel, tps=tps),
        grid=(t // tm,),
        in_specs=[pl.BlockSpec((tm, d), lambda i: (i, 0)),
                  pl.BlockSpec((tm, n_pool), lambda i: (i, pool_block)),
                  pl.BlockSpec((tm, y_rwkv.shape[1]), lambda i: (i, 0)),
                  pl.BlockSpec((1,) + mod3.shape[1:], lambda i: (i // tps, 0, 0)),
                  pl.BlockSpec((1, d), lambda i: (0, 0)),
                  pl.BlockSpec(pool_w.shape, lambda i: (0, 0, 0)),
                  pl.BlockSpec((1, n_pool), lambda i: (0, 0)),
                  pl.BlockSpec(w_out.shape, lambda i: (0, 0))],
        out_specs=pl.BlockSpec((tm, d), lambda i: (i, 0)),
        out_shape=jax.ShapeDtypeStruct((t, d), F32),
        scratch_shapes=[pltpu.VMEM((tm + max(POOL_WINDOWS), n_pool), F32)],
        compiler_params=_cparams(("arbitrary",)),
        name="outproj",
    )(x2, p_all, y_rwkv, mod3, gpost, pool_w, pool_scale, w_out)


def _pad_cols(w, n):
    return jnp.pad(w, ((0, 0), (0, n - w.shape[1])))


def _layer(x2, c, batch, seq, w_ada, b_ada, norm_pre, norm_post, f1g, f1u, f1d, w_in, mu_shift, pool_w,
           pool_scale, w0, w2, a0, a2, g2, k_k, k_a, r_k, lnx_w, lnx_b, w_out, f2g, f2u, f2d):
    d = x2.shape[1]
    pool_width = pool_scale.shape[0]
    rw = w0.shape[0]
    n_sub = norm_pre.shape[0]
    col_tile = 512

    mod3 = _ada(c, w_ada, b_ada).reshape(batch, n_sub * N_MOD, d)
    row = lambda vec: vec.reshape(1, -1)

    x2 = _ffn(x2, mod3, row(norm_pre[0]), row(norm_post[0]), f1g.astype(BF16), f1u.astype(BF16),
              f1d.astype(BF16), sub=0, seq=seq)

    n_lora = w_in.shape[1] - pool_width - 3 * rw
    assert (3 * rw) % col_tile == 0 and n_lora <= col_tile and pool_width == col_tile and rw % LANES == 0
    w_in_b = w_in.astype(BF16)
    w_in_p = jnp.concatenate([w_in_b[:, pool_width:pool_width + 3 * rw],
                              _pad_cols(w_in_b[:, pool_width + 3 * rw:], col_tile),
                              w_in_b[:, :pool_width]], axis=1)
    mu_p = jnp.concatenate([mu_shift, jnp.zeros((col_tile - n_lora + pool_width,), F32)])
    p_all = _inproj(x2, mod3, row(norm_pre[1]), w_in_p, row(mu_p), seq=seq, tm=1024,
                    tn=w_in_p.shape[1] // 4)
    lora_block = 3 * rw // col_tile

    n_w, n_a, n_g = w2.shape[0], a2.shape[0], g2.shape[0]
    assert n_w + n_a == LANES
    w2p = jnp.pad(w2, ((0, n_a), (0, 0))).astype(BF16)
    a2p = jnp.pad(a2, ((n_w, 0), (0, 0))).astype(BF16)
    g2p = jnp.pad(g2, ((0, -n_g % LANES), (0, 0))).astype(BF16)
    y_rwkv = _scan(p_all, lora_block, col_tile, w2p, a2p, g2p, row(w0), row(a0), row(k_k), row(k_a), row(r_k),
                   row(lnx_w), row(lnx_b), batch=batch, seq=seq, r_block=0, k_block=rw // LANES,
                   v_block=2 * rw // LANES)


    x2 = _outproj(x2, p_all, lora_block + 1, y_rwkv, mod3, row(norm_post[1]), pool_w.astype(BF16),
                  row(pool_scale), w_out.astype(BF16), seq=seq)

    x2 = _ffn(x2, mod3, row(norm_pre[2]), row(norm_post[2]), f2g.astype(BF16), f2u.astype(BF16),
              f2d.astype(BF16), sub=2, seq=seq)
    return x2


def kernel(x, c, w_ada, b_ada, norm_pre, norm_post, ffn1_w_gate, ffn1_w_up, ffn1_w_down, w_in, mu_shift,
           pool_w, pool_scale, w0, w2, a0, a2, g2, k_k, k_a, r_k, lnx_w, lnx_b, w_out, ffn2_w_gate,
           ffn2_w_up, ffn2_w_down):
    batch, seq, d = x.shape
    x2 = x.reshape(batch * seq, d)
    for l in range(w_ada.shape[0]):
        x2 = _layer(x2, c, batch, seq, w_ada[l], b_ada[l], norm_pre[l], norm_post[l], ffn1_w_gate[l],
                    ffn1_w_up[l], ffn1_w_down[l], w_in[l], mu_shift[l], pool_w[l], pool_scale[l], w0[l],
                    w2[l], a0[l], a2[l], g2[l], k_k[l], k_a[l], r_k[l], lnx_w[l], lnx_b[l], w_out[l],
                    ffn2_w_gate[l], ffn2_w_up[l], ffn2_w_down[l])
    return x2.reshape(batch, seq, d)
```
